```python
import jax, jax.numpy as jnp
from jax import lax
import numpy as np


D_MODEL = 2048
BATCH = 16
SEQ = 256
DEPTH = 4
DEC_BATCH = 4
DEC_SEQ = 4096
PAST_LEN = 256

GRID_W = 64
HEAD_DIM = 128
HEADS_A = 6
KV_HEADS_A = 2
HEADS_B = 5
Q_RANK_B = 512
KV_RANK_B = 256
NOPE_DIM_B = 128
ROPE_DIM_B = 64
V_DIM_B = 128
HEADS_C = 5
WIN_ROWS_MAX = 8
WIN_COLS = 16
N_EXPERTS = 16
EXPERT_FF = 1024
CAPACITY_FACTOR = 2
Q_BLOCK = 128
ROPE_THETA = 10000.0
EPS = 1e-6

Q_A_COLS = HEADS_A * HEAD_DIM
KV_A_COLS = KV_HEADS_A * HEAD_DIM
WIDTH_A = HEADS_A * HEAD_DIM
WIDTH_B = HEADS_B * V_DIM_B
WIDTH_C = HEADS_C * HEAD_DIM
MIX_WIDTH = WIDTH_A + WIDTH_B + WIDTH_C
IN_COLS = Q_A_COLS + 2 * KV_A_COLS + Q_RANK_B + KV_RANK_B + ROPE_DIM_B + 3 * WIDTH_C

kernel_name = 'hybrid_gqa_mla_natten_ec_moe_diffusion_step'


def rms_norm(x, g):
    xf = x.astype(jnp.float32)
    y = xf * lax.rsqrt(jnp.mean(xf * xf, axis=-1, keepdims=True) + EPS)
    return (y * g.astype(jnp.float32)).astype(x.dtype)


def modulation(cond, w_ada_l, b_ada_l):
    m = jax.nn.silu(cond) @ w_ada_l + b_ada_l
    m = m.reshape(m.shape[:-1] + (1, 6, D_MODEL))
    return [m[..., i, :] for i in range(6)]


def modulated_norm(x, g, shift, scale):
    return rms_norm(x, g) * (1 + scale) + shift


def split_heads(t, n_heads):
    return t.reshape(t.shape[:-1] + (n_heads, t.shape[-1] // n_heads))


def axial_rope_tables(n_tokens, rot_dim):
    t = jnp.arange(n_tokens, dtype=jnp.int32)
    row = (t // GRID_W).astype(jnp.float32)
    col = (t % GRID_W).astype(jnp.float32)
    axis_dim = rot_dim // 2
    freqs = ROPE_THETA ** (-jnp.arange(0, axis_dim, 2, dtype=jnp.float32) / axis_dim)
    ang = jnp.concatenate([row[:, None] * freqs[None, :], col[:, None] * freqs[None, :]], axis=-1)
    return jnp.cos(ang), jnp.sin(ang)


def apply_rope(x, cos, sin):
    xf = x.astype(jnp.float32).reshape(x.shape[:-1] + (x.shape[-1] // 2, 2))
    x1, x2 = xf[..., 0], xf[..., 1]
    c = cos[None, :, None, :]
    s = sin[None, :, None, :]
    out = jnp.stack([x1 * c - x2 * s, x1 * s + x2 * c], axis=-1)
    return out.reshape(x.shape).astype(x.dtype)


def block_attention(q, k, v):
    b, n, h, dq = q.shape
    kvh = k.shape[2]
    g = h // kvh
    dv = v.shape[-1]
    scale = dq ** -0.5
    nb = n // Q_BLOCK
    qb = jnp.moveaxis(q.reshape(b, nb, Q_BLOCK, kvh, g, dq), 1, 0)

    def one_block(qblk):
        s = jnp.einsum('bqkgd,bmkd->bkgqm', qblk, k, preferred_element_type=jnp.float32) * scale
        p = jax.nn.softmax(s, axis=-1).astype(v.dtype)
        return jnp.einsum('bkgqm,bmkd->bqkgd', p, v)

    o = lax.map(one_block, qb)
    return jnp.moveaxis(o, 0, 1).reshape(b, n, h, dv)


def neighbourhood_attention(q, k, v, k_ctx, v_ctx, rel_bias):
    b, n, h, d = q.shape
    rows = n // GRID_W
    kh = min(WIN_ROWS_MAX, rows)
    kw = WIN_COLS
    scale = d ** -0.5
    kg = k.reshape(b, rows, GRID_W, h, d)
    vg = v.reshape(b, rows, GRID_W, h, d)
    qg = jnp.moveaxis(q.reshape(b, rows, GRID_W, h, d), 1, 0)
    cols = jnp.arange(GRID_W, dtype=jnp.int32)
    col_start = jnp.clip(cols - kw // 2, 0, GRID_W - kw)
    col_idx = col_start[:, None] + jnp.arange(kw, dtype=jnp.int32)[None, :]
    dc = col_idx - cols[:, None] + (WIN_COLS - 1)
    n_win = kh * kw

    def one_row(args):
        r, q_row = args
        r0 = jnp.clip(r - kh // 2, 0, rows - kh)
        k_win = lax.dynamic_slice_in_dim(kg, r0, kh, axis=1)[:, :, col_idx]
        v_win = lax.dynamic_slice_in_dim(vg, r0, kh, axis=1)[:, :, col_idx]
        dr = r0 + jnp.arange(kh, dtype=jnp.int32) - r + (WIN_ROWS_MAX - 1)
        bias = jnp.transpose(rel_bias[:, dr][:, :, dc], (0, 2, 1, 3))
        s_win = jnp.einsum('bchd,bicjhd->bhcij', q_row, k_win, preferred_element_type=jnp.float32) * scale
        s_win = s_win + bias[None].astype(jnp.float32)
        s_ctx = jnp.einsum('bchd,bmhd->bhcm', q_row, k_ctx, preferred_element_type=jnp.float32) * scale
        s = jnp.concatenate([s_win.reshape(b, h, GRID_W, n_win), s_ctx], axis=-1)
        p = jax.nn.softmax(s, axis=-1).astype(v.dtype)
        p_win = p[..., :n_win].reshape(b, h, GRID_W, kh, kw)
        p_ctx = p[..., n_win:]
        return jnp.einsum('bhcij,bicjhd->bchd', p_win, v_win) + jnp.einsum('bhcm,bmhd->bchd', p_ctx, v_ctx)

    o = lax.map(one_row, (jnp.arange(rows, dtype=jnp.int32), qg))
    return jnp.moveaxis(o, 0, 1).reshape(b, n, h, d)


def split_projection(p):
    sizes = [Q_A_COLS, KV_A_COLS, KV_A_COLS, Q_RANK_B, KV_RANK_B, ROPE_DIM_B, WIDTH_C, WIDTH_C, WIDTH_C]
    bounds = [int(s) for s in np.cumsum(sizes)[:-1]]
    return jnp.split(p, bounds, axis=-1)


def mla_queries(cq, g_q, w_q_up_l, rope):
    cq = rms_norm(cq, g_q)
    q = split_heads(cq @ w_q_up_l, HEADS_B)
    q_nope, q_rope = q[..., :NOPE_DIM_B], q[..., NOPE_DIM_B:]
    if rope is not None:
        q_rope = apply_rope(q_rope, rope[0], rope[1])
    return jnp.concatenate([q_nope, q_rope], axis=-1)


def mla_keys_values(ckv, krope, w_kv_up_l):
    kv = split_heads(ckv @ w_kv_up_l, HEADS_B)
    k_nope, v = kv[..., :NOPE_DIM_B], kv[..., NOPE_DIM_B:]
    k_rope = jnp.broadcast_to(krope[:, :, None, :], k_nope.shape[:-1] + (ROPE_DIM_B,))
    return jnp.concatenate([k_nope, k_rope], axis=-1), v


def merge_heads(o_a, o_b, o_c, w_out_l):
    b, n = o_a.shape[:2]
    o = jnp.concatenate([o_a.reshape(b, n, WIDTH_A), o_b.reshape(b, n, WIDTH_B), o_c.reshape(b, n, WIDTH_C)], axis=-1)
    return o @ w_out_l


def mixers_context(h, lp):
    qa, ka, va, cq, ckv, krope, qc, kc, vc = split_projection(h @ lp['w_in'])
    qa = rms_norm(split_heads(qa, HEADS_A), lp['qa_g'])
    ka = rms_norm(split_heads(ka, KV_HEADS_A), lp['ka_g'])
    va = split_heads(va, KV_HEADS_A)
    o_a = block_attention(qa, ka, va)
    ckv = rms_norm(ckv, lp['kv_norm_b'])
    q_b = mla_queries(cq, lp['q_norm_b'], lp['w_q_up'], None)
    k_b, v_b = mla_keys_values(ckv, krope, lp['w_kv_up'])
    o_b = block_attention(q_b, k_b, v_b)
    kc = split_heads(kc, HEADS_C)
    vc = split_heads(vc, HEADS_C)
    o_c = block_attention(split_heads(qc, HEADS_C), kc, vc)
    return merge_heads(o_a, o_b, o_c, lp['w_out']), (ka, va, ckv, krope, kc, vc)


def mixers_latent(h, lp, cached, rope_a, rope_b):
    ctx_ak, ctx_av, ctx_ckv, ctx_krope, ctx_ck, ctx_cv = cached
    qa, ka, va, cq, ckv, krope, qc, kc, vc = split_projection(h @ lp['w_in'])
    qa = apply_rope(rms_norm(split_heads(qa, HEADS_A), lp['qa_g']), rope_a[0], rope_a[1])
    ka = apply_rope(rms_norm(split_heads(ka, KV_HEADS_A), lp['ka_g']), rope_a[0], rope_a[1])
    va = split_heads(va, KV_HEADS_A)
    o_a = block_attention(qa, jnp.concatenate([ctx_ak, ka], axis=1), jnp.concatenate([ctx_av, va], axis=1))
    ckv = rms_norm(ckv, lp['kv_norm_b'])
    krope = apply_rope(krope[:, :, None, :], rope_b[0], rope_b[1])[:, :, 0, :]
    q_b = mla_queries(cq, lp['q_norm_b'], lp['w_q_up'], rope_b)
    k_b, v_b = mla_keys_values(jnp.concatenate([ctx_ckv, ckv], axis=1), jnp.concatenate([ctx_krope, krope], axis=1), lp['w_kv_up'])
    o_b = block_attention(q_b, k_b, v_b)
    o_c = neighbourhood_attention(split_heads(qc, HEADS_C), split_heads(kc, HEADS_C), split_heads(vc, HEADS_C), ctx_ck, ctx_cv, lp['na_bias'])
    return merge_heads(o_a, o_b, o_c, lp['w_out'])


def expert_choice_ffn(h, w_router_l, w_gate_l, w_up_l, w_down_l):
    b, n, _ = h.shape
    cap = CAPACITY_FACTOR * n // N_EXPERTS
    logits = jnp.einsum('bnd,de->ben', h, w_router_l, preferred_element_type=jnp.float32)
    aff = jax.nn.softmax(logits, axis=1)
    gate, idx = lax.top_k(aff, cap)
    bidx = jnp.arange(b)[:, None, None]
    xe = h[bidx, idx]
    hid = jax.nn.silu(jnp.einsum('becd,edf->becf', xe, w_gate_l)) * jnp.einsum('becd,edf->becf', xe, w_up_l)
    ye = jnp.einsum('becf,efd->becd', hid, w_down_l) * gate[..., None].astype(h.dtype)
    return jnp.zeros_like(h).at[bidx, idx].add(ye)


def setup_inputs(seed: int = 0) -> dict:
    key = jax.random.key(seed)
    ks = jax.random.split(key, 32)
    f = jnp.float32
    nrm = jax.random.normal
    D = D_MODEL
    return {
        'x_prompt': nrm(ks[0], (BATCH, SEQ, D), f),
        'x_sample': nrm(ks[1], (DEC_BATCH, DEC_SEQ, D), f),
        'cache_a_k': nrm(ks[2], (DEC_BATCH, DEPTH, PAST_LEN, KV_HEADS_A, HEAD_DIM), f),
        'cache_a_v': nrm(ks[3], (DEC_BATCH, DEPTH, PAST_LEN, KV_HEADS_A, HEAD_DIM), f),
        'cache_b_ckv': nrm(ks[4], (DEC_BATCH, DEPTH, PAST_LEN, KV_RANK_B), f),
        'cache_b_krope': nrm(ks[5], (DEC_BATCH, DEPTH, PAST_LEN, ROPE_DIM_B), f),
        'cache_c_k': nrm(ks[6], (DEC_BATCH, DEPTH, PAST_LEN, HEADS_C, HEAD_DIM), f),
        'cache_c_v': nrm(ks[7], (DEC_BATCH, DEPTH, PAST_LEN, HEADS_C, HEAD_DIM), f),
        'c': nrm(ks[8], (DEC_BATCH, D), f),
        'c_ctx': nrm(ks[9], (D,), f),
        'w_ada': nrm(ks[10], (DEPTH, D, 6 * D), f) * (0.5 * D ** -0.5),
        'b_ada': nrm(ks[11], (DEPTH, 6 * D), f) * 0.01,
        'norm1_g': 1.0 + 0.05 * nrm(ks[12], (DEPTH, D), f),
        'norm2_g': 1.0 + 0.05 * nrm(ks[13], (DEPTH, D), f),
        'w_in': nrm(ks[14], (DEPTH, D, IN_COLS), f) * D ** -0.5,
        'qa_norm_g': 1.0 + 0.05 * nrm(ks[15], (DEPTH, HEAD_DIM), f),
        'ka_norm_g': 1.0 + 0.05 * nrm(ks[16], (DEPTH, HEAD_DIM), f),
        'q_norm_b': 1.0 + 0.05 * nrm(ks[17], (DEPTH, Q_RANK_B), f),
        'kv_norm_b': 1.0 + 0.05 * nrm(ks[18], (DEPTH, KV_RANK_B), f),
        'w_q_up': nrm(ks[19], (DEPTH, Q_RANK_B, HEADS_B * (NOPE_DIM_B + ROPE_DIM_B)), f) * Q_RANK_B ** -0.5,
        'w_kv_up': nrm(ks[20], (DEPTH, KV_RANK_B, HEADS_B * (NOPE_DIM_B + V_DIM_B)), f) * KV_RANK_B ** -0.5,
        'na_bias': nrm(ks[21], (DEPTH, HEADS_C, 2 * WIN_ROWS_MAX - 1, 2 * WIN_COLS - 1), f) * 0.1,
        'w_out': nrm(ks[22], (DEPTH, MIX_WIDTH, D), f) * MIX_WIDTH ** -0.5,
        'w_router': nrm(ks[23], (DEPTH, D, N_EXPERTS), f) * D ** -0.5,
        'w_gate': nrm(ks[24], (DEPTH, N_EXPERTS, D, EXPERT_FF), f) * D ** -0.5,
        'w_up': nrm(ks[25], (DEPTH, N_EXPERTS, D, EXPERT_FF), f) * D ** -0.5,
        'w_down': nrm(ks[26], (DEPTH, N_EXPERTS, EXPERT_FF, D), f) * EXPERT_FF ** -0.5,
        'final_norm_g': 1.0 + 0.05 * nrm(ks[27], (D,), f),
    }


def reference(x_prompt, x_sample, cache_a_k, cache_a_v, cache_b_ckv, cache_b_krope, cache_c_k, cache_c_v, c, c_ctx, w_ada, b_ada, norm1_g, norm2_g, w_in, qa_norm_g, ka_norm_g, q_norm_b, kv_norm_b, w_q_up, w_kv_up, na_bias, w_out, w_router, w_gate, w_up, w_down, final_norm_g):
    n_lat = x_sample.shape[1]
    rope_a = axial_rope_tables(n_lat, HEAD_DIM)
    rope_b = axial_rope_tables(n_lat, ROPE_DIM_B)
    xp = x_prompt
    xs = x_sample
    states = ([], [], [], [], [], [])
    for l in range(DEPTH):
        lp = {
            'w_in': w_in[l], 'qa_g': qa_norm_g[l], 'ka_g': ka_norm_g[l],
            'q_norm_b': q_norm_b[l], 'kv_norm_b': kv_norm_b[l],
            'w_q_up': w_q_up[l], 'w_kv_up': w_kv_up[l], 'w_out': w_out[l], 'na_bias': na_bias[l],
        }
        mc = modulation(c_ctx, w_ada[l], b_ada[l])
        ml = modulation(c, w_ada[l], b_ada[l])
        h = modulated_norm(xp, norm1_g[l], mc[0], mc[1])
        o, ctx_tensors = mixers_context(h, lp)
        xp = xp + mc[2] * o
        h = modulated_norm(xp, norm2_g[l], mc[3], mc[4])
        xp = xp + mc[5] * expert_choice_ffn(h, w_router[l], w_gate[l], w_up[l], w_down[l])
        for lst, t in zip(states, ctx_tensors):
            lst.append(t)
        cached = (cache_a_k[:, l], cache_a_v[:, l], cache_b_ckv[:, l], cache_b_krope[:, l], cache_c_k[:, l], cache_c_v[:, l])
        h = modulated_norm(xs, norm1_g[l], ml[0], ml[1])
        xs = xs + ml[2] * mixers_latent(h, lp, cached, rope_a, rope_b)
        h = modulated_norm(xs, norm2_g[l], ml[3], ml[4])
        xs = xs + ml[5] * expert_choice_ffn(h, w_router[l], w_gate[l], w_up[l], w_down[l])
    y_prompt = rms_norm(xp, final_norm_g)
    y_sample = rms_norm(xs, final_norm_g)
    new_a_k = jnp.stack(states[0], axis=1)
    new_a_v = jnp.stack(states[1], axis=1)
    new_b_ckv = jnp.stack(states[2], axis=1)
    new_b_krope = jnp.stack(states[3], axis=1)
    new_c_k = jnp.stack(states[4], axis=1)
    new_c_v = jnp.stack(states[5], axis=1)
    return (y_prompt, y_sample, new_a_k, new_a_v, new_b_ckv, new_b_krope, new_c_k, new_c_v)
```

```python
import functools

import jax
import jax.numpy as jnp
from jax import lax
from jax.experimental import pallas as pl
from jax.experimental.pallas import tpu as pltpu

F32 = jnp.float32
BF16 = jnp.bfloat16

GRID_W = 64
HEAD_DIM = 128
HEADS_A = 6
KV_HEADS_A = 2
HEADS_B = 5
Q_RANK_B = 512
KV_RANK_B = 256
NOPE_DIM_B = 128
ROPE_DIM_B = 64
V_DIM_B = 128
HEADS_C = 5
WIN_ROWS_MAX = 8
WIN_COLS = 16
N_EXPERTS = 16
CAPACITY_FACTOR = 2
ROPE_THETA = 10000.0
EPS = 1e-6

QB_PAD = 256
WIDTH_A = HEADS_A * HEAD_DIM
WIDTH_B = HEADS_B * V_DIM_B
WIDTH_C = HEADS_C * HEAD_DIM
KV_A = KV_HEADS_A * HEAD_DIM

SEG_QA = (0, WIDTH_A)
SEG_KA = (SEG_QA[1], SEG_QA[1] + KV_A)
SEG_VA = (SEG_KA[1], SEG_KA[1] + KV_A)
SEG_CQ = (SEG_VA[1], SEG_VA[1] + Q_RANK_B)
SEG_CKV = (SEG_CQ[1], SEG_CQ[1] + KV_RANK_B)
SEG_QC = (SEG_CKV[1], SEG_CKV[1] + WIDTH_C)
SEG_KC = (SEG_QC[1], SEG_QC[1] + WIDTH_C)
SEG_VC = (SEG_KC[1], SEG_KC[1] + WIDTH_C)
SEG_KR = (SEG_VC[1], SEG_VC[1] + 128)
IN_COLS_P = SEG_KR[1]

ST_KA = (0, KV_A)
ST_VA = (ST_KA[1], ST_KA[1] + KV_A)
ST_CKV = (ST_VA[1], ST_VA[1] + KV_RANK_B)
ST_KC = (ST_CKV[1], ST_CKV[1] + WIDTH_C)
ST_VC = (ST_KC[1], ST_KC[1] + WIDTH_C)
ST_KR = (ST_VC[1], ST_VC[1] + 128)
ST_COLS = ST_KR[1]

NAT_QROWS = 4
NAT_KROWS = 12
MASK_VALUE = -1e30

TOKEN_TILE = 256
VMEM_LIMIT = 56 * 1024 * 1024
NT_DIMS = (((1,), (1,)), ((), ()))


def _params(*sem):
    return pltpu.CompilerParams(dimension_semantics=sem, vmem_limit_bytes=VMEM_LIMIT)


def _resident(block_shape, index_map):
    return pl.BlockSpec(block_shape, index_map, pipeline_mode=pl.Buffered(1))


def _rms(x, g):
    ms = jnp.mean(x * x, axis=-1, keepdims=True)
    return x * lax.rsqrt(ms + EPS) * g


def _swap_pairs(x):
    lane = lax.broadcasted_iota(jnp.int32, x.shape, x.ndim - 1)
    nxt = pltpu.roll(x, x.shape[-1] - 1, x.ndim - 1)
    prv = pltpu.roll(x, 1, x.ndim - 1)
    return jnp.where((lane & 1) == 0, nxt, prv)


def _rope(x, cos, sin_signed):
    return x * cos + _swap_pairs(x) * sin_signed


def _mod_body(c_ref, w_ref, b_ref, o_ref):
    c = c_ref[...]
    s = (c / (1.0 + jnp.exp(-c))).astype(BF16)
    o_ref[...] = jnp.dot(s, w_ref[...].astype(BF16), preferred_element_type=F32) + b_ref[...]


def _modulation(cond8, w_ada, b_ada):
    depth, d, cols = w_ada.shape
    tn = 1024
    return pl.pallas_call(
        _mod_body,
        grid=(depth, cols // tn),
        in_specs=[
            pl.BlockSpec((8, d), lambda l, j: (0, 0)),
            pl.BlockSpec((None, d, tn), lambda l, j: (l, 0, j)),
            pl.BlockSpec((None, 1, tn), lambda l, j: (l, 0, j)),
        ],
        out_specs=pl.BlockSpec((None, 8, tn), lambda l, j: (l, 0, j)),
        out_shape=jax.ShapeDtypeStruct((depth, 8, cols), F32),
        compiler_params=_params("parallel", "parallel"),
        name="modulation",
    )(cond8, w_ada, b_ada.reshape(depth, 1, cols))


def _in_body(*refs, rope, state):
    it = iter(refs)
    x_ref, sh_ref, sc_ref, g_ref, win_ref = (next(it) for _ in range(5))
    qag_ref, kag_ref, qng_ref, kvng_ref = (next(it) for _ in range(4))
    wq_ref, wkk_ref, wkv_ref = (next(it) for _ in range(3))
    if rope:
        ca_ref, sa_ref, cb_ref, sb_ref = (next(it) for _ in range(4))
    qa_o, ka_o, va_o, qb_o, kb_o, vb_o, qc_o, kc_o, vc_o = (next(it) for _ in range(9))
    st_o = next(it) if state else None

    h = _rms(x_ref[...], g_ref[...]) * (1.0 + sc_ref[...]) + sh_ref[...]
    hb = h.astype(BF16)

    def seg(bounds):
        return jnp.dot(hb, win_ref[:, bounds[0]:bounds[1]], preferred_element_type=F32)

    def rope_a(y):
        return _rope(y, ca_ref[...], sa_ref[...]) if rope else y

    def rope_b(y):
        return _rope(y, cb_ref[...], sb_ref[...]) if rope else y

    qa = seg(SEG_QA)
    for hd in range(HEADS_A):
        lo = hd * HEAD_DIM
        y = rope_a(_rms(qa[:, lo:lo + HEAD_DIM], qag_ref[...]))
        qa_o[:, lo:lo + HEAD_DIM] = (y * HEAD_DIM ** -0.5).astype(BF16)
    ka = seg(SEG_KA)
    for hd in range(KV_HEADS_A):
        lo = hd * HEAD_DIM
        y = rope_a(_rms(ka[:, lo:lo + HEAD_DIM], kag_ref[...]))
        ka_o[:, lo:lo + HEAD_DIM] = y.astype(BF16)
        if state:
            st_o[:, ST_KA[0] + lo:ST_KA[0] + lo + HEAD_DIM] = y
    va = seg(SEG_VA)
    va_o[...] = va.astype(BF16)

    cq = _rms(seg(SEG_CQ), qng_ref[...]).astype(BF16)
    qb = jnp.dot(cq, wq_ref[...], preferred_element_type=F32)
    qscale = (NOPE_DIM_B + ROPE_DIM_B) ** -0.5
    for hd in range(HEADS_B):
        lo = hd * QB_PAD
        qb_o[:, lo:lo + 128] = (qb[:, lo:lo + 128] * qscale).astype(BF16)
        qb_o[:, lo + 128:lo + 256] = (rope_b(qb[:, lo + 128:lo + 256]) * qscale).astype(BF16)
    ckv = _rms(seg(SEG_CKV), kvng_ref[...])
    ckvb = ckv.astype(BF16)
    kr = rope_b(seg(SEG_KR))
    krb = kr.astype(BF16)
    kn = jnp.dot(ckvb, wkk_ref[...], preferred_element_type=F32)
    for hd in range(HEADS_B):
        kb_o[:, hd * QB_PAD:hd * QB_PAD + 128] = kn[:, hd * 128:(hd + 1) * 128].astype(BF16)
        kb_o[:, hd * QB_PAD + 128:(hd + 1) * QB_PAD] = krb
    vb_o[...] = jnp.dot(ckvb, wkv_ref[...], preferred_element_type=F32).astype(BF16)

    qc_o[...] = (seg(SEG_QC) * HEAD_DIM ** -0.5).astype(BF16)
    kc = seg(SEG_KC)
    kc_o[...] = kc.astype(BF16)
    vc = seg(SEG_VC)
    vc_o[...] = vc.astype(BF16)

    if state:
        st_o[:, ST_VA[0]:ST_VA[1]] = va
        st_o[:, ST_CKV[0]:ST_CKV[1]] = ckv
        st_o[:, ST_KC[0]:ST_KC[1]] = kc
        st_o[:, ST_VC[0]:ST_VC[1]] = vc
        st_o[:, ST_KR[0]:ST_KR[1]] = kr


def _input_side(x, mod, layer, wts, rope_tabs, state):
    b, n, d = x.shape
    tm = TOKEN_TILE
    mb = mod.shape[0]
    bsel = (lambda i: i) if mb > 1 else (lambda i: 0)
    rope = rope_tabs is not None

    def modspec(k):
        return pl.BlockSpec((None, None, 1, d), lambda bi, i: (bsel(bi), k, 0, 0))

    def vec(w):
        return pl.BlockSpec((None, 1, w), lambda bi, i: (layer, 0, 0))

    in_specs = [
        pl.BlockSpec((None, tm, d), lambda bi, i: (bi, i, 0)),
        modspec(0), modspec(1), vec(d),
        _resident((None, d, IN_COLS_P), lambda bi, i: (layer, 0, 0)),
        vec(HEAD_DIM), vec(HEAD_DIM), vec(Q_RANK_B), vec(KV_RANK_B),
        _resident((None, Q_RANK_B, HEADS_B * QB_PAD), lambda bi, i: (layer, 0, 0)),
        _resident((None, KV_RANK_B, HEADS_B * 128), lambda bi, i: (layer, 0, 0)),
        _resident((None, KV_RANK_B, HEADS_B * 128), lambda bi, i: (layer, 0, 0)),
    ]
    args = [x, mod, mod, wts["norm1_g"], wts["w_in"], wts["qa_g"], wts["ka_g"], wts["qn_g"], wts["kvn_g"],
            wts["w_q_up"], wts["w_kv_k"], wts["w_kv_v"]]
    if rope:
        in_specs += [pl.BlockSpec((tm, 128), lambda bi, i: (i, 0))] * 4
        args += list(rope_tabs)

    widths = [WIDTH_A, KV_A, KV_A, HEADS_B * QB_PAD, HEADS_B * QB_PAD, WIDTH_B, WIDTH_C, WIDTH_C, WIDTH_C]
    out_shape = [jax.ShapeDtypeStruct((b, n, w), BF16) for w in widths]
    out_specs = [pl.BlockSpec((None, tm, w), lambda bi, i: (bi, i, 0)) for w in widths]
    if state:
        out_shape.append(jax.ShapeDtypeStruct((b, n, ST_COLS), F32))
        out_specs.append(pl.BlockSpec((None, tm, ST_COLS), lambda bi, i: (bi, i, 0)))

    return pl.pallas_call(
        functools.partial(_in_body, rope=rope, state=state),
        grid=(b, n // tm),
        in_specs=in_specs,
        out_specs=out_specs,
        out_shape=out_shape,
        compiler_params=_params("parallel", "parallel"),
        name="input_side",
    )(*args)


def _attn_body(*refs, cache):
    if cache:
        q_ref, ks_ref, vs_ref, kc_ref, vc_ref, o_ref = refs
    else:
        q_ref, ks_ref, vs_ref, o_ref = refs
    q = q_ref[...]
    s_self = lax.dot_general(q, ks_ref[...], NT_DIMS, preferred_element_type=F32)
    m = jnp.max(s_self, axis=-1, keepdims=True)
    if cache:
        s_ctx = lax.dot_general(q, kc_ref[...], NT_DIMS, preferred_element_type=F32)
        m = jnp.maximum(m, jnp.max(s_ctx, axis=-1, keepdims=True))
    p = jnp.exp(s_self - m)
    l = jnp.sum(p, axis=-1, keepdims=True)
    acc = jnp.dot(p.astype(BF16), vs_ref[...], preferred_element_type=F32)
    if cache:
        p = jnp.exp(s_ctx - m)
        l = l + jnp.sum(p, axis=-1, keepdims=True)
        acc = acc + jnp.dot(p.astype(BF16), vc_ref[...], preferred_element_type=F32)
    o_ref[...] = (acc / l).astype(o_ref.dtype)


def _attention(q, ks, vs, kc, vc, layer, heads, group, dq, dv):
    b, n, _ = q.shape
    ms = ks.shape[1]
    tq = TOKEN_TILE
    cache = kc is not None
    in_specs = [
        pl.BlockSpec((None, tq, dq), lambda bi, h, i: (bi, i, h)),
        pl.BlockSpec((None, ms, dq), lambda bi, h, i: (bi, 0, h // group)),
        pl.BlockSpec((None, ms, dv), lambda bi, h, i: (bi, 0, h // group)),
    ]
    args = [q, ks, vs]
    if cache:
        mc = kc.shape[2]
        in_specs += [
            pl.BlockSpec((None, None, mc, dq), lambda bi, h, i: (bi, layer, 0, h // group)),
            pl.BlockSpec((None, None, mc, dv), lambda bi, h, i: (bi, layer, 0, h // group)),
        ]
        args += [kc, vc]
    return pl.pallas_call(
        functools.partial(_attn_body, cache=cache),
        grid=(b, heads, n // tq),
        in_specs=in_specs,
        out_specs=pl.BlockSpec((None, tq, dv), lambda bi, h, i: (bi, i, h)),
        out_shape=jax.ShapeDtypeStruct((b, n, heads * dv), BF16),
        compiler_params=_params("parallel", "parallel", "parallel"),
        name="attention",
    )(*args)


def _nat_body(q_ref, ks_ref, vs_ref, kc_ref, vc_ref, bias_ref, o_ref):
    r = pl.program_id(2)
    key_rows = ks_ref.shape[0] // GRID_W
    kr0 = jnp.clip(NAT_QROWS * r - WIN_ROWS_MAX // 2, 0, key_rows - NAT_KROWS)
    start = pl.multiple_of(kr0 * GRID_W, NAT_QROWS * GRID_W)
    nk = NAT_KROWS * GRID_W
    q = q_ref[...]
    s_win = lax.dot_general(q, ks_ref[pl.ds(start, nk), :], NT_DIMS, preferred_element_type=F32) + bias_ref[...]
    s_ctx = lax.dot_general(q, kc_ref[...], NT_DIMS, preferred_element_type=F32)
    m = jnp.maximum(jnp.max(s_win, axis=-1, keepdims=True), jnp.max(s_ctx, axis=-1, keepdims=True))
    p_win = jnp.exp(s_win - m)
    p_ctx = jnp.exp(s_ctx - m)
    l = jnp.sum(p_win, axis=-1, keepdims=True) + jnp.sum(p_ctx, axis=-1, keepdims=True)
    acc = jnp.dot(p_win.astype(BF16), vs_ref[pl.ds(start, nk), :], preferred_element_type=F32)
    acc = acc + jnp.dot(p_ctx.astype(BF16), vc_ref[...], preferred_element_type=F32)
    o_ref[...] = (acc / l).astype(o_ref.dtype)


def _nat_bias(na_bias_l, rows):
    kh = min(WIN_ROWS_MAX, rows)
    qi = jnp.arange(NAT_QROWS * GRID_W, dtype=jnp.int32)
    ki = jnp.arange(NAT_KROWS * GRID_W, dtype=jnp.int32)
    mats = []
    for blk in (0, 1, rows // NAT_QROWS - 1):
        r_first = NAT_QROWS * blk
        kr0 = min(max(r_first - WIN_ROWS_MAX // 2, 0), rows - NAT_KROWS)
        r = (r_first + qi // GRID_W)[:, None]
        c = (qi % GRID_W)[:, None]
        kr = (kr0 + ki // GRID_W)[None, :]
        kc = (ki % GRID_W)[None, :]
        r0 = jnp.clip(r - kh // 2, 0, rows - kh)
        c0 = jnp.clip(c - WIN_COLS // 2, 0, GRID_W - WIN_COLS)
        valid = (kr >= r0) & (kr < r0 + kh) & (kc >= c0) & (kc < c0 + WIN_COLS)
        dr = jnp.clip(kr - r + (WIN_ROWS_MAX - 1), 0, 2 * WIN_ROWS_MAX - 2)
        dc = jnp.clip(kc - c + (WIN_COLS - 1), 0, 2 * WIN_COLS - 2)
        mats.append(jnp.where(valid[None], na_bias_l[:, dr, dc], MASK_VALUE))
    return jnp.stack(mats, axis=0)


def _neighbourhood_attention(q, ks, vs, kc, vc, bias, layer):
    b, n, _ = q.shape
    nblk = n // (NAT_QROWS * GRID_W)
    tq = NAT_QROWS * GRID_W
    mc = kc.shape[2]
    d = HEAD_DIM

    def bias_map(bi, h, r):
        return (jnp.where(r == 0, 0, jnp.where(r == nblk - 1, 2, 1)), h, 0, 0)

    return pl.pallas_call(
        _nat_body,
        grid=(b, HEADS_C, nblk),
        in_specs=[
            pl.BlockSpec((None, tq, d), lambda bi, h, r: (bi, r, h)),
            pl.BlockSpec((None, n, d), lambda bi, h, r: (bi, 0, h)),
            pl.BlockSpec((None, n, d), lambda bi, h, r: (bi, 0, h)),
            pl.BlockSpec((None, None, mc, d), lambda bi, h, r: (bi, layer, 0, h)),
            pl.BlockSpec((None, None, mc, d), lambda bi, h, r: (bi, layer, 0, h)),
            pl.BlockSpec((None, None, tq, NAT_KROWS * GRID_W), bias_map),
        ],
        out_specs=pl.BlockSpec((None, tq, d), lambda bi, h, r: (bi, r, h)),
        out_shape=jax.ShapeDtypeStruct((b, n, HEADS_C * d), BF16),
        compiler_params=_params("parallel", "parallel", "arbitrary"),
        name="neighbourhood_attention",
    )(q, ks, vs, kc, vc, bias)


def _cache_kv_body(ckv_ref, kr_ref, wkk_ref, wkv_ref, kb_o, vb_o):
    ckvb = ckv_ref[...].astype(BF16)
    krb = kr_ref[...].astype(BF16)
    kn = jnp.dot(ckvb, wkk_ref[...], preferred_element_type=F32)
    zeros = jnp.zeros((krb.shape[0], QB_PAD - 128 - ROPE_DIM_B), BF16)
    for hd in range(HEADS_B):
        lo = hd * QB_PAD
        kb_o[:, lo:lo + 128] = kn[:, hd * 128:(hd + 1) * 128].astype(BF16)
        kb_o[:, lo + 128:lo + 128 + ROPE_DIM_B] = krb
        kb_o[:, lo + 128 + ROPE_DIM_B:lo + QB_PAD] = zeros
    vb_o[...] = jnp.dot(ckvb, wkv_ref[...], preferred_element_type=F32).astype(BF16)


def _cache_mla_kv(cache_ckv, cache_krope, w_kv_k, w_kv_v):
    b, depth, m, _ = cache_ckv.shape
    return pl.pallas_call(
        _cache_kv_body,
        grid=(b, depth),
        in_specs=[
            pl.BlockSpec((None, None, m, KV_RANK_B), lambda bi, l: (bi, l, 0, 0)),
            pl.BlockSpec((None, None, m, ROPE_DIM_B), lambda bi, l: (bi, l, 0, 0)),
            pl.BlockSpec((None, KV_RANK_B, HEADS_B * 128), lambda bi, l: (l, 0, 0)),
            pl.BlockSpec((None, KV_RANK_B, HEADS_B * 128), lambda bi, l: (l, 0, 0)),
        ],
        out_specs=[
            pl.BlockSpec((None, None, m, HEADS_B * QB_PAD), lambda bi, l: (bi, l, 0, 0)),
            pl.BlockSpec((None, None, m, WIDTH_B), lambda bi, l: (bi, l, 0, 0)),
        ],
        out_shape=[
            jax.ShapeDtypeStruct((b, depth, m, HEADS_B * QB_PAD), BF16),
            jax.ShapeDtypeStruct((b, depth, m, WIDTH_B), BF16),
        ],
        compiler_params=_params("parallel", "parallel"),
        name="cache_mla_kv",
    )(cache_ckv, cache_krope, w_kv_k, w_kv_v)


def _out_body(oa_ref, ob_ref, oc_ref, wa_ref, wb_ref, wc_ref, x_ref, gate_ref, sh_ref, sc_ref, g_ref, wr_ref,
              x1_o, h2_o, lg_o):
    o = jnp.dot(oa_ref[...], wa_ref[...], preferred_element_type=F32)
    o = o + jnp.dot(ob_ref[...], wb_ref[...], preferred_element_type=F32)
    o = o + jnp.dot(oc_ref[...], wc_ref[...], preferred_element_type=F32)
    x1 = x_ref[...] + gate_ref[...] * o
    x1_o[...] = x1
    h = _rms(x1, g_ref[...]) * (1.0 + sc_ref[...]) + sh_ref[...]
    h2_o[...] = h.astype(BF16)
    lg_o[...] = lax.dot_general(wr_ref[...], h, NT_DIMS, precision=lax.Precision.HIGHEST,
                                preferred_element_type=F32)


def _output_side(o_a, o_b, o_c, x, mod, layer, wts):
    b, n, d = x.shape
    tm = TOKEN_TILE
    mb = mod.shape[0]
    bsel = (lambda i: i) if mb > 1 else (lambda i: 0)

    def modspec(k):
        return pl.BlockSpec((None, None, 1, d), lambda bi, i: (bsel(bi), k, 0, 0))

    def tok(w):
        return pl.BlockSpec((None, tm, w), lambda bi, i: (bi, i, 0))

    return pl.pallas_call(
        _out_body,
        grid=(b, n // tm),
        in_specs=[
            tok(WIDTH_A), tok(WIDTH_B), tok(WIDTH_C),
            _resident((None, WIDTH_A, d), lambda bi, i: (layer, 0, 0)),
            _resident((None, WIDTH_B, d), lambda bi, i: (layer, 0, 0)),
            _resident((None, WIDTH_C, d), lambda bi, i: (layer, 0, 0)),
            tok(d), modspec(2), modspec(3), modspec(4),
            pl.BlockSpec((None, 1, d), lambda bi, i: (layer, 0, 0)),
            pl.BlockSpec((None, N_EXPERTS, d), lambda bi, i: (layer, 0, 0)),
        ],
        out_specs=[tok(d), tok(d), pl.BlockSpec((None, N_EXPERTS, tm), lambda bi, i: (bi, 0, i))],
        out_shape=[
            jax.ShapeDtypeStruct((b, n, d), F32),
            jax.ShapeDtypeStruct((b, n, d), BF16),
            jax.ShapeDtypeStruct((b, N_EXPERTS, n), F32),
        ],
        compiler_params=_params("parallel", "parallel"),
        name="output_side",
    )(o_a, o_b, o_c, wts["w_out_a"], wts["w_out_b"], wts["w_out_c"], x, mod, mod, mod, wts["norm2_g"],
      wts["w_router_t"])


def _prefix_exclusive(mask):
    e, n = mask.shape
    ones = jnp.where(mask, 1.0, 0.0)
    rr = lax.broadcasted_iota(jnp.int32, (128, 128), 0)
    cc = lax.broadcasted_iota(jnp.int32, (128, 128), 1)
    tri = jnp.where(rr <= cc, 1.0, 0.0).astype(BF16)
    carry = jnp.zeros((e, 1), F32)
    outs = []
    for c in range(n // 128):
        blk = ones[:, c * 128:(c + 1) * 128]
        inc = jnp.dot(blk.astype(BF16), tri, preferred_element_type=F32)
        outs.append(inc - blk + carry)
        carry = carry + inc[:, 127:128]
    return jnp.concatenate(outs, axis=1)


def _router_body(lg_ref, rank_o, rank_t_o, aff_t_o, *, cap):
    lg = lg_ref[...]
    e, n = lg.shape
    ex = jnp.exp(lg - jnp.max(lg, axis=0, keepdims=True))
    aff = ex / jnp.sum(ex, axis=0, keepdims=True)
    key = pltpu.bitcast(aff, jnp.int32)

    def step(i, t):
        cand = t | lax.shift_left(jnp.int32(1), 30 - i)
        cnt = jnp.sum(jnp.where(key >= cand, 1.0, 0.0), axis=1, keepdims=True)
        return jnp.where(cnt >= cap, cand, t)

    thr = lax.fori_loop(0, 31, step, jnp.zeros((e, 1), jnp.int32))
    above = key > thr
    tied = key == thr
    need = cap - jnp.sum(jnp.where(above, 1.0, 0.0), axis=1, keepdims=True)
    chosen = above | (tied & (_prefix_exclusive(tied) < need))
    rank = jnp.where(chosen, _prefix_exclusive(chosen), -1.0)
    rank_o[...] = rank.astype(jnp.int32)

    fill = jnp.full((128 - e, n), -1.0, F32)
    rank_p = jnp.concatenate([rank, fill], axis=0)
    aff_p = jnp.concatenate([aff, fill], axis=0)
    for c in range(n // 128):
        rank_t_o[c * 128:(c + 1) * 128, :] = rank_p[:, c * 128:(c + 1) * 128].T
        aff_t_o[c * 128:(c + 1) * 128, :] = aff_p[:, c * 128:(c + 1) * 128].T


def _route(logits_t):
    b, e, n = logits_t.shape
    cap = CAPACITY_FACTOR * n // e
    return pl.pallas_call(
        functools.partial(_router_body, cap=cap),
        grid=(b,),
        in_specs=[pl.BlockSpec((None, e, n), lambda bi: (bi, 0, 0))],
        out_specs=[
            pl.BlockSpec((None, e, n), lambda bi: (bi, 0, 0)),
            pl.BlockSpec((None, n, 128), lambda bi: (bi, 0, 0)),
            pl.BlockSpec((None, n, 128), lambda bi: (bi, 0, 0)),
        ],
        out_shape=[
            jax.ShapeDtypeStruct((b, e, n), jnp.int32),
            jax.ShapeDtypeStruct((b, n, 128), F32),
            jax.ShapeDtypeStruct((b, n, 128), F32),
        ],
        compiler_params=_params("parallel"),
        name="route",
    )(logits_t)


def _gather_body(h_ref, rank_ref, xe_o, *, chunk):
    n, d = h_ref.shape
    cap = xe_o.shape[0]
    rank = rank_ref[...]
    slot = lax.broadcasted_iota(jnp.int32, (cap, chunk), 0)
    acc = jnp.zeros((cap, d), F32)
    for c in range(n // chunk):
        onehot = jnp.where(slot == rank[:, c * chunk:(c + 1) * chunk], 1.0, 0.0).astype(BF16)
        acc = acc + jnp.dot(onehot, h_ref[c * chunk:(c + 1) * chunk, :], preferred_element_type=F32)
    xe_o[...] = acc.astype(xe_o.dtype)


def _gather(h2, rank):
    b, n, d = h2.shape
    e = rank.shape[1]
    cap = CAPACITY_FACTOR * n // e
    return pl.pallas_call(
        functools.partial(_gather_body, chunk=min(n, 512)),
        grid=(b, e),
        in_specs=[
            pl.BlockSpec((None, n, d), lambda bi, ei: (bi, 0, 0)),
            pl.BlockSpec((None, None, 1, n), lambda bi, ei: (bi, ei, 0, 0)),
        ],
        out_specs=pl.BlockSpec((None, None, cap, d), lambda bi, ei: (ei, bi, 0, 0)),
        out_shape=jax.ShapeDtypeStruct((e, b, cap, d), BF16),
        compiler_params=_params("parallel", "parallel"),
        name="gather",
    )(h2, rank)


def _ffn_body(xe_ref, wg_ref, wu_ref, wd_ref, ye_o):
    bb, cap, d = xe_ref.shape
    xe = xe_ref[...].reshape(bb * cap, d)
    g = jnp.dot(xe, wg_ref[...], preferred_element_type=F32)
    u = jnp.dot(xe, wu_ref[...], preferred_element_type=F32)
    hid = (g / (1.0 + jnp.exp(-g)) * u).astype(BF16)
    ye = jnp.dot(hid, wd_ref[...], preferred_element_type=F32)
    ye_o[...] = ye.astype(ye_o.dtype).reshape(bb, cap, d)


def _expert_ffn(xe, layer, wts):
    e, b, cap, d = xe.shape
    ff = wts["w_gate"].shape[-1]
    bb = max(1, min(b, 512 // cap))
    return pl.pallas_call(
        _ffn_body,
        grid=(e, b // bb),
        in_specs=[
            pl.BlockSpec((None, bb, cap, d), lambda ei, bi: (ei, bi, 0, 0)),
            pl.BlockSpec((None, None, d, ff), lambda ei, bi: (layer, ei, 0, 0)),
            pl.BlockSpec((None, None, d, ff), lambda ei, bi: (layer, ei, 0, 0)),
            pl.BlockSpec((None, None, ff, d), lambda ei, bi: (layer, ei, 0, 0)),
        ],
        out_specs=pl.BlockSpec((None, bb, cap, d), lambda ei, bi: (ei, bi, 0, 0)),
        out_shape=jax.ShapeDtypeStruct((e, b, cap, d), BF16),
        compiler_params=_params("parallel", "arbitrary"),
        name="expert_ffn",
    )(xe, wts["w_gate"], wts["w_up"], wts["w_down"])


def _scatter_body(ye_ref, x_ref, rank_ref, aff_ref, gate_ref, o_ref):
    e, cap, dh = ye_ref.shape
    tn = x_ref.shape[0]
    slot = lax.broadcasted_iota(jnp.int32, (tn, cap), 1).astype(F32)
    acc = jnp.zeros((tn, dh), F32)
    for ei in range(e):
        onehot = jnp.where(rank_ref[:, ei:ei + 1] == slot, 1.0, 0.0).astype(BF16)
        contrib = jnp.dot(onehot, ye_ref[ei], preferred_element_type=F32)
        acc = acc + contrib * aff_ref[:, ei:ei + 1]
    o_ref[...] = x_ref[...] + gate_ref[...] * acc


def _scatter(ye, x1, rank_t, aff_t, mod):
    e, b, cap, d = ye.shape
    n = x1.shape[1]
    tn = min(n, 512)
    dh = d // 2
    mb = mod.shape[0]
    bsel = (lambda i: i) if mb > 1 else (lambda i: 0)
    return pl.pallas_call(
        _scatter_body,
        grid=(b, 2, n // tn),
        in_specs=[
            pl.BlockSpec((e, None, cap, dh), lambda bi, j, i: (0, bi, 0, j)),
            pl.BlockSpec((None, tn, dh), lambda bi, j, i: (bi, i, j)),
            pl.BlockSpec((None, tn, 128), lambda bi, j, i: (bi, i, 0)),
            pl.BlockSpec((None, tn, 128), lambda bi, j, i: (bi, i, 0)),
            pl.BlockSpec((None, None, 1, dh), lambda bi, j, i: (bsel(bi), 5, 0, j)),
        ],
        out_specs=pl.BlockSpec((None, tn, dh), lambda bi, j, i: (bi, i, j)),
        out_shape=jax.ShapeDtypeStruct(x1.shape, F32),
        compiler_params=_params("parallel", "parallel", "parallel"),
        name="scatter",
    )(ye, x1, rank_t, aff_t, mod)


def _final_norm_body(x_ref, g_ref, o_ref):
    o_ref[...] = _rms(x_ref[...], g_ref[...])


def _final_norm(x, g):
    b, n, d = x.shape
    tm = TOKEN_TILE
    return pl.pallas_call(
        _final_norm_body,
        grid=(b, n // tm),
        in_specs=[pl.BlockSpec((None, tm, d), lambda bi, i: (bi, i, 0)), pl.BlockSpec((1, d), lambda bi, i: (0, 0))],
        out_specs=pl.BlockSpec((None, tm, d), lambda bi, i: (bi, i, 0)),
        out_shape=jax.ShapeDtypeStruct(x.shape, F32),
        compiler_params=_params("parallel", "parallel"),
        name="final_norm",
    )(x, g.reshape(1, d))


def _rope_tables(n_tokens, rot_dim):
    t = jnp.arange(n_tokens, dtype=jnp.int32)
    row = (t // GRID_W).astype(F32)
    col = (t % GRID_W).astype(F32)
    axis_dim = rot_dim // 2
    freqs = ROPE_THETA ** (-jnp.arange(0, axis_dim, 2, dtype=F32) / axis_dim)
    ang = jnp.concatenate([row[:, None] * freqs[None, :], col[:, None] * freqs[None, :]], axis=-1)
    cos, sin = jnp.cos(ang), jnp.sin(ang)
    cos2 = jnp.repeat(cos, 2, axis=-1)
    sin2 = jnp.stack([-sin, sin], axis=-1).reshape(n_tokens, rot_dim)
    pad = 128 - rot_dim
    if pad:
        cos2 = jnp.concatenate([cos2, jnp.ones((n_tokens, pad), F32)], axis=-1)
        sin2 = jnp.concatenate([sin2, jnp.zeros((n_tokens, pad), F32)], axis=-1)
    return cos2, sin2


def _prepare_weights(norm1_g, norm2_g, w_in, qa_norm_g, ka_norm_g, q_norm_b, kv_norm_b, w_q_up, w_kv_up, w_out,
                     w_router, w_gate, w_up, w_down):
    depth, d, _ = w_in.shape
    kr_lo = SEG_CKV[1]
    w_in_p = jnp.concatenate(
        [w_in[:, :, :kr_lo], w_in[:, :, kr_lo + ROPE_DIM_B:], w_in[:, :, kr_lo:kr_lo + ROPE_DIM_B],
         jnp.zeros((depth, d, 128 - ROPE_DIM_B), w_in.dtype)], axis=-1).astype(BF16)
    wq = w_q_up.reshape(depth, Q_RANK_B, HEADS_B, NOPE_DIM_B + ROPE_DIM_B)
    wq = jnp.pad(wq, ((0, 0), (0, 0), (0, 0), (0, QB_PAD - NOPE_DIM_B - ROPE_DIM_B)))
    wkv = w_kv_up.reshape(depth, KV_RANK_B, HEADS_B, NOPE_DIM_B + V_DIM_B)
    return {
        "norm1_g": norm1_g.reshape(depth, 1, d),
        "norm2_g": norm2_g.reshape(depth, 1, d),
        "w_in": w_in_p,
        "qa_g": qa_norm_g.reshape(depth, 1, HEAD_DIM),
        "ka_g": ka_norm_g.reshape(depth, 1, HEAD_DIM),
        "qn_g": q_norm_b.reshape(depth, 1, Q_RANK_B),
        "kvn_g": kv_norm_b.reshape(depth, 1, KV_RANK_B),
        "w_q_up": wq.reshape(depth, Q_RANK_B, HEADS_B * QB_PAD).astype(BF16),
        "w_kv_k": wkv[..., :NOPE_DIM_B].reshape(depth, KV_RANK_B, HEADS_B * NOPE_DIM_B).astype(BF16),
        "w_kv_v": wkv[..., NOPE_DIM_B:].reshape(depth, KV_RANK_B, HEADS_B * V_DIM_B).astype(BF16),
        "w_out_a": w_out[:, :WIDTH_A].astype(BF16),
        "w_out_b": w_out[:, WIDTH_A:WIDTH_A + WIDTH_B].astype(BF16),
        "w_out_c": w_out[:, WIDTH_A + WIDTH_B:].astype(BF16),
        "w_router_t": jnp.swapaxes(w_router, 1, 2),
        "w_gate": w_gate.astype(BF16),
        "w_up": w_up.astype(BF16),
        "w_down": w_down.astype(BF16),
    }


def _moe(h2, logits_t, x1, mod, layer, wts):
    rank, rank_t, aff_t = _route(logits_t)
    xe = _gather(h2, rank.reshape(rank.shape[0], rank.shape[1], 1, rank.shape[2]))
    ye = _expert_ffn(xe, layer, wts)
    return _scatter(ye, x1, rank_t, aff_t, mod)


def kernel(x_prompt, x_sample, cache_a_k, cache_a_v, cache_b_ckv, cache_b_krope, cache_c_k, cache_c_v, c, c_ctx, w_ada, b_ada, norm1_g, norm2_g, w_in, qa_norm_g, ka_norm_g, q_norm_b, kv_norm_b, w_q_up, w_kv_up, na_bias, w_out, w_router, w_gate, w_up, w_down, final_norm_g):
    depth, d, _ = w_in.shape
    bp, seq, _ = x_prompt.shape
    bs, n_lat, _ = x_sample.shape
    past = cache_a_k.shape[2]
    rows = n_lat // GRID_W

    wts = _prepare_weights(norm1_g, norm2_g, w_in, qa_norm_g, ka_norm_g, q_norm_b, kv_norm_b, w_q_up, w_kv_up,
                           w_out, w_router, w_gate, w_up, w_down)
    rope_tabs = _rope_tables(n_lat, HEAD_DIM) + _rope_tables(n_lat, ROPE_DIM_B)

    cond8 = jnp.concatenate([c_ctx[None], c, jnp.zeros((8 - 1 - bs, d), F32)], axis=0)
    mods = _modulation(cond8, w_ada, b_ada)
    mods = mods.reshape(depth, 8, 6, 1, d)

    ca_k = cache_a_k.reshape(bs, depth, past, KV_A).astype(BF16)
    ca_v = cache_a_v.reshape(bs, depth, past, KV_A).astype(BF16)
    cc_k = cache_c_k.reshape(bs, depth, past, WIDTH_C).astype(BF16)
    cc_v = cache_c_v.reshape(bs, depth, past, WIDTH_C).astype(BF16)
    cb_k, cb_v = _cache_mla_kv(cache_b_ckv, cache_b_krope, wts["w_kv_k"], wts["w_kv_v"])

    xp, xs = x_prompt, x_sample
    states = []
    for l in range(depth):
        mod_c = mods[l, 0:1]
        mod_l = mods[l, 1:1 + bs]

        qa, ka, va, qb, kb, vb, qc, kc, vc, st = _input_side(xp, mod_c, l, wts, None, True)
        o_a = _attention(qa, ka, va, None, None, l, HEADS_A, HEADS_A // KV_HEADS_A, HEAD_DIM, HEAD_DIM)
        o_b = _attention(qb, kb, vb, None, None, l, HEADS_B, 1, QB_PAD, V_DIM_B)
        o_c = _attention(qc, kc, vc, None, None, l, HEADS_C, 1, HEAD_DIM, HEAD_DIM)
        x1, h2, lg = _output_side(o_a, o_b, o_c, xp, mod_c, l, wts)
        xp = _moe(h2, lg, x1, mod_c, l, wts)
        states.append(st)

        qa, ka, va, qb, kb, vb, qc, kc, vc = _input_side(xs, mod_l, l, wts, rope_tabs, False)
        o_a = _attention(qa, ka, va, ca_k, ca_v, l, HEADS_A, HEADS_A // KV_HEADS_A, HEAD_DIM, HEAD_DIM)
        o_b = _attention(qb, kb, vb, cb_k, cb_v, l, HEADS_B, 1, QB_PAD, V_DIM_B)
        o_c = _neighbourhood_attention(qc, kc, vc, cc_k, cc_v, _nat_bias(na_bias[l], rows), l)
        x1, h2, lg = _output_side(o_a, o_b, o_c, xs, mod_l, l, wts)
        xs = _moe(h2, lg, x1, mod_l, l, wts)

    y_prompt = _final_norm(xp, final_norm_g)
    y_sample = _final_norm(xs, final_norm_g)
    st = jnp.stack(states, axis=1)
    new_a_k = st[..., ST_KA[0]:ST_KA[1]].reshape(bp, depth, seq, KV_HEADS_A, HEAD_DIM)
    new_a_v = st[..., ST_VA[0]:ST_VA[1]].reshape(bp, depth, seq, KV_HEADS_A, HEAD_DIM)
    new_b_ckv = st[..., ST_CKV[0]:ST_CKV[1]]
    new_b_krope = st[..., ST_KR[0]:ST_KR[0] + ROPE_DIM_B]
    new_c_k = st[..., ST_KC[0]:ST_KC[1]].reshape(bp, depth, seq, HEADS_C, HEAD_DIM)
    new_c_v = st[..., ST_VC[0]:ST_VC[1]].reshape(bp, depth, seq, HEADS_C, HEAD_DIM)
    return (y_prompt, y_sample, new_a_k, new_a_v, new_b_ckv, new_b_krope, new_c_k, new_c_v)
```

```python
import functools

import jax
import jax.numpy as jnp
from jax import lax
from jax.experimental import pallas as pl
from jax.experimental.pallas import tpu as pltpu

F32 = jnp.float32
BF16 = jnp.bfloat16

GRID_W = 64
HEAD_DIM = 128
HEADS_A = 6
KV_HEADS_A = 2
HEADS_B = 5
Q_RANK_B = 512
KV_RANK_B = 256
NOPE_DIM_B = 128
ROPE_DIM_B = 64
V_DIM_B = 128
HEADS_C = 5
WIN_ROWS_MAX = 8
WIN_COLS = 16
N_EXPERTS = 16
CAPACITY_FACTOR = 2
ROPE_THETA = 10000.0
EPS = 1e-6

QB_PAD = 256
WIDTH_A = HEADS_A * HEAD_DIM
WIDTH_B = HEADS_B * V_DIM_B
WIDTH_C = HEADS_C * HEAD_DIM
KV_A = KV_HEADS_A * HEAD_DIM

SEG_QA = (0, WIDTH_A)
SEG_KA = (SEG_QA[1], SEG_QA[1] + KV_A)
SEG_VA = (SEG_KA[1], SEG_KA[1] + KV_A)
SEG_CQ = (SEG_VA[1], SEG_VA[1] + Q_RANK_B)
SEG_CKV = (SEG_CQ[1], SEG_CQ[1] + KV_RANK_B)
SEG_QC = (SEG_CKV[1], SEG_CKV[1] + WIDTH_C)
SEG_KC = (SEG_QC[1], SEG_QC[1] + WIDTH_C)
SEG_VC = (SEG_KC[1], SEG_KC[1] + WIDTH_C)
SEG_KR = (SEG_VC[1], SEG_VC[1] + 128)
IN_COLS_P = SEG_KR[1]

ST_KA = (0, KV_A)
ST_VA = (ST_KA[1], ST_KA[1] + KV_A)
ST_CKV = (ST_VA[1], ST_VA[1] + KV_RANK_B)
ST_KC = (ST_CKV[1], ST_CKV[1] + WIDTH_C)
ST_VC = (ST_KC[1], ST_KC[1] + WIDTH_C)
ST_KR = (ST_VC[1], ST_VC[1] + 128)
ST_COLS = ST_KR[1]

NAT_QROWS = 4
NAT_KROWS = 12
MASK_VALUE = -1e30
LOG2E = 1.4426950408889634

TOKEN_TILE = 256
GATHER_CHUNK = 512
GATHER_ROWS = 128
VMEM_LIMIT = 56 * 1024 * 1024
NT_DIMS = (((1,), (1,)), ((), ()))


def _params(*sem):
    return pltpu.CompilerParams(dimension_semantics=sem, vmem_limit_bytes=VMEM_LIMIT)


def _resident(block_shape, index_map):
    return pl.BlockSpec(block_shape, index_map, pipeline_mode=pl.Buffered(1))


def _rms(x, g):
    ms = jnp.mean(x * x, axis=-1, keepdims=True)
    return x * lax.rsqrt(ms + EPS) * g


def _swap_pairs(x):
    lane = lax.broadcasted_iota(jnp.int32, x.shape, x.ndim - 1)
    nxt = pltpu.roll(x, x.shape[-1] - 1, x.ndim - 1)
    prv = pltpu.roll(x, 1, x.ndim - 1)
    return jnp.where((lane & 1) == 0, nxt, prv)


def _rope(x, cos, sin_signed):
    return x * cos + _swap_pairs(x) * sin_signed


def _mod_body(c_ref, w_ref, b_ref, o_ref):
    c = c_ref[...]
    s = (c / (1.0 + jnp.exp(-c))).astype(BF16)
    o_ref[...] = jnp.dot(s, w_ref[...].astype(BF16), preferred_element_type=F32) + b_ref[...]


def _modulation(cond8, w_ada, b_ada):
    depth, d, cols = w_ada.shape
    tn = 1024
    return pl.pallas_call(
        _mod_body,
        grid=(depth, cols // tn),
        in_specs=[
            pl.BlockSpec((8, d), lambda l, j: (0, 0)),
            pl.BlockSpec((None, d, tn), lambda l, j: (l, 0, j)),
            pl.BlockSpec((None, 1, tn), lambda l, j: (l, 0, j)),
        ],
        out_specs=pl.BlockSpec((None, 8, tn), lambda l, j: (l, 0, j)),
        out_shape=jax.ShapeDtypeStruct((depth, 8, cols), F32),
        compiler_params=_params("parallel", "parallel"),
        name="modulation",
    )(cond8, w_ada, b_ada.reshape(depth, 1, cols))


def _in_body(*refs, rope, state):
    it = iter(refs)
    x_ref, sh_ref, sc_ref, g_ref, win_ref = (next(it) for _ in range(5))
    qag_ref, kag_ref, qng_ref, kvng_ref = (next(it) for _ in range(4))
    wq_ref, wkk_ref, wkv_ref = (next(it) for _ in range(3))
    if rope:
        ca_ref, sa_ref, cb_ref, sb_ref = (next(it) for _ in range(4))
    qa_o, ka_o, va_o, qb_o, kb_o, vb_o, qc_o, kc_o, vc_o = (next(it) for _ in range(9))
    st_o = next(it) if state else None

    h = _rms(x_ref[...], g_ref[...]) * (1.0 + sc_ref[...]) + sh_ref[...]
    hb = h.astype(BF16)

    def seg(bounds):
        return jnp.dot(hb, win_ref[:, bounds[0]:bounds[1]], preferred_element_type=F32)

    def rope_a(y):
        return _rope(y, ca_ref[...], sa_ref[...]) if rope else y

    def rope_b(y):
        return _rope(y, cb_ref[...], sb_ref[...]) if rope else y

    qa = seg(SEG_QA)
    for hd in range(HEADS_A):
        lo = hd * HEAD_DIM
        y = rope_a(_rms(qa[:, lo:lo + HEAD_DIM], qag_ref[...]))
        qa_o[:, lo:lo + HEAD_DIM] = (y * (HEAD_DIM ** -0.5 * LOG2E)).astype(BF16)
    ka = seg(SEG_KA)
    for hd in range(KV_HEADS_A):
        lo = hd * HEAD_DIM
        y = rope_a(_rms(ka[:, lo:lo + HEAD_DIM], kag_ref[...]))
        ka_o[:, lo:lo + HEAD_DIM] = y.astype(BF16)
        if state:
            st_o[:, ST_KA[0] + lo:ST_KA[0] + lo + HEAD_DIM] = y
    va = seg(SEG_VA)
    va_o[...] = va.astype(BF16)

    cq = _rms(seg(SEG_CQ), qng_ref[...]).astype(BF16)
    qb = jnp.dot(cq, wq_ref[...], preferred_element_type=F32)
    qscale = (NOPE_DIM_B + ROPE_DIM_B) ** -0.5 * LOG2E
    for hd in range(HEADS_B):
        lo = hd * QB_PAD
        qb_o[:, lo:lo + 128] = (qb[:, lo:lo + 128] * qscale).astype(BF16)
        qb_o[:, lo + 128:lo + 256] = (rope_b(qb[:, lo + 128:lo + 256]) * qscale).astype(BF16)
    ckv = _rms(seg(SEG_CKV), kvng_ref[...])
    ckvb = ckv.astype(BF16)
    kr = rope_b(seg(SEG_KR))
    krb = kr.astype(BF16)
    kn = jnp.dot(ckvb, wkk_ref[...], preferred_element_type=F32)
    for hd in range(HEADS_B):
        kb_o[:, hd * QB_PAD:hd * QB_PAD + 128] = kn[:, hd * 128:(hd + 1) * 128].astype(BF16)
        kb_o[:, hd * QB_PAD + 128:(hd + 1) * QB_PAD] = krb
    vb_o[...] = jnp.dot(ckvb, wkv_ref[...], preferred_element_type=F32).astype(BF16)

    qc_o[...] = (seg(SEG_QC) * (HEAD_DIM ** -0.5 * LOG2E)).astype(BF16)
    kc = seg(SEG_KC)
    kc_o[...] = kc.astype(BF16)
    vc = seg(SEG_VC)
    vc_o[...] = vc.astype(BF16)

    if state:
        st_o[:, ST_VA[0]:ST_VA[1]] = va
        st_o[:, ST_CKV[0]:ST_CKV[1]] = ckv
        st_o[:, ST_KC[0]:ST_KC[1]] = kc
        st_o[:, ST_VC[0]:ST_VC[1]] = vc
        st_o[:, ST_KR[0]:ST_KR[1]] = kr


def _input_side(x, mod, layer, wts, rope_tabs, state):
    b, n, d = x.shape
    tm = TOKEN_TILE
    mb = mod.shape[0]
    bsel = (lambda i: i) if mb > 1 else (lambda i: 0)
    rope = rope_tabs is not None

    def modspec(k):
        return pl.BlockSpec((None, None, 1, d), lambda bi, i: (bsel(bi), k, 0, 0))

    def vec(w):
        return pl.BlockSpec((None, 1, w), lambda bi, i: (layer, 0, 0))

    in_specs = [
        pl.BlockSpec((None, tm, d), lambda bi, i: (bi, i, 0)),
        modspec(0), modspec(1), vec(d),
        _resident((None, d, IN_COLS_P), lambda bi, i: (layer, 0, 0)),
        vec(HEAD_DIM), vec(HEAD_DIM), vec(Q_RANK_B), vec(KV_RANK_B),
        _resident((None, Q_RANK_B, HEADS_B * QB_PAD), lambda bi, i: (layer, 0, 0)),
        _resident((None, KV_RANK_B, HEADS_B * 128), lambda bi, i: (layer, 0, 0)),
        _resident((None, KV_RANK_B, HEADS_B * 128), lambda bi, i: (layer, 0, 0)),
    ]
    args = [x, mod, mod, wts["norm1_g"], wts["w_in"], wts["qa_g"], wts["ka_g"], wts["qn_g"], wts["kvn_g"],
            wts["w_q_up"], wts["w_kv_k"], wts["w_kv_v"]]
    if rope:
        in_specs += [pl.BlockSpec((tm, 128), lambda bi, i: (i, 0))] * 4
        args += list(rope_tabs)

    widths = [WIDTH_A, KV_A, KV_A, HEADS_B * QB_PAD, HEADS_B * QB_PAD, WIDTH_B, WIDTH_C, WIDTH_C, WIDTH_C]
    out_shape = [jax.ShapeDtypeStruct((b, n, w), BF16) for w in widths]
    out_specs = [pl.BlockSpec((None, tm, w), lambda bi, i: (bi, i, 0)) for w in widths]
    if state:
        out_shape.append(jax.ShapeDtypeStruct((b, n, ST_COLS), F32))
        out_specs.append(pl.BlockSpec((None, tm, ST_COLS), lambda bi, i: (bi, i, 0)))

    return pl.pallas_call(
        functools.partial(_in_body, rope=rope, state=state),
        grid=(b, n // tm),
        in_specs=in_specs,
        out_specs=out_specs,
        out_shape=out_shape,
        compiler_params=_params("parallel", "parallel"),
        name="input_side",
    )(*args)


def _attn_body(*refs, cache):
    if cache:
        q_ref, ks_ref, vs_ref, kc_ref, vc_ref, o_ref = refs
    else:
        q_ref, ks_ref, vs_ref, o_ref = refs
    rows = min(q_ref.shape[0], TOKEN_TILE)
    for c in range(q_ref.shape[0] // rows):
        q = q_ref[c * rows:(c + 1) * rows, :]
        s_self = lax.dot_general(q, ks_ref[...], NT_DIMS, preferred_element_type=F32)
        m = jnp.max(s_self, axis=-1, keepdims=True)
        if cache:
            s_ctx = lax.dot_general(q, kc_ref[...], NT_DIMS, preferred_element_type=F32)
            m = jnp.maximum(m, jnp.max(s_ctx, axis=-1, keepdims=True))
        p = jnp.exp2(s_self - m)
        l = jnp.sum(p, axis=-1, keepdims=True)
        acc = jnp.dot(p.astype(BF16), vs_ref[...], preferred_element_type=F32)
        if cache:
            p = jnp.exp2(s_ctx - m)
            l = l + jnp.sum(p, axis=-1, keepdims=True)
            acc = acc + jnp.dot(p.astype(BF16), vc_ref[...], preferred_element_type=F32)
        o_ref[c * rows:(c + 1) * rows, :] = (acc / l).astype(o_ref.dtype)


def _attention(q, ks, vs, kc, vc, layer, heads, group, dq, dv):
    b, n, _ = q.shape
    ms = ks.shape[1]
    tq = min(n, 2 * TOKEN_TILE)
    cache = kc is not None
    in_specs = [
        pl.BlockSpec((None, tq, dq), lambda bi, h, i: (bi, i, h)),
        pl.BlockSpec((None, ms, dq), lambda bi, h, i: (bi, 0, h // group)),
        pl.BlockSpec((None, ms, dv), lambda bi, h, i: (bi, 0, h // group)),
    ]
    args = [q, ks, vs]
    if cache:
        mc = kc.shape[2]
        in_specs += [
            pl.BlockSpec((None, None, mc, dq), lambda bi, h, i: (bi, layer, 0, h // group)),
            pl.BlockSpec((None, None, mc, dv), lambda bi, h, i: (bi, layer, 0, h // group)),
        ]
        args += [kc, vc]
    return pl.pallas_call(
        functools.partial(_attn_body, cache=cache),
        grid=(b, heads, n // tq),
        in_specs=in_specs,
        out_specs=pl.BlockSpec((None, tq, dv), lambda bi, h, i: (bi, i, h)),
        out_shape=jax.ShapeDtypeStruct((b, n, heads * dv), BF16),
        compiler_params=_params("parallel", "parallel", "parallel"),
        name="attention",
    )(*args)


def _nat_body(q_ref, ks_ref, vs_ref, kc_ref, vc_ref, bias_ref, o_ref):
    r = pl.program_id(2)
    key_rows = ks_ref.shape[0] // GRID_W
    kr0 = jnp.clip(NAT_QROWS * r - WIN_ROWS_MAX // 2, 0, key_rows - NAT_KROWS)
    start = pl.multiple_of(kr0 * GRID_W, NAT_QROWS * GRID_W)
    nk = NAT_KROWS * GRID_W
    q = q_ref[...]
    s_win = lax.dot_general(q, ks_ref[pl.ds(start, nk), :], NT_DIMS, preferred_element_type=F32) + bias_ref[...]
    s_ctx = lax.dot_general(q, kc_ref[...], NT_DIMS, preferred_element_type=F32)
    m = jnp.maximum(jnp.max(s_win, axis=-1, keepdims=True), jnp.max(s_ctx, axis=-1, keepdims=True))
    p_win = jnp.exp2(s_win - m)
    p_ctx = jnp.exp2(s_ctx - m)
    l = jnp.sum(p_win, axis=-1, keepdims=True) + jnp.sum(p_ctx, axis=-1, keepdims=True)
    acc = jnp.dot(p_win.astype(BF16), vs_ref[pl.ds(start, nk), :], preferred_element_type=F32)
    acc = acc + jnp.dot(p_ctx.astype(BF16), vc_ref[...], preferred_element_type=F32)
    o_ref[...] = (acc / l).astype(o_ref.dtype)


N_DROW = 2 * WIN_ROWS_MAX - 1
N_DCOL = 2 * WIN_COLS - 1


def _nat_bias_body(b_ref, o_ref, *, rows):
    base = (pl.program_id(0) * HEADS_C + pl.program_id(1)) * (N_DROW * N_DCOL)
    c = lax.broadcasted_iota(jnp.int32, (GRID_W, GRID_W), 0)
    kc = lax.broadcasted_iota(jnp.int32, (GRID_W, GRID_W), 1)
    c0 = jnp.clip(c - WIN_COLS // 2, 0, GRID_W - WIN_COLS)
    in_window = (kc >= c0) & (kc < c0 + WIN_COLS)
    dc = kc - c + (WIN_COLS - 1)
    masked = jnp.full((GRID_W, GRID_W), MASK_VALUE, F32)
    tables = {}

    def table(dr):
        if dr not in tables:
            t = jnp.zeros((GRID_W, GRID_W), F32)
            for j in range(N_DCOL):
                t = jnp.where(dc == j, b_ref[base + dr * N_DCOL + j] * LOG2E, t)
            tables[dr] = jnp.where(in_window, t, MASK_VALUE)
        return tables[dr]

    kh = min(WIN_ROWS_MAX, rows)
    for ty, blk in enumerate((0, 1, rows // NAT_QROWS - 1)):
        r_first = NAT_QROWS * blk
        kr0 = min(max(r_first - WIN_ROWS_MAX // 2, 0), rows - NAT_KROWS)
        for a in range(NAT_QROWS):
            r = r_first + a
            r0 = min(max(r - kh // 2, 0), rows - kh)
            for i in range(NAT_KROWS):
                kr = kr0 + i
                blkval = table(kr - r + WIN_ROWS_MAX - 1) if r0 <= kr < r0 + kh else masked
                o_ref[ty, a * GRID_W:(a + 1) * GRID_W, i * GRID_W:(i + 1) * GRID_W] = blkval


def _nat_bias(na_bias, rows):
    depth = na_bias.shape[0]
    tq, nk = NAT_QROWS * GRID_W, NAT_KROWS * GRID_W
    return pl.pallas_call(
        functools.partial(_nat_bias_body, rows=rows),
        grid=(depth, HEADS_C),
        in_specs=[pl.BlockSpec(memory_space=pltpu.SMEM)],
        out_specs=pl.BlockSpec((None, None, 3, tq, nk), lambda l, h: (l, h, 0, 0, 0)),
        out_shape=jax.ShapeDtypeStruct((depth, HEADS_C, 3, tq, nk), F32),
        compiler_params=_params("parallel", "parallel"),
        name="nat_bias",
    )(na_bias.reshape(-1))


def _neighbourhood_attention(q, ks, vs, kc, vc, bias, layer):
    b, n, _ = q.shape
    nblk = n // (NAT_QROWS * GRID_W)
    tq = NAT_QROWS * GRID_W
    mc = kc.shape[2]
    d = HEAD_DIM

    def bias_map(bi, h, r):
        return (layer, h, jnp.where(r == 0, 0, jnp.where(r == nblk - 1, 2, 1)), 0, 0)

    return pl.pallas_call(
        _nat_body,
        grid=(b, HEADS_C, nblk),
        in_specs=[
            pl.BlockSpec((None, tq, d), lambda bi, h, r: (bi, r, h)),
            pl.BlockSpec((None, n, d), lambda bi, h, r: (bi, 0, h)),
            pl.BlockSpec((None, n, d), lambda bi, h, r: (bi, 0, h)),
            pl.BlockSpec((None, None, mc, d), lambda bi, h, r: (bi, layer, 0, h)),
            pl.BlockSpec((None, None, mc, d), lambda bi, h, r: (bi, layer, 0, h)),
            pl.BlockSpec((None, None, None, tq, NAT_KROWS * GRID_W), bias_map),
        ],
        out_specs=pl.BlockSpec((None, tq, d), lambda bi, h, r: (bi, r, h)),
        out_shape=jax.ShapeDtypeStruct((b, n, HEADS_C * d), BF16),
        compiler_params=_params("parallel", "parallel", "arbitrary"),
        name="neighbourhood_attention",
    )(q, ks, vs, kc, vc, bias)


def _cache_kv_body(ckv_ref, kr_ref, wkk_ref, wkv_ref, kb_o, vb_o):
    ckvb = ckv_ref[...].astype(BF16)
    krb = kr_ref[...].astype(BF16)
    kn = jnp.dot(ckvb, wkk_ref[...], preferred_element_type=F32)
    zeros = jnp.zeros((krb.shape[0], QB_PAD - 128 - ROPE_DIM_B), BF16)
    for hd in range(HEADS_B):
        lo = hd * QB_PAD
        kb_o[:, lo:lo + 128] = kn[:, hd * 128:(hd + 1) * 128].astype(BF16)
        kb_o[:, lo + 128:lo + 128 + ROPE_DIM_B] = krb
        kb_o[:, lo + 128 + ROPE_DIM_B:lo + QB_PAD] = zeros
    vb_o[...] = jnp.dot(ckvb, wkv_ref[...], preferred_element_type=F32).astype(BF16)


def _cache_mla_kv(cache_ckv, cache_krope, w_kv_k, w_kv_v):
    b, depth, m, _ = cache_ckv.shape
    return pl.pallas_call(
        _cache_kv_body,
        grid=(b, depth),
        in_specs=[
            pl.BlockSpec((None, None, m, KV_RANK_B), lambda bi, l: (bi, l, 0, 0)),
            pl.BlockSpec((None, None, m, ROPE_DIM_B), lambda bi, l: (bi, l, 0, 0)),
            pl.BlockSpec((None, KV_RANK_B, HEADS_B * 128), lambda bi, l: (l, 0, 0)),
            pl.BlockSpec((None, KV_RANK_B, HEADS_B * 128), lambda bi, l: (l, 0, 0)),
        ],
        out_specs=[
            pl.BlockSpec((None, None, m, HEADS_B * QB_PAD), lambda bi, l: (bi, l, 0, 0)),
            pl.BlockSpec((None, None, m, WIDTH_B), lambda bi, l: (bi, l, 0, 0)),
        ],
        out_shape=[
            jax.ShapeDtypeStruct((b, depth, m, HEADS_B * QB_PAD), BF16),
            jax.ShapeDtypeStruct((b, depth, m, WIDTH_B), BF16),
        ],
        compiler_params=_params("parallel", "parallel"),
        name="cache_mla_kv",
    )(cache_ckv, cache_krope, w_kv_k, w_kv_v)


def _out_body(oa_ref, ob_ref, oc_ref, wa_ref, wb_ref, wc_ref, x_ref, gate_ref, sh_ref, sc_ref, g_ref, wr_ref,
              x1_o, h2_o, lg_o):
    o = jnp.dot(oa_ref[...], wa_ref[...], preferred_element_type=F32)
    o = o + jnp.dot(ob_ref[...], wb_ref[...], preferred_element_type=F32)
    o = o + jnp.dot(oc_ref[...], wc_ref[...], preferred_element_type=F32)
    x1 = x_ref[...] + gate_ref[...] * o
    x1_o[...] = x1
    h = _rms(x1, g_ref[...]) * (1.0 + sc_ref[...]) + sh_ref[...]
    hb = h.astype(BF16)
    h2_o[...] = hb
    lg = jnp.dot(hb, wr_ref[...], preferred_element_type=F32)
    for c in range(lg.shape[0] // 128):
        lg_o[:, c * 128:(c + 1) * 128] = lg[c * 128:(c + 1) * 128, :].T[:N_EXPERTS, :]


def _output_side(o_a, o_b, o_c, x, mod, layer, wts):
    b, n, d = x.shape
    tm = TOKEN_TILE
    mb = mod.shape[0]
    bsel = (lambda i: i) if mb > 1 else (lambda i: 0)

    def modspec(k):
        return pl.BlockSpec((None, None, 1, d), lambda bi, i: (bsel(bi), k, 0, 0))

    def tok(w):
        return pl.BlockSpec((None, tm, w), lambda bi, i: (bi, i, 0))

    return pl.pallas_call(
        _out_body,
        grid=(b, n // tm),
        in_specs=[
            tok(WIDTH_A), tok(WIDTH_B), tok(WIDTH_C),
            _resident((None, WIDTH_A, d), lambda bi, i: (layer, 0, 0)),
            _resident((None, WIDTH_B, d), lambda bi, i: (layer, 0, 0)),
            _resident((None, WIDTH_C, d), lambda bi, i: (layer, 0, 0)),
            tok(d), modspec(2), modspec(3), modspec(4),
            pl.BlockSpec((None, 1, d), lambda bi, i: (layer, 0, 0)),
            pl.BlockSpec((None, d, 128), lambda bi, i: (layer, 0, 0)),
        ],
        out_specs=[tok(d), tok(d), pl.BlockSpec((None, N_EXPERTS, tm), lambda bi, i: (bi, 0, i))],
        out_shape=[
            jax.ShapeDtypeStruct((b, n, d), F32),
            jax.ShapeDtypeStruct((b, n, d), BF16),
            jax.ShapeDtypeStruct((b, N_EXPERTS, n), F32),
        ],
        compiler_params=_params("parallel", "parallel"),
        name="output_side",
    )(o_a, o_b, o_c, wts["w_out_a"], wts["w_out_b"], wts["w_out_c"], x, mod, mod, mod, wts["norm2_g"],
      wts["w_router"])


def _prefix_exclusive(mask):
    e, n = mask.shape
    ones = jnp.where(mask, 1.0, 0.0)
    rr = lax.broadcasted_iota(jnp.int32, (128, 128), 0)
    cc = lax.broadcasted_iota(jnp.int32, (128, 128), 1)
    tri = jnp.where(rr <= cc, 1.0, 0.0).astype(BF16)
    carry = jnp.zeros((e, 1), F32)
    outs = []
    for c in range(n // 128):
        blk = ones[:, c * 128:(c + 1) * 128]
        inc = jnp.dot(blk.astype(BF16), tri, preferred_element_type=F32)
        outs.append(inc - blk + carry)
        carry = carry + inc[:, 127:128]
    return jnp.concatenate(outs, axis=1)


def _router_body(lg_ref, rank_o, rank_t_o, aff_t_o, cum_o, *, cap, chunk):
    lg = lg_ref[...]
    e, n = lg.shape
    ex = jnp.exp(lg - jnp.max(lg, axis=0, keepdims=True))
    aff = ex / jnp.sum(ex, axis=0, keepdims=True)
    key = pltpu.bitcast(aff, jnp.int32)

    def step(i, t):
        cand = t | lax.shift_left(jnp.int32(1), 30 - i)
        cnt = jnp.sum(jnp.where(key >= cand, 1.0, 0.0), axis=1, keepdims=True)
        return jnp.where(cnt >= cap, cand, t)

    thr = lax.fori_loop(0, 31, step, jnp.zeros((e, 1), jnp.int32))
    above = key > thr
    tied = key == thr
    need = cap - jnp.sum(jnp.where(above, 1.0, 0.0), axis=1, keepdims=True)
    chosen = above | (tied & (_prefix_exclusive(tied) < need))
    before = _prefix_exclusive(chosen)
    rank = jnp.where(chosen, before, -1.0)
    rank_o[...] = rank.astype(jnp.int32)

    lane = lax.broadcasted_iota(jnp.int32, (e, 128), 1)
    cum = jnp.full((e, 128), float(cap), F32)
    for k in range(n // chunk):
        cum = jnp.where(lane == k, before[:, k * chunk:k * chunk + 1], cum)
    cum_o[...] = cum.astype(jnp.int32)

    fill = jnp.full((128 - e, n), -1.0, F32)
    rank_p = jnp.concatenate([rank, fill], axis=0)
    aff_p = jnp.concatenate([aff, fill], axis=0)
    for c in range(n // 128):
        rank_t_o[c * 128:(c + 1) * 128, :] = rank_p[:, c * 128:(c + 1) * 128].T
        aff_t_o[c * 128:(c + 1) * 128, :] = aff_p[:, c * 128:(c + 1) * 128].T


def _route(logits_t):
    b, e, n = logits_t.shape
    cap = CAPACITY_FACTOR * n // e
    return pl.pallas_call(
        functools.partial(_router_body, cap=cap, chunk=min(n, GATHER_CHUNK)),
        grid=(b,),
        in_specs=[pl.BlockSpec((None, e, n), lambda bi: (bi, 0, 0))],
        out_specs=[
            pl.BlockSpec((None, e, n), lambda bi: (bi, 0, 0)),
            pl.BlockSpec((None, n, 128), lambda bi: (bi, 0, 0)),
            pl.BlockSpec((None, n, 128), lambda bi: (bi, 0, 0)),
            pl.BlockSpec((None, e, 128), lambda bi: (bi, 0, 0)),
        ],
        out_shape=[
            jax.ShapeDtypeStruct((b, e, n), jnp.int32),
            jax.ShapeDtypeStruct((b, n, 128), F32),
            jax.ShapeDtypeStruct((b, n, 128), F32),
            jax.ShapeDtypeStruct((b, e, 128), jnp.int32),
        ],
        compiler_params=_params("parallel"),
        name="route",
    )(logits_t)


def _gather_body(cum_ref, h_ref, rank_ref, xe_o, *, chunk, rows):
    n, d = h_ref.shape
    cap = xe_o.shape[0]
    nch = n // chunk
    base = (pl.program_id(0) * pl.num_programs(1) + pl.program_id(1)) * (nch + 1)
    rank = rank_ref[...]
    xe_o[...] = jnp.zeros(xe_o.shape, xe_o.dtype)
    for c in range(nch):
        lo = cum_ref[base + c]
        hi = cum_ref[base + c + 1]
        rank_c = rank[:, c * chunk:(c + 1) * chunk]
        for jb in range(cap // rows):

            @pl.when((lo < (jb + 1) * rows) & (hi > jb * rows))
            def _():
                slot = lax.broadcasted_iota(jnp.int32, (rows, chunk), 0) + jb * rows
                onehot = jnp.where(slot == rank_c, 1.0, 0.0).astype(BF16)
                picked = jnp.dot(onehot, h_ref[c * chunk:(c + 1) * chunk, :], preferred_element_type=F32)
                xe_o[jb * rows:(jb + 1) * rows, :] += picked.astype(xe_o.dtype)


def _gather(h2, rank, cum):
    b, n, d = h2.shape
    e = rank.shape[1]
    cap = CAPACITY_FACTOR * n // e
    chunk = min(n, GATHER_CHUNK)
    cum_flat = cum[:, :, :n // chunk + 1].reshape(-1)
    return pl.pallas_call(
        functools.partial(_gather_body, chunk=chunk, rows=min(cap, GATHER_ROWS)),
        grid_spec=pltpu.PrefetchScalarGridSpec(
            num_scalar_prefetch=1,
            grid=(b, e),
            in_specs=[
                pl.BlockSpec((None, n, d), lambda bi, ei, cum_ref: (bi, 0, 0)),
                pl.BlockSpec((None, None, 1, n), lambda bi, ei, cum_ref: (bi, ei, 0, 0)),
            ],
            out_specs=pl.BlockSpec((None, None, cap, d), lambda bi, ei, cum_ref: (ei, bi, 0, 0)),
        ),
        out_shape=jax.ShapeDtypeStruct((e, b, cap, d), BF16),
        compiler_params=_params("parallel", "parallel"),
        name="gather",
    )(cum_flat, h2, rank)


def _ffn_body(xe_ref, wg_ref, wu_ref, wd_ref, ye_o):
    bb, cap, d = xe_ref.shape
    xe = xe_ref[...].reshape(bb * cap, d)
    g = jnp.dot(xe, wg_ref[...], preferred_element_type=F32)
    u = jnp.dot(xe, wu_ref[...], preferred_element_type=F32)
    hid = (g / (1.0 + jnp.exp(-g)) * u).astype(BF16)
    ye = jnp.dot(hid, wd_ref[...], preferred_element_type=F32)
    ye_o[...] = ye.astype(ye_o.dtype).reshape(bb, cap, d)


def _expert_ffn(xe, layer, wts):
    e, b, cap, d = xe.shape
    ff = wts["w_gate"].shape[-1]
    bb = max(1, min(b, 512 // cap))
    return pl.pallas_call(
        _ffn_body,
        grid=(e, b // bb),
        in_specs=[
            pl.BlockSpec((None, bb, cap, d), lambda ei, bi: (ei, bi, 0, 0)),
            pl.BlockSpec((None, None, d, ff), lambda ei, bi: (layer, ei, 0, 0)),
            pl.BlockSpec((None, None, d, ff), lambda ei, bi: (layer, ei, 0, 0)),
            pl.BlockSpec((None, None, ff, d), lambda ei, bi: (layer, ei, 0, 0)),
        ],
        out_specs=pl.BlockSpec((None, bb, cap, d), lambda ei, bi: (ei, bi, 0, 0)),
        out_shape=jax.ShapeDtypeStruct((e, b, cap, d), BF16),
        compiler_params=_params("parallel", "arbitrary"),
        name="expert_ffn",
    )(xe, wts["w_gate"], wts["w_up"], wts["w_down"])


def _scatter_body(ye_ref, x_ref, rank_ref, aff_ref, gate_ref, o_ref):
    e, cap, dh = ye_ref.shape
    tn = x_ref.shape[0]
    slot = lax.broadcasted_iota(jnp.int32, (tn, cap), 1).astype(F32)
    acc = jnp.zeros((tn, dh), F32)
    for ei in range(e):
        onehot = jnp.where(rank_ref[:, ei:ei + 1] == slot, 1.0, 0.0).astype(BF16)
        contrib = jnp.dot(onehot, ye_ref[ei], preferred_element_type=F32)
        acc = acc + contrib * aff_ref[:, ei:ei + 1]
    o_ref[...] = x_ref[...] + gate_ref[...] * acc


def _scatter(ye, x1, rank_t, aff_t, mod):
    e, b, cap, d = ye.shape
    n = x1.shape[1]
    tn = min(n, 512)
    dh = d // 2
    mb = mod.shape[0]
    bsel = (lambda i: i) if mb > 1 else (lambda i: 0)
    return pl.pallas_call(
        _scatter_body,
        grid=(b, 2, n // tn),
        in_specs=[
            pl.BlockSpec((e, None, cap, dh), lambda bi, j, i: (0, bi, 0, j)),
            pl.BlockSpec((None, tn, dh), lambda bi, j, i: (bi, i, j)),
            pl.BlockSpec((None, tn, 128), lambda bi, j, i: (bi, i, 0)),
            pl.BlockSpec((None, tn, 128), lambda bi, j, i: (bi, i, 0)),
            pl.BlockSpec((None, None, 1, dh), lambda bi, j, i: (bsel(bi), 5, 0, j)),
        ],
        out_specs=pl.BlockSpec((None, tn, dh), lambda bi, j, i: (bi, i, j)),
        out_shape=jax.ShapeDtypeStruct(x1.shape, F32),
        compiler_params=_params("parallel", "parallel", "parallel"),
        name="scatter",
    )(ye, x1, rank_t, aff_t, mod)


def _final_norm_body(x_ref, g_ref, o_ref):
    o_ref[...] = _rms(x_ref[...], g_ref[...])


def _final_norm(x, g):
    b, n, d = x.shape
    tm = TOKEN_TILE
    return pl.pallas_call(
        _final_norm_body,
        grid=(b, n // tm),
        in_specs=[pl.BlockSpec((None, tm, d), lambda bi, i: (bi, i, 0)), pl.BlockSpec((1, d), lambda bi, i: (0, 0))],
        out_specs=pl.BlockSpec((None, tm, d), lambda bi, i: (bi, i, 0)),
        out_shape=jax.ShapeDtypeStruct(x.shape, F32),
        compiler_params=_params("parallel", "parallel"),
        name="final_norm",
    )(x, g.reshape(1, d))


def _rope_tables(n_tokens, rot_dim):
    t = jnp.arange(n_tokens, dtype=jnp.int32)
    row = (t // GRID_W).astype(F32)
    col = (t % GRID_W).astype(F32)
    axis_dim = rot_dim // 2
    freqs = ROPE_THETA ** (-jnp.arange(0, axis_dim, 2, dtype=F32) / axis_dim)
    ang = jnp.concatenate([row[:, None] * freqs[None, :], col[:, None] * freqs[None, :]], axis=-1)
    cos, sin = jnp.cos(ang), jnp.sin(ang)
    cos2 = jnp.repeat(cos, 2, axis=-1)
    sin2 = jnp.stack([-sin, sin], axis=-1).reshape(n_tokens, rot_dim)
    pad = 128 - rot_dim
    if pad:
        cos2 = jnp.concatenate([cos2, jnp.ones((n_tokens, pad), F32)], axis=-1)
        sin2 = jnp.concatenate([sin2, jnp.zeros((n_tokens, pad), F32)], axis=-1)
    return cos2, sin2


def _prepare_weights(norm1_g, norm2_g, w_in, qa_norm_g, ka_norm_g, q_norm_b, kv_norm_b, w_q_up, w_kv_up, w_out,
                     w_router, w_gate, w_up, w_down):
    depth, d, _ = w_in.shape
    kr_lo = SEG_CKV[1]
    w_in_p = jnp.concatenate(
        [w_in[:, :, :kr_lo], w_in[:, :, kr_lo + ROPE_DIM_B:], w_in[:, :, kr_lo:kr_lo + ROPE_DIM_B],
         jnp.zeros((depth, d, 128 - ROPE_DIM_B), w_in.dtype)], axis=-1).astype(BF16)
    wq = w_q_up.reshape(depth, Q_RANK_B, HEADS_B, NOPE_DIM_B + ROPE_DIM_B)
    wq = jnp.pad(wq, ((0, 0), (0, 0), (0, 0), (0, QB_PAD - NOPE_DIM_B - ROPE_DIM_B)))
    wkv = w_kv_up.reshape(depth, KV_RANK_B, HEADS_B, NOPE_DIM_B + V_DIM_B)
    return {
        "norm1_g": norm1_g.reshape(depth, 1, d),
        "norm2_g": norm2_g.reshape(depth, 1, d),
        "w_in": w_in_p,
        "qa_g": qa_norm_g.reshape(depth, 1, HEAD_DIM),
        "ka_g": ka_norm_g.reshape(depth, 1, HEAD_DIM),
        "qn_g": q_norm_b.reshape(depth, 1, Q_RANK_B),
        "kvn_g": kv_norm_b.reshape(depth, 1, KV_RANK_B),
        "w_q_up": wq.reshape(depth, Q_RANK_B, HEADS_B * QB_PAD).astype(BF16),
        "w_kv_k": wkv[..., :NOPE_DIM_B].reshape(depth, KV_RANK_B, HEADS_B * NOPE_DIM_B).astype(BF16),
        "w_kv_v": wkv[..., NOPE_DIM_B:].reshape(depth, KV_RANK_B, HEADS_B * V_DIM_B).astype(BF16),
        "w_out_a": w_out[:, :WIDTH_A].astype(BF16),
        "w_out_b": w_out[:, WIDTH_A:WIDTH_A + WIDTH_B].astype(BF16),
        "w_out_c": w_out[:, WIDTH_A + WIDTH_B:].astype(BF16),
        "w_router": jnp.pad(w_router, ((0, 0), (0, 0), (0, 128 - N_EXPERTS))).astype(BF16),
        "w_gate": w_gate.astype(BF16),
        "w_up": w_up.astype(BF16),
        "w_down": w_down.astype(BF16),
    }


def _moe(h2, logits_t, x1, mod, layer, wts):
    rank, rank_t, aff_t, cum = _route(logits_t)
    xe = _gather(h2, rank.reshape(rank.shape[0], rank.shape[1], 1, rank.shape[2]), cum)
    ye = _expert_ffn(xe, layer, wts)
    return _scatter(ye, x1, rank_t, aff_t, mod)


def kernel(x_prompt, x_sample, cache_a_k, cache_a_v, cache_b_ckv, cache_b_krope, cache_c_k, cache_c_v, c, c_ctx, w_ada, b_ada, norm1_g, norm2_g, w_in, qa_norm_g, ka_norm_g, q_norm_b, kv_norm_b, w_q_up, w_kv_up, na_bias, w_out, w_router, w_gate, w_up, w_down, final_norm_g):
    depth, d, _ = w_in.shape
    bp, seq, _ = x_prompt.shape
    bs, n_lat, _ = x_sample.shape
    past = cache_a_k.shape[2]
    rows = n_lat // GRID_W

    wts = _prepare_weights(norm1_g, norm2_g, w_in, qa_norm_g, ka_norm_g, q_norm_b, kv_norm_b, w_q_up, w_kv_up,
                           w_out, w_router, w_gate, w_up, w_down)
    rope_tabs = _rope_tables(n_lat, HEAD_DIM) + _rope_tables(n_lat, ROPE_DIM_B)

    cond8 = jnp.concatenate([c_ctx[None], c, jnp.zeros((8 - 1 - bs, d), F32)], axis=0)
    mods = _modulation(cond8, w_ada, b_ada)
    mods = mods.reshape(depth, 8, 6, 1, d)

    ca_k = cache_a_k.reshape(bs, depth, past, KV_A).astype(BF16)
    ca_v = cache_a_v.reshape(bs, depth, past, KV_A).astype(BF16)
    cc_k = cache_c_k.reshape(bs, depth, past, WIDTH_C).astype(BF16)
    cc_v = cache_c_v.reshape(bs, depth, past, WIDTH_C).astype(BF16)
    cb_k, cb_v = _cache_mla_kv(cache_b_ckv, cache_b_krope, wts["w_kv_k"], wts["w_kv_v"])
    nat_bias = _nat_bias(na_bias, rows)

    xp, xs = x_prompt, x_sample
    states = []
    for l in range(depth):
        mod_c = mods[l, 0:1]
        mod_l = mods[l, 1:1 + bs]

        qa, ka, va, qb, kb, vb, qc, kc, vc, st = _input_side(xp, mod_c, l, wts, None, True)
        o_a = _attention(qa, ka, va, None, None, l, HEADS_A, HEADS_A // KV_HEADS_A, HEAD_DIM, HEAD_DIM)
        o_b = _attention(qb, kb, vb, None, None, l, HEADS_B, 1, QB_PAD, V_DIM_B)
        o_c = _attention(qc, kc, vc, None, None, l, HEADS_C, 1, HEAD_DIM, HEAD_DIM)
        x1, h2, lg = _output_side(o_a, o_b, o_c, xp, mod_c, l, wts)
        xp = _moe(h2, lg, x1, mod_c, l, wts)
        states.append(st)

        qa, ka, va, qb, kb, vb, qc, kc, vc = _input_side(xs, mod_l, l, wts, rope_tabs, False)
        o_a = _attention(qa, ka, va, ca_k, ca_v, l, HEADS_A, HEADS_A // KV_HEADS_A, HEAD_DIM, HEAD_DIM)
        o_b = _attention(qb, kb, vb, cb_k, cb_v, l, HEADS_B, 1, QB_PAD, V_DIM_B)
        o_c = _neighbourhood_attention(qc, kc, vc, cc_k, cc_v, nat_bias, l)
        x1, h2, lg = _output_side(o_a, o_b, o_c, xs, mod_l, l, wts)
        xs = _moe(h2, lg, x1, mod_l, l, wts)

    y_prompt = _final_norm(xp, final_norm_g)
    y_sample = _final_norm(xs, final_norm_g)
    st = jnp.stack(states, axis=1)
    new_a_k = st[..., ST_KA[0]:ST_KA[1]].reshape(bp, depth, seq, KV_HEADS_A, HEAD_DIM)
    new_a_v = st[..., ST_VA[0]:ST_VA[1]].reshape(bp, depth, seq, KV_HEADS_A, HEAD_DIM)
    new_b_ckv = st[..., ST_CKV[0]:ST_CKV[1]]
    new_b_krope = st[..., ST_KR[0]:ST_KR[0] + ROPE_DIM_B]
    new_c_k = st[..., ST_KC[0]:ST_KC[1]].reshape(bp, depth, seq, HEADS_C, HEAD_DIM)
    new_c_v = st[..., ST_VC[0]:ST_VC[1]].reshape(bp, depth, seq, HEADS_C, HEAD_DIM)
    return (y_prompt, y_sample, new_a_k, new_a_v, new_b_ckv, new_b_krope, new_c_k, new_c_v)
```

```python
import functools

import jax
import jax.numpy as jnp
from jax import lax
from jax.experimental import pallas as pl
from jax.experimental.pallas import tpu as pltpu

F32 = jnp.float32
BF16 = jnp.bfloat16

GRID_W = 64
HEAD_DIM = 128
HEADS_A = 6
KV_HEADS_A = 2
HEADS_B = 5
Q_RANK_B = 512
KV_RANK_B = 256
NOPE_DIM_B = 128
ROPE_DIM_B = 64
V_DIM_B = 128
HEADS_C = 5
WIN_ROWS_MAX = 8
WIN_COLS = 16
N_EXPERTS = 16
CAPACITY_FACTOR = 2
ROPE_THETA = 10000.0
EPS = 1e-6

QB_PAD = 256
WIDTH_A = HEADS_A * HEAD_DIM
WIDTH_B = HEADS_B * V_DIM_B
WIDTH_C = HEADS_C * HEAD_DIM
KV_A = KV_HEADS_A * HEAD_DIM

SEG_QA = (0, WIDTH_A)
SEG_KA = (SEG_QA[1], SEG_QA[1] + KV_A)
SEG_VA = (SEG_KA[1], SEG_KA[1] + KV_A)
SEG_CQ = (SEG_VA[1], SEG_VA[1] + Q_RANK_B)
SEG_CKV = (SEG_CQ[1], SEG_CQ[1] + KV_RANK_B)
SEG_QC = (SEG_CKV[1], SEG_CKV[1] + WIDTH_C)
SEG_KC = (SEG_QC[1], SEG_QC[1] + WIDTH_C)
SEG_VC = (SEG_KC[1], SEG_KC[1] + WIDTH_C)
SEG_KR = (SEG_VC[1], SEG_VC[1] + 128)
IN_COLS_P = SEG_KR[1]

ST_KA = (0, KV_A)
ST_VA = (ST_KA[1], ST_KA[1] + KV_A)
ST_CKV = (ST_VA[1], ST_VA[1] + KV_RANK_B)
ST_KC = (ST_CKV[1], ST_CKV[1] + WIDTH_C)
ST_VC = (ST_KC[1], ST_KC[1] + WIDTH_C)
ST_KR = (ST_VC[1], ST_VC[1] + 128)
ST_COLS = ST_KR[1]

NAT_QROWS = 4
NAT_KROWS = 12
MASK_VALUE = -1e30
LOG2E = 1.4426950408889634

TOKEN_TILE = 256
ATTN_GROUPS = 4
GATHER_CHUNK = 512
GATHER_ROWS = 128
SCATTER_TOKENS = 128
VMEM_LIMIT = 56 * 1024 * 1024
NT_DIMS = (((1,), (1,)), ((), ()))


def _params(*sem):
    return pltpu.CompilerParams(dimension_semantics=sem, vmem_limit_bytes=VMEM_LIMIT)


def _resident(block_shape, index_map):
    return pl.BlockSpec(block_shape, index_map, pipeline_mode=pl.Buffered(1))


def _rms(x, g):
    ms = jnp.mean(x * x, axis=-1, keepdims=True)
    return x * lax.rsqrt(ms + EPS) * g


def _swap_pairs(x):
    lane = lax.broadcasted_iota(jnp.int32, x.shape, x.ndim - 1)
    nxt = pltpu.roll(x, x.shape[-1] - 1, x.ndim - 1)
    prv = pltpu.roll(x, 1, x.ndim - 1)
    return jnp.where((lane & 1) == 0, nxt, prv)


def _rope(x, cos, sin_signed):
    return x * cos + _swap_pairs(x) * sin_signed


def _mod_body(c_ref, w_ref, b_ref, o_ref):
    c = c_ref[...]
    s = (c / (1.0 + jnp.exp(-c))).astype(BF16)
    o_ref[...] = jnp.dot(s, w_ref[...].astype(BF16), preferred_element_type=F32) + b_ref[...]


def _modulation(cond8, w_ada, b_ada):
    depth, d, cols = w_ada.shape
    tn = 1024
    return pl.pallas_call(
        _mod_body,
        grid=(depth, cols // tn),
        in_specs=[
            pl.BlockSpec((8, d), lambda l, j: (0, 0)),
            pl.BlockSpec((None, d, tn), lambda l, j: (l, 0, j)),
            pl.BlockSpec((None, 1, tn), lambda l, j: (l, 0, j)),
        ],
        out_specs=pl.BlockSpec((None, 8, tn), lambda l, j: (l, 0, j)),
        out_shape=jax.ShapeDtypeStruct((depth, 8, cols), F32),
        compiler_params=_params("parallel", "parallel"),
        name="modulation",
    )(cond8, w_ada, b_ada.reshape(depth, 1, cols))


def _in_body(*refs, rope, state):
    it = iter(refs)
    x_ref, sh_ref, sc_ref, g_ref, win_ref = (next(it) for _ in range(5))
    qag_ref, kag_ref, qng_ref, kvng_ref = (next(it) for _ in range(4))
    wq_ref, wkk_ref, wkv_ref = (next(it) for _ in range(3))
    if rope:
        ca_ref, sa_ref, cb_ref, sb_ref = (next(it) for _ in range(4))
    qa_o, ka_o, va_o, qb_o, kb_o, vb_o, qc_o, kc_o, vc_o = (next(it) for _ in range(9))
    st_o = next(it) if state else None

    h = _rms(x_ref[...], g_ref[...]) * (1.0 + sc_ref[...]) + sh_ref[...]
    hb = h.astype(BF16)

    def seg(bounds):
        return jnp.dot(hb, win_ref[:, bounds[0]:bounds[1]], preferred_element_type=F32)

    def rope_a(y):
        return _rope(y, ca_ref[...], sa_ref[...]) if rope else y

    def rope_b(y):
        return _rope(y, cb_ref[...], sb_ref[...]) if rope else y

    qa = seg(SEG_QA)
    for hd in range(HEADS_A):
        lo = hd * HEAD_DIM
        y = rope_a(_rms(qa[:, lo:lo + HEAD_DIM], qag_ref[...]))
        qa_o[:, lo:lo + HEAD_DIM] = (y * (HEAD_DIM ** -0.5 * LOG2E)).astype(BF16)
    ka = seg(SEG_KA)
    for hd in range(KV_HEADS_A):
        lo = hd * HEAD_DIM
        y = rope_a(_rms(ka[:, lo:lo + HEAD_DIM], kag_ref[...]))
        ka_o[:, lo:lo + HEAD_DIM] = y.astype(BF16)
        if state:
            st_o[:, ST_KA[0] + lo:ST_KA[0] + lo + HEAD_DIM] = y
    va = seg(SEG_VA)
    va_o[...] = va.astype(BF16)

    cq = _rms(seg(SEG_CQ), qng_ref[...]).astype(BF16)
    qb = jnp.dot(cq, wq_ref[...], preferred_element_type=F32)
    qscale = (NOPE_DIM_B + ROPE_DIM_B) ** -0.5 * LOG2E
    for hd in range(HEADS_B):
        lo = hd * QB_PAD
        qb_o[:, lo:lo + 128] = (qb[:, lo:lo + 128] * qscale).astype(BF16)
        qb_o[:, lo + 128:lo + 256] = (rope_b(qb[:, lo + 128:lo + 256]) * qscale).astype(BF16)
    ckv = _rms(seg(SEG_CKV), kvng_ref[...])
    ckvb = ckv.astype(BF16)
    kr = rope_b(seg(SEG_KR))
    krb = kr.astype(BF16)
    kn = jnp.dot(ckvb, wkk_ref[...], preferred_element_type=F32)
    for hd in range(HEADS_B):
        kb_o[:, hd * QB_PAD:hd * QB_PAD + 128] = kn[:, hd * 128:(hd + 1) * 128].astype(BF16)
        kb_o[:, hd * QB_PAD + 128:(hd + 1) * QB_PAD] = krb
    vb_o[...] = jnp.dot(ckvb, wkv_ref[...], preferred_element_type=F32).astype(BF16)

    qc_o[...] = (seg(SEG_QC) * (HEAD_DIM ** -0.5 * LOG2E)).astype(BF16)
    kc = seg(SEG_KC)
    kc_o[...] = kc.astype(BF16)
    vc = seg(SEG_VC)
    vc_o[...] = vc.astype(BF16)

    if state:
        st_o[:, ST_VA[0]:ST_VA[1]] = va
        st_o[:, ST_CKV[0]:ST_CKV[1]] = ckv
        st_o[:, ST_KC[0]:ST_KC[1]] = kc
        st_o[:, ST_VC[0]:ST_VC[1]] = vc
        st_o[:, ST_KR[0]:ST_KR[1]] = kr


def _input_side(x, mod, layer, wts, rope_tabs, state):
    b, n, d = x.shape
    tm = TOKEN_TILE
    mb = mod.shape[0]
    bsel = (lambda i: i) if mb > 1 else (lambda i: 0)
    rope = rope_tabs is not None

    def modspec(k):
        return pl.BlockSpec((None, None, 1, d), lambda bi, i: (bsel(bi), k, 0, 0))

    def vec(w):
        return pl.BlockSpec((None, 1, w), lambda bi, i: (layer, 0, 0))

    in_specs = [
        pl.BlockSpec((None, tm, d), lambda bi, i: (bi, i, 0)),
        modspec(0), modspec(1), vec(d),
        _resident((None, d, IN_COLS_P), lambda bi, i: (layer, 0, 0)),
        vec(HEAD_DIM), vec(HEAD_DIM), vec(Q_RANK_B), vec(KV_RANK_B),
        _resident((None, Q_RANK_B, HEADS_B * QB_PAD), lambda bi, i: (layer, 0, 0)),
        _resident((None, KV_RANK_B, HEADS_B * 128), lambda bi, i: (layer, 0, 0)),
        _resident((None, KV_RANK_B, HEADS_B * 128), lambda bi, i: (layer, 0, 0)),
    ]
    args = [x, mod, mod, wts["norm1_g"], wts["w_in"], wts["qa_g"], wts["ka_g"], wts["qn_g"], wts["kvn_g"],
            wts["w_q_up"], wts["w_kv_k"], wts["w_kv_v"]]
    if rope:
        in_specs += [pl.BlockSpec((tm, 128), lambda bi, i: (i, 0))] * 4
        args += list(rope_tabs)

    widths = [WIDTH_A, KV_A, KV_A, HEADS_B * QB_PAD, HEADS_B * QB_PAD, WIDTH_B, WIDTH_C, WIDTH_C, WIDTH_C]
    out_shape = [jax.ShapeDtypeStruct((b, n, w), BF16) for w in widths]
    out_specs = [pl.BlockSpec((None, tm, w), lambda bi, i: (bi, i, 0)) for w in widths]
    if state:
        out_shape.append(jax.ShapeDtypeStruct((b, n, ST_COLS), F32))
        out_specs.append(pl.BlockSpec((None, tm, ST_COLS), lambda bi, i: (bi, i, 0)))

    return pl.pallas_call(
        functools.partial(_in_body, rope=rope, state=state),
        grid=(b, n // tm),
        in_specs=in_specs,
        out_specs=out_specs,
        out_shape=out_shape,
        compiler_params=_params("parallel", "parallel"),
        name="input_side",
    )(*args)


def _attn_body(*refs, cache):
    if cache:
        q_ref, ks_ref, vs_ref, kc_ref, vc_ref, o_ref = refs
    else:
        q_ref, ks_ref, vs_ref, o_ref = refs
    rows = min(q_ref.shape[0], TOKEN_TILE)
    groups = q_ref.shape[0] // rows

    def scores(c):
        q = q_ref[c * rows:(c + 1) * rows, :]
        s_self = lax.dot_general(q, ks_ref[...], NT_DIMS, preferred_element_type=F32)
        m = jnp.max(s_self, axis=-1, keepdims=True)
        s_ctx = None
        if cache:
            s_ctx = lax.dot_general(q, kc_ref[...], NT_DIMS, preferred_element_type=F32)
            m = jnp.maximum(m, jnp.max(s_ctx, axis=-1, keepdims=True))
        return s_self, s_ctx, m

    def finish(c, s_self, s_ctx, m):
        p = jnp.exp2(s_self - m)
        l = jnp.sum(p, axis=-1, keepdims=True)
        acc = jnp.dot(p.astype(BF16), vs_ref[...], preferred_element_type=F32)
        if cache:
            p = jnp.exp2(s_ctx - m)
            l = l + jnp.sum(p, axis=-1, keepdims=True)
            acc = acc + jnp.dot(p.astype(BF16), vc_ref[...], preferred_element_type=F32)
        o_ref[c * rows:(c + 1) * rows, :] = (acc / l).astype(o_ref.dtype)

    pending = scores(0)
    for c in range(groups):
        following = scores(c + 1) if c + 1 < groups else None
        finish(c, *pending)
        pending = following


def _attention(q, ks, vs, kc, vc, layer, heads, group, dq, dv):
    b, n, _ = q.shape
    ms = ks.shape[1]
    tq = min(n, ATTN_GROUPS * TOKEN_TILE)
    cache = kc is not None
    in_specs = [
        pl.BlockSpec((None, tq, dq), lambda bi, h, i: (bi, i, h)),
        pl.BlockSpec((None, ms, dq), lambda bi, h, i: (bi, 0, h // group)),
        pl.BlockSpec((None, ms, dv), lambda bi, h, i: (bi, 0, h // group)),
    ]
    args = [q, ks, vs]
    if cache:
        mc = kc.shape[2]
        in_specs += [
            pl.BlockSpec((None, None, mc, dq), lambda bi, h, i: (bi, layer, 0, h // group)),
            pl.BlockSpec((None, None, mc, dv), lambda bi, h, i: (bi, layer, 0, h // group)),
        ]
        args += [kc, vc]
    return pl.pallas_call(
        functools.partial(_attn_body, cache=cache),
        grid=(b, heads, n // tq),
        in_specs=in_specs,
        out_specs=pl.BlockSpec((None, tq, dv), lambda bi, h, i: (bi, i, h)),
        out_shape=jax.ShapeDtypeStruct((b, n, heads * dv), BF16),
        compiler_params=_params("parallel", "parallel", "parallel"),
        name="attention",
    )(*args)


def _nat_body(q_ref, ks_ref, vs_ref, kc_ref, vc_ref, bias_ref, o_ref):
    tq = NAT_QROWS * GRID_W
    nk = NAT_KROWS * GRID_W
    groups = q_ref.shape[0] // tq
    key_rows = ks_ref.shape[0] // GRID_W
    last = ks_ref.shape[0] // tq - 1

    def scores(c):
        r = pl.program_id(2) * groups + c
        kr0 = jnp.clip(NAT_QROWS * r - WIN_ROWS_MAX // 2, 0, key_rows - NAT_KROWS)
        start = pl.multiple_of(kr0 * GRID_W, tq)
        kind = jnp.where(r == 0, 0, jnp.where(r == last, 2, 1))
        q = q_ref[c * tq:(c + 1) * tq, :]
        s_win = lax.dot_general(q, ks_ref[pl.ds(start, nk), :], NT_DIMS, preferred_element_type=F32)
        s_win = s_win + bias_ref[kind]
        s_ctx = lax.dot_general(q, kc_ref[...], NT_DIMS, preferred_element_type=F32)
        m = jnp.maximum(jnp.max(s_win, axis=-1, keepdims=True), jnp.max(s_ctx, axis=-1, keepdims=True))
        return s_win, s_ctx, m, start

    def finish(c, s_win, s_ctx, m, start):
        p_win = jnp.exp2(s_win - m)
        p_ctx = jnp.exp2(s_ctx - m)
        l = jnp.sum(p_win, axis=-1, keepdims=True) + jnp.sum(p_ctx, axis=-1, keepdims=True)
        acc = jnp.dot(p_win.astype(BF16), vs_ref[pl.ds(start, nk), :], preferred_element_type=F32)
        acc = acc + jnp.dot(p_ctx.astype(BF16), vc_ref[...], preferred_element_type=F32)
        o_ref[c * tq:(c + 1) * tq, :] = (acc / l).astype(o_ref.dtype)

    pending = scores(0)
    for c in range(groups):
        following = scores(c + 1) if c + 1 < groups else None
        finish(c, *pending)
        pending = following


N_DROW = 2 * WIN_ROWS_MAX - 1
N_DCOL = 2 * WIN_COLS - 1


def _nat_bias_body(b_ref, o_ref, *, rows):
    base = (pl.program_id(0) * HEADS_C + pl.program_id(1)) * (N_DROW * N_DCOL)
    c = lax.broadcasted_iota(jnp.int32, (GRID_W, GRID_W), 0)
    kc = lax.broadcasted_iota(jnp.int32, (GRID_W, GRID_W), 1)
    c0 = jnp.clip(c - WIN_COLS // 2, 0, GRID_W - WIN_COLS)
    in_window = (kc >= c0) & (kc < c0 + WIN_COLS)
    dc = kc - c + (WIN_COLS - 1)
    masked = jnp.full((GRID_W, GRID_W), MASK_VALUE, F32)
    tables = {}

    def table(dr):
        if dr not in tables:
            t = jnp.zeros((GRID_W, GRID_W), F32)
            for j in range(N_DCOL):
                t = jnp.where(dc == j, b_ref[base + dr * N_DCOL + j] * LOG2E, t)
            tables[dr] = jnp.where(in_window, t, MASK_VALUE)
        return tables[dr]

    kh = min(WIN_ROWS_MAX, rows)
    for ty, blk in enumerate((0, 1, rows // NAT_QROWS - 1)):
        r_first = NAT_QROWS * blk
        kr0 = min(max(r_first - WIN_ROWS_MAX // 2, 0), rows - NAT_KROWS)
        for a in range(NAT_QROWS):
            r = r_first + a
            r0 = min(max(r - kh // 2, 0), rows - kh)
            for i in range(NAT_KROWS):
                kr = kr0 + i
                blkval = table(kr - r + WIN_ROWS_MAX - 1) if r0 <= kr < r0 + kh else masked
                o_ref[ty, a * GRID_W:(a + 1) * GRID_W, i * GRID_W:(i + 1) * GRID_W] = blkval


def _nat_bias(na_bias, rows):
    depth = na_bias.shape[0]
    tq, nk = NAT_QROWS * GRID_W, NAT_KROWS * GRID_W
    return pl.pallas_call(
        functools.partial(_nat_bias_body, rows=rows),
        grid=(depth, HEADS_C),
        in_specs=[pl.BlockSpec(memory_space=pltpu.SMEM)],
        out_specs=pl.BlockSpec((None, None, 3, tq, nk), lambda l, h: (l, h, 0, 0, 0)),
        out_shape=jax.ShapeDtypeStruct((depth, HEADS_C, 3, tq, nk), F32),
        compiler_params=_params("parallel", "parallel"),
        name="nat_bias",
    )(na_bias.reshape(-1))


def _neighbourhood_attention(q, ks, vs, kc, vc, bias, layer):
    b, n, _ = q.shape
    nblk = n // (NAT_QROWS * GRID_W)
    groups = min(nblk, ATTN_GROUPS)
    tq = groups * NAT_QROWS * GRID_W
    mc = kc.shape[2]
    d = HEAD_DIM

    return pl.pallas_call(
        _nat_body,
        grid=(b, HEADS_C, nblk // groups),
        in_specs=[
            pl.BlockSpec((None, tq, d), lambda bi, h, r: (bi, r, h)),
            pl.BlockSpec((None, n, d), lambda bi, h, r: (bi, 0, h)),
            pl.BlockSpec((None, n, d), lambda bi, h, r: (bi, 0, h)),
            pl.BlockSpec((None, None, mc, d), lambda bi, h, r: (bi, layer, 0, h)),
            pl.BlockSpec((None, None, mc, d), lambda bi, h, r: (bi, layer, 0, h)),
            pl.BlockSpec((None, None, 3, NAT_QROWS * GRID_W, NAT_KROWS * GRID_W),
                         lambda bi, h, r: (layer, h, 0, 0, 0)),
        ],
        out_specs=pl.BlockSpec((None, tq, d), lambda bi, h, r: (bi, r, h)),
        out_shape=jax.ShapeDtypeStruct((b, n, HEADS_C * d), BF16),
        compiler_params=_params("parallel", "parallel", "arbitrary"),
        name="neighbourhood_attention",
    )(q, ks, vs, kc, vc, bias)


def _cache_kv_body(ckv_ref, kr_ref, wkk_ref, wkv_ref, kb_o, vb_o):
    ckvb = ckv_ref[...].astype(BF16)
    krb = kr_ref[...].astype(BF16)
    kn = jnp.dot(ckvb, wkk_ref[...], preferred_element_type=F32)
    zeros = jnp.zeros((krb.shape[0], QB_PAD - 128 - ROPE_DIM_B), BF16)
    for hd in range(HEADS_B):
        lo = hd * QB_PAD
        kb_o[:, lo:lo + 128] = kn[:, hd * 128:(hd + 1) * 128].astype(BF16)
        kb_o[:, lo + 128:lo + 128 + ROPE_DIM_B] = krb
        kb_o[:, lo + 128 + ROPE_DIM_B:lo + QB_PAD] = zeros
    vb_o[...] = jnp.dot(ckvb, wkv_ref[...], preferred_element_type=F32).astype(BF16)


def _cache_mla_kv(cache_ckv, cache_krope, w_kv_k, w_kv_v):
    b, depth, m, _ = cache_ckv.shape
    return pl.pallas_call(
        _cache_kv_body,
        grid=(b, depth),
        in_specs=[
            pl.BlockSpec((None, None, m, KV_RANK_B), lambda bi, l: (bi, l, 0, 0)),
            pl.BlockSpec((None, None, m, ROPE_DIM_B), lambda bi, l: (bi, l, 0, 0)),
            pl.BlockSpec((None, KV_RANK_B, HEADS_B * 128), lambda bi, l: (l, 0, 0)),
            pl.BlockSpec((None, KV_RANK_B, HEADS_B * 128), lambda bi, l: (l, 0, 0)),
        ],
        out_specs=[
            pl.BlockSpec((None, None, m, HEADS_B * QB_PAD), lambda bi, l: (bi, l, 0, 0)),
            pl.BlockSpec((None, None, m, WIDTH_B), lambda bi, l: (bi, l, 0, 0)),
        ],
        out_shape=[
            jax.ShapeDtypeStruct((b, depth, m, HEADS_B * QB_PAD), BF16),
            jax.ShapeDtypeStruct((b, depth, m, WIDTH_B), BF16),
        ],
        compiler_params=_params("parallel", "parallel"),
        name="cache_mla_kv",
    )(cache_ckv, cache_krope, w_kv_k, w_kv_v)


def _out_body(oa_ref, ob_ref, oc_ref, wa_ref, wb_ref, wc_ref, x_ref, gate_ref, sh_ref, sc_ref, g_ref, wr_ref,
              x1_o, h2_o, lg_o):
    o = jnp.dot(oa_ref[...], wa_ref[...], preferred_element_type=F32)
    o = o + jnp.dot(ob_ref[...], wb_ref[...], preferred_element_type=F32)
    o = o + jnp.dot(oc_ref[...], wc_ref[...], preferred_element_type=F32)
    x1 = x_ref[...] + gate_ref[...] * o
    x1_o[...] = x1
    h = _rms(x1, g_ref[...]) * (1.0 + sc_ref[...]) + sh_ref[...]
    hb = h.astype(BF16)
    h2_o[...] = hb
    lg = jnp.dot(hb, wr_ref[...], preferred_element_type=F32)
    for c in range(lg.shape[0] // 128):
        lg_o[:, c * 128:(c + 1) * 128] = lg[c * 128:(c + 1) * 128, :].T[:N_EXPERTS, :]


def _output_side(o_a, o_b, o_c, x, mod, layer, wts):
    b, n, d = x.shape
    tm = TOKEN_TILE
    mb = mod.shape[0]
    bsel = (lambda i: i) if mb > 1 else (lambda i: 0)

    def modspec(k):
        return pl.BlockSpec((None, None, 1, d), lambda bi, i: (bsel(bi), k, 0, 0))

    def tok(w):
        return pl.BlockSpec((None, tm, w), lambda bi, i: (bi, i, 0))

    return pl.pallas_call(
        _out_body,
        grid=(b, n // tm),
        in_specs=[
            tok(WIDTH_A), tok(WIDTH_B), tok(WIDTH_C),
            _resident((None, WIDTH_A, d), lambda bi, i: (layer, 0, 0)),
            _resident((None, WIDTH_B, d), lambda bi, i: (layer, 0, 0)),
            _resident((None, WIDTH_C, d), lambda bi, i: (layer, 0, 0)),
            tok(d), modspec(2), modspec(3), modspec(4),
            pl.BlockSpec((None, 1, d), lambda bi, i: (layer, 0, 0)),
            pl.BlockSpec((None, d, 128), lambda bi, i: (layer, 0, 0)),
        ],
        out_specs=[tok(d), tok(d), pl.BlockSpec((None, N_EXPERTS, tm), lambda bi, i: (bi, 0, i))],
        out_shape=[
            jax.ShapeDtypeStruct((b, n, d), F32),
            jax.ShapeDtypeStruct((b, n, d), BF16),
            jax.ShapeDtypeStruct((b, N_EXPERTS, n), F32),
        ],
        compiler_params=_params("parallel", "parallel"),
        name="output_side",
    )(o_a, o_b, o_c, wts["w_out_a"], wts["w_out_b"], wts["w_out_c"], x, mod, mod, mod, wts["norm2_g"],
      wts["w_router"])


def _prefix_exclusive(mask):
    e, n = mask.shape
    ones = jnp.where(mask, 1.0, 0.0)
    rr = lax.broadcasted_iota(jnp.int32, (128, 128), 0)
    cc = lax.broadcasted_iota(jnp.int32, (128, 128), 1)
    tri = jnp.where(rr <= cc, 1.0, 0.0).astype(BF16)
    carry = jnp.zeros((e, 1), F32)
    outs = []
    for c in range(n // 128):
        blk = ones[:, c * 128:(c + 1) * 128]
        inc = jnp.dot(blk.astype(BF16), tri, preferred_element_type=F32)
        outs.append(inc - blk + carry)
        carry = carry + inc[:, 127:128]
    return jnp.concatenate(outs, axis=1)


def _router_body(lg_ref, rank_o, rank_t_o, aff_t_o, cum_o, *, cap, chunk):
    lg = lg_ref[...]
    e, n = lg.shape
    ex = jnp.exp(lg - jnp.max(lg, axis=0, keepdims=True))
    aff = ex / jnp.sum(ex, axis=0, keepdims=True)
    key = pltpu.bitcast(aff, jnp.int32)

    def step(i, t):
        cand = t | lax.shift_left(jnp.int32(1), 30 - i)
        cnt = jnp.sum(jnp.where(key >= cand, 1.0, 0.0), axis=1, keepdims=True)
        return jnp.where(cnt >= cap, cand, t)

    thr = lax.fori_loop(0, 31, step, jnp.zeros((e, 1), jnp.int32))
    above = key > thr
    tied = key == thr
    need = cap - jnp.sum(jnp.where(above, 1.0, 0.0), axis=1, keepdims=True)
    chosen = above | (tied & (_prefix_exclusive(tied) < need))
    before = _prefix_exclusive(chosen)
    rank = jnp.where(chosen, before, -1.0)
    rank_o[...] = rank.astype(jnp.int32)

    lane = lax.broadcasted_iota(jnp.int32, (e, 128), 1)
    cum = jnp.full((e, 128), float(cap), F32)
    for k in range(n // chunk):
        cum = jnp.where(lane == k, before[:, k * chunk:k * chunk + 1], cum)
    cum_o[...] = cum.astype(jnp.int32)

    fill = jnp.full((128 - e, n), -1.0, F32)
    rank_p = jnp.concatenate([rank, fill], axis=0)
    aff_p = jnp.concatenate([aff, fill], axis=0)
    for c in range(n // 128):
        rank_t_o[c * 128:(c + 1) * 128, :] = rank_p[:, c * 128:(c + 1) * 128].T
        aff_t_o[c * 128:(c + 1) * 128, :] = aff_p[:, c * 128:(c + 1) * 128].T


def _route(logits_t):
    b, e, n = logits_t.shape
    cap = CAPACITY_FACTOR * n // e
    return pl.pallas_call(
        functools.partial(_router_body, cap=cap, chunk=SCATTER_TOKENS),
        grid=(b,),
        in_specs=[pl.BlockSpec((None, e, n), lambda bi: (bi, 0, 0))],
        out_specs=[
            pl.BlockSpec((None, e, n), lambda bi: (bi, 0, 0)),
            pl.BlockSpec((None, n, 128), lambda bi: (bi, 0, 0)),
            pl.BlockSpec((None, n, 128), lambda bi: (bi, 0, 0)),
            pl.BlockSpec((None, e, 128), lambda bi: (bi, 0, 0)),
        ],
        out_shape=[
            jax.ShapeDtypeStruct((b, e, n), jnp.int32),
            jax.ShapeDtypeStruct((b, n, 128), F32),
            jax.ShapeDtypeStruct((b, n, 128), F32),
            jax.ShapeDtypeStruct((b, e, 128), jnp.int32),
        ],
        compiler_params=_params("parallel"),
        name="route",
    )(logits_t)


def _gather_body(cum_ref, h_ref, rank_ref, xe_o, *, chunk, rows):
    n, d = h_ref.shape
    cap = xe_o.shape[0]
    nch = n // chunk
    per_chunk = chunk // SCATTER_TOKENS
    base = (pl.program_id(0) * pl.num_programs(1) + pl.program_id(1)) * (n // SCATTER_TOKENS + 1)
    rank = rank_ref[...]
    xe_o[...] = jnp.zeros(xe_o.shape, xe_o.dtype)
    for c in range(nch):
        lo = cum_ref[base + c * per_chunk]
        hi = cum_ref[base + (c + 1) * per_chunk]
        rank_c = rank[:, c * chunk:(c + 1) * chunk]
        for jb in range(cap // rows):

            @pl.when((lo < (jb + 1) * rows) & (hi > jb * rows))
            def _():
                slot = lax.broadcasted_iota(jnp.int32, (rows, chunk), 0) + jb * rows
                onehot = jnp.where(slot == rank_c, 1.0, 0.0).astype(BF16)
                picked = jnp.dot(onehot, h_ref[c * chunk:(c + 1) * chunk, :], preferred_element_type=F32)
                xe_o[jb * rows:(jb + 1) * rows, :] += picked.astype(xe_o.dtype)


def _gather(h2, rank, cum_flat):
    b, n, d = h2.shape
    e = rank.shape[1]
    cap = CAPACITY_FACTOR * n // e
    chunk = min(n, GATHER_CHUNK)
    return pl.pallas_call(
        functools.partial(_gather_body, chunk=chunk, rows=min(cap, GATHER_ROWS)),
        grid_spec=pltpu.PrefetchScalarGridSpec(
            num_scalar_prefetch=1,
            grid=(b, e),
            in_specs=[
                pl.BlockSpec((None, n, d), lambda bi, ei, cum_ref: (bi, 0, 0)),
                pl.BlockSpec((None, None, 1, n), lambda bi, ei, cum_ref: (bi, ei, 0, 0)),
            ],
            out_specs=pl.BlockSpec((None, None, cap, d), lambda bi, ei, cum_ref: (ei, bi, 0, 0)),
        ),
        out_shape=jax.ShapeDtypeStruct((e, b, cap, d), BF16),
        compiler_params=_params("parallel", "parallel"),
        name="gather",
    )(cum_flat, h2, rank)


def _ffn_body(xe_ref, wg_ref, wu_ref, wd_ref, ye_o):
    bb, cap, d = xe_ref.shape
    xe = xe_ref[...].reshape(bb * cap, d)
    g = jnp.dot(xe, wg_ref[...], preferred_element_type=F32)
    u = jnp.dot(xe, wu_ref[...], preferred_element_type=F32)
    hid = (g / (1.0 + jnp.exp(-g)) * u).astype(BF16)
    ye = jnp.dot(hid, wd_ref[...], preferred_element_type=F32)
    ye_o[...] = ye.astype(ye_o.dtype).reshape(bb, cap, d)


def _expert_ffn(xe, layer, wts):
    e, b, cap, d = xe.shape
    ff = wts["w_gate"].shape[-1]
    bb = max(1, min(b, 512 // cap))
    return pl.pallas_call(
        _ffn_body,
        grid=(e, b // bb),
        in_specs=[
            pl.BlockSpec((None, bb, cap, d), lambda ei, bi: (ei, bi, 0, 0)),
            pl.BlockSpec((None, None, d, ff), lambda ei, bi: (layer, ei, 0, 0)),
            pl.BlockSpec((None, None, d, ff), lambda ei, bi: (layer, ei, 0, 0)),
            pl.BlockSpec((None, None, ff, d), lambda ei, bi: (layer, ei, 0, 0)),
        ],
        out_specs=pl.BlockSpec((None, bb, cap, d), lambda ei, bi: (ei, bi, 0, 0)),
        out_shape=jax.ShapeDtypeStruct((e, b, cap, d), BF16),
        compiler_params=_params("parallel", "arbitrary"),
        name="expert_ffn",
    )(xe, wts["w_gate"], wts["w_up"], wts["w_down"])


def _scatter_body(cum_ref, ye_ref, x_ref, rank_ref, aff_ref, gate_ref, o_ref):
    e, cap, dh = ye_ref.shape
    tn = x_ref.shape[0]
    sub_tiles = tn // SCATTER_TOKENS
    win = min(cap, 2 * SCATTER_TOKENS)
    per_expert = pl.num_programs(2) * sub_tiles + 1
    col = lax.broadcasted_iota(jnp.int32, (SCATTER_TOKENS, win), 1).astype(F32)
    for sub in range(sub_tiles):
        rows = slice(sub * SCATTER_TOKENS, (sub + 1) * SCATTER_TOKENS)
        acc = jnp.zeros((SCATTER_TOKENS, dh), F32)
        for ei in range(e):
            rank = rank_ref[rows, ei:ei + 1]
            if cap > win:
                first = cum_ref[(pl.program_id(0) * e + ei) * per_expert + pl.program_id(2) * sub_tiles + sub]
                start = jnp.clip((first // SCATTER_TOKENS) * SCATTER_TOKENS, 0, cap - win)
                start = pl.multiple_of(start, SCATTER_TOKENS)
                onehot = jnp.where(rank - start.astype(F32) == col, 1.0, 0.0).astype(BF16)
                contrib = jnp.dot(onehot, ye_ref[ei, pl.ds(start, win), :], preferred_element_type=F32)
            else:
                onehot = jnp.where(rank == col, 1.0, 0.0).astype(BF16)
                contrib = jnp.dot(onehot, ye_ref[ei], preferred_element_type=F32)
            acc = acc + contrib * aff_ref[rows, ei:ei + 1]
        o_ref[rows, :] = x_ref[rows, :] + gate_ref[...] * acc


def _scatter(ye, x1, rank_t, aff_t, mod, cum_flat):
    e, b, cap, d = ye.shape
    n = x1.shape[1]
    tn = min(n, 512)
    dh = d // 2
    mb = mod.shape[0]
    bsel = (lambda i: i) if mb > 1 else (lambda i: 0)
    return pl.pallas_call(
        _scatter_body,
        grid_spec=pltpu.PrefetchScalarGridSpec(
            num_scalar_prefetch=1,
            grid=(b, 2, n // tn),
            in_specs=[
                pl.BlockSpec((e, None, cap, dh), lambda bi, j, i, cum_ref: (0, bi, 0, j)),
                pl.BlockSpec((None, tn, dh), lambda bi, j, i, cum_ref: (bi, i, j)),
                pl.BlockSpec((None, tn, 128), lambda bi, j, i, cum_ref: (bi, i, 0)),
                pl.BlockSpec((None, tn, 128), lambda bi, j, i, cum_ref: (bi, i, 0)),
                pl.BlockSpec((None, None, 1, dh), lambda bi, j, i, cum_ref: (bsel(bi), 5, 0, j)),
            ],
            out_specs=pl.BlockSpec((None, tn, dh), lambda bi, j, i, cum_ref: (bi, i, j)),
        ),
        out_shape=jax.ShapeDtypeStruct(x1.shape, F32),
        compiler_params=_params("parallel", "parallel", "parallel"),
        name="scatter",
    )(cum_flat, ye, x1, rank_t, aff_t, mod)


def _final_norm_body(x_ref, g_ref, o_ref):
    o_ref[...] = _rms(x_ref[...], g_ref[...])


def _final_norm(x, g):
    b, n, d = x.shape
    tm = TOKEN_TILE
    return pl.pallas_call(
        _final_norm_body,
        grid=(b, n // tm),
        in_specs=[pl.BlockSpec((None, tm, d), lambda bi, i: (bi, i, 0)), pl.BlockSpec((1, d), lambda bi, i: (0, 0))],
        out_specs=pl.BlockSpec((None, tm, d), lambda bi, i: (bi, i, 0)),
        out_shape=jax.ShapeDtypeStruct(x.shape, F32),
        compiler_params=_params("parallel", "parallel"),
        name="final_norm",
    )(x, g.reshape(1, d))


def _rope_tables(n_tokens, rot_dim):
    t = jnp.arange(n_tokens, dtype=jnp.int32)
    row = (t // GRID_W).astype(F32)
    col = (t % GRID_W).astype(F32)
    axis_dim = rot_dim // 2
    freqs = ROPE_THETA ** (-jnp.arange(0, axis_dim, 2, dtype=F32) / axis_dim)
    ang = jnp.concatenate([row[:, None] * freqs[None, :], col[:, None] * freqs[None, :]], axis=-1)
    cos, sin = jnp.cos(ang), jnp.sin(ang)
    cos2 = jnp.repeat(cos, 2, axis=-1)
    sin2 = jnp.stack([-sin, sin], axis=-1).reshape(n_tokens, rot_dim)
    pad = 128 - rot_dim
    if pad:
        cos2 = jnp.concatenate([cos2, jnp.ones((n_tokens, pad), F32)], axis=-1)
        sin2 = jnp.concatenate([sin2, jnp.zeros((n_tokens, pad), F32)], axis=-1)
    return cos2, sin2


def _prepare_weights(norm1_g, norm2_g, w_in, qa_norm_g, ka_norm_g, q_norm_b, kv_norm_b, w_q_up, w_kv_up, w_out,
                     w_router, w_gate, w_up, w_down):
    depth, d, _ = w_in.shape
    kr_lo = SEG_CKV[1]
    w_in_p = jnp.concatenate(
        [w_in[:, :, :kr_lo], w_in[:, :, kr_lo + ROPE_DIM_B:], w_in[:, :, kr_lo:kr_lo + ROPE_DIM_B],
         jnp.zeros((depth, d, 128 - ROPE_DIM_B), w_in.dtype)], axis=-1).astype(BF16)
    wq = w_q_up.reshape(depth, Q_RANK_B, HEADS_B, NOPE_DIM_B + ROPE_DIM_B)
    wq = jnp.pad(wq, ((0, 0), (0, 0), (0, 0), (0, QB_PAD - NOPE_DIM_B - ROPE_DIM_B)))
    wkv = w_kv_up.reshape(depth, KV_RANK_B, HEADS_B, NOPE_DIM_B + V_DIM_B)
    return {
        "norm1_g": norm1_g.reshape(depth, 1, d),
        "norm2_g": norm2_g.reshape(depth, 1, d),
        "w_in": w_in_p,
        "qa_g": qa_norm_g.reshape(depth, 1, HEAD_DIM),
        "ka_g": ka_norm_g.reshape(depth, 1, HEAD_DIM),
        "qn_g": q_norm_b.reshape(depth, 1, Q_RANK_B),
        "kvn_g": kv_norm_b.reshape(depth, 1, KV_RANK_B),
        "w_q_up": wq.reshape(depth, Q_RANK_B, HEADS_B * QB_PAD).astype(BF16),
        "w_kv_k": wkv[..., :NOPE_DIM_B].reshape(depth, KV_RANK_B, HEADS_B * NOPE_DIM_B).astype(BF16),
        "w_kv_v": wkv[..., NOPE_DIM_B:].reshape(depth, KV_RANK_B, HEADS_B * V_DIM_B).astype(BF16),
        "w_out_a": w_out[:, :WIDTH_A].astype(BF16),
        "w_out_b": w_out[:, WIDTH_A:WIDTH_A + WIDTH_B].astype(BF16),
        "w_out_c": w_out[:, WIDTH_A + WIDTH_B:].astype(BF16),
        "w_router": jnp.pad(w_router, ((0, 0), (0, 0), (0, 128 - N_EXPERTS))).astype(BF16),
        "w_gate": w_gate.astype(BF16),
        "w_up": w_up.astype(BF16),
        "w_down": w_down.astype(BF16),
    }


def _moe(h2, logits_t, x1, mod, layer, wts):
    rank, rank_t, aff_t, cum = _route(logits_t)
    n = rank.shape[2]
    cum_flat = cum[:, :, :n // SCATTER_TOKENS + 1].reshape(-1)
    xe = _gather(h2, rank.reshape(rank.shape[0], rank.shape[1], 1, n), cum_flat)
    ye = _expert_ffn(xe, layer, wts)
    return _scatter(ye, x1, rank_t, aff_t, mod, cum_flat)


def kernel(x_prompt, x_sample, cache_a_k, cache_a_v, cache_b_ckv, cache_b_krope, cache_c_k, cache_c_v, c, c_ctx, w_ada, b_ada, norm1_g, norm2_g, w_in, qa_norm_g, ka_norm_g, q_norm_b, kv_norm_b, w_q_up, w_kv_up, na_bias, w_out, w_router, w_gate, w_up, w_down, final_norm_g):
    depth, d, _ = w_in.shape
    bp, seq, _ = x_prompt.shape
    bs, n_lat, _ = x_sample.shape
    past = cache_a_k.shape[2]
    rows = n_lat // GRID_W

    wts = _prepare_weights(norm1_g, norm2_g, w_in, qa_norm_g, ka_norm_g, q_norm_b, kv_norm_b, w_q_up, w_kv_up,
                           w_out, w_router, w_gate, w_up, w_down)
    rope_tabs = _rope_tables(n_lat, HEAD_DIM) + _rope_tables(n_lat, ROPE_DIM_B)

    cond8 = jnp.concatenate([c_ctx[None], c, jnp.zeros((8 - 1 - bs, d), F32)], axis=0)
    mods = _modulation(cond8, w_ada, b_ada)
    mods = mods.reshape(depth, 8, 6, 1, d)

    ca_k = cache_a_k.reshape(bs, depth, past, KV_A).astype(BF16)
    ca_v = cache_a_v.reshape(bs, depth, past, KV_A).astype(BF16)
    cc_k = cache_c_k.reshape(bs, depth, past, WIDTH_C).astype(BF16)
    cc_v = cache_c_v.reshape(bs, depth, past, WIDTH_C).astype(BF16)
    cb_k, cb_v = _cache_mla_kv(cache_b_ckv, cache_b_krope, wts["w_kv_k"], wts["w_kv_v"])
    nat_bias = _nat_bias(na_bias, rows)

    xp, xs = x_prompt, x_sample
    states = []
    for l in range(depth):
        mod_c = mods[l, 0:1]
        mod_l = mods[l, 1:1 + bs]

        qa, ka, va, qb, kb, vb, qc, kc, vc, st = _input_side(xp, mod_c, l, wts, None, True)
        o_a = _attention(qa, ka, va, None, None, l, HEADS_A, HEADS_A // KV_HEADS_A, HEAD_DIM, HEAD_DIM)
        o_b = _attention(qb, kb, vb, None, None, l, HEADS_B, 1, QB_PAD, V_DIM_B)
        o_c = _attention(qc, kc, vc, None, None, l, HEADS_C, 1, HEAD_DIM, HEAD_DIM)
        x1, h2, lg = _output_side(o_a, o_b, o_c, xp, mod_c, l, wts)
        xp = _moe(h2, lg, x1, mod_c, l, wts)
        states.append(st)

        qa, ka, va, qb, kb, vb, qc, kc, vc = _input_side(xs, mod_l, l, wts, rope_tabs, False)
        o_a = _attention(qa, ka, va, ca_k, ca_v, l, HEADS_A, HEADS_A // KV_HEADS_A, HEAD_DIM, HEAD_DIM)
        o_b = _attention(qb, kb, vb, cb_k, cb_v, l, HEADS_B, 1, QB_PAD, V_DIM_B)
        o_c = _neighbourhood_attention(qc, kc, vc, cc_k, cc_v, nat_bias, l)
        x1, h2, lg = _output_side(o_a, o_b, o_c, xs, mod_l, l, wts)
        xs = _moe(h2, lg, x1, mod_l, l, wts)

    y_prompt = _final_norm(xp, final_norm_g)
    y_sample = _final_norm(xs, final_norm_g)
    st = jnp.stack(states, axis=1)
    new_a_k = st[..., ST_KA[0]:ST_KA[1]].reshape(bp, depth, seq, KV_HEADS_A, HEAD_DIM)
    new_a_v = st[..., ST_VA[0]:ST_VA[1]].reshape(bp, depth, seq, KV_HEADS_A, HEAD_DIM)
    new_b_ckv = st[..., ST_CKV[0]:ST_CKV[1]]
    new_b_krope = st[..., ST_KR[0]:ST_KR[0] + ROPE_DIM_B]
    new_c_k = st[..., ST_KC[0]:ST_KC[1]].reshape(bp, depth, seq, HEADS_C, HEAD_DIM)
    new_c_v = st[..., ST_VC[0]:ST_VC[1]].reshape(bp, depth, seq, HEADS_C, HEAD_DIM)
    return (y_prompt, y_sample, new_a_k, new_a_v, new_b_ckv, new_b_krope, new_c_k, new_c_v)
```

```python
import functools

import jax
import jax.numpy as jnp
from jax import lax
from jax.experimental import pallas as pl
from jax.experimental.pallas import tpu as pltpu

F32 = jnp.float32
BF16 = jnp.bfloat16

GRID_W = 64
HEAD_DIM = 128
HEADS_A = 6
KV_HEADS_A = 2
HEADS_B = 5
Q_RANK_B = 512
KV_RANK_B = 256
NOPE_DIM_B = 128
ROPE_DIM_B = 64
V_DIM_B = 128
HEADS_C = 5
WIN_ROWS_MAX = 8
WIN_COLS = 16
N_EXPERTS = 16
CAPACITY_FACTOR = 2
ROPE_THETA = 10000.0
EPS = 1e-6

QB_PAD = 256
WIDTH_A = HEADS_A * HEAD_DIM
WIDTH_B = HEADS_B * V_DIM_B
WIDTH_C = HEADS_C * HEAD_DIM
KV_A = KV_HEADS_A * HEAD_DIM

SEG_QA = (0, WIDTH_A)
SEG_KA = (SEG_QA[1], SEG_QA[1] + KV_A)
SEG_VA = (SEG_KA[1], SEG_KA[1] + KV_A)
SEG_CQ = (SEG_VA[1], SEG_VA[1] + Q_RANK_B)
SEG_CKV = (SEG_CQ[1], SEG_CQ[1] + KV_RANK_B)
SEG_QC = (SEG_CKV[1], SEG_CKV[1] + WIDTH_C)
SEG_KC = (SEG_QC[1], SEG_QC[1] + WIDTH_C)
SEG_VC = (SEG_KC[1], SEG_KC[1] + WIDTH_C)
SEG_KR = (SEG_VC[1], SEG_VC[1] + 128)
IN_COLS_P = SEG_KR[1]

ST_KA = (0, KV_A)
ST_VA = (ST_KA[1], ST_KA[1] + KV_A)
ST_CKV = (ST_VA[1], ST_VA[1] + KV_RANK_B)
ST_KC = (ST_CKV[1], ST_CKV[1] + WIDTH_C)
ST_VC = (ST_KC[1], ST_KC[1] + WIDTH_C)
ST_KR = (ST_VC[1], ST_VC[1] + 128)
ST_COLS = ST_KR[1]

NAT_QROWS = 4
NAT_KROWS = 12
MASK_VALUE = -1e30
LOG2E = 1.4426950408889634

TOKEN_TILE = 256
PROJ_TILE = 512
ATTN_GROUPS = 4
GATHER_CHUNK = 512
GATHER_ROWS = 128
SCATTER_TOKENS = 128
ROUTE_TOKENS = 4096
FF_SLICE = 256
VMEM_LIMIT = 56 * 1024 * 1024
NT_DIMS = (((1,), (1,)), ((), ()))


def _params(*sem):
    return pltpu.CompilerParams(dimension_semantics=sem, vmem_limit_bytes=VMEM_LIMIT)


def _resident(block_shape, index_map):
    return pl.BlockSpec(block_shape, index_map, pipeline_mode=pl.Buffered(1))


def _rms(x, g):
    ms = jnp.mean(x * x, axis=-1, keepdims=True)
    return x * lax.rsqrt(ms + EPS) * g


def _swap_pairs(x):
    lane = lax.broadcasted_iota(jnp.int32, x.shape, x.ndim - 1)
    nxt = pltpu.roll(x, x.shape[-1] - 1, x.ndim - 1)
    prv = pltpu.roll(x, 1, x.ndim - 1)
    return jnp.where((lane & 1) == 0, nxt, prv)


def _rope(x, cos, sin_signed):
    return x * cos + _swap_pairs(x) * sin_signed


def _mod_body(c_ref, w_ref, b_ref, o_ref):
    c = c_ref[...]
    s = (c / (1.0 + jnp.exp(-c))).astype(BF16)
    o_ref[...] = jnp.dot(s, w_ref[...].astype(BF16), preferred_element_type=F32) + b_ref[...]


def _modulation(cond8, w_ada, b_ada):
    depth, d, cols = w_ada.shape
    tn = 1024
    return pl.pallas_call(
        _mod_body,
        grid=(depth, cols // tn),
        in_specs=[
            pl.BlockSpec((8, d), lambda l, j: (0, 0)),
            pl.BlockSpec((None, d, tn), lambda l, j: (l, 0, j)),
            pl.BlockSpec((None, 1, tn), lambda l, j: (l, 0, j)),
        ],
        out_specs=pl.BlockSpec((None, 8, tn), lambda l, j: (l, 0, j)),
        out_shape=jax.ShapeDtypeStruct((depth, 8, cols), F32),
        compiler_params=_params("parallel", "parallel"),
        name="modulation",
    )(cond8, w_ada, b_ada.reshape(depth, 1, cols))


def _in_body(*refs, rope, state):
    it = iter(refs)
    x_ref, sh_ref, sc_ref, g_ref, win_ref = (next(it) for _ in range(5))
    qag_ref, kag_ref, qng_ref, kvng_ref = (next(it) for _ in range(4))
    wq_ref, wkk_ref, wkv_ref = (next(it) for _ in range(3))
    if rope:
        ca_ref, sa_ref, cb_ref, sb_ref = (next(it) for _ in range(4))
    qa_o, ka_o, va_o, qb_o, kb_o, vb_o, qc_o, kc_o, vc_o = (next(it) for _ in range(9))
    st_o = next(it) if state else None

    h = _rms(x_ref[...], g_ref[...]) * (1.0 + sc_ref[...]) + sh_ref[...]
    hb = h.astype(BF16)

    def seg(bounds):
        return jnp.dot(hb, win_ref[:, bounds[0]:bounds[1]], preferred_element_type=F32)

    def rope_a(y):
        return _rope(y, ca_ref[...], sa_ref[...]) if rope else y

    def rope_b(y):
        return _rope(y, cb_ref[...], sb_ref[...]) if rope else y

    qa = seg(SEG_QA)
    for hd in range(HEADS_A):
        lo = hd * HEAD_DIM
        y = rope_a(_rms(qa[:, lo:lo + HEAD_DIM], qag_ref[...]))
        qa_o[:, lo:lo + HEAD_DIM] = (y * (HEAD_DIM ** -0.5 * LOG2E)).astype(BF16)
    ka = seg(SEG_KA)
    for hd in range(KV_HEADS_A):
        lo = hd * HEAD_DIM
        y = rope_a(_rms(ka[:, lo:lo + HEAD_DIM], kag_ref[...]))
        ka_o[:, lo:lo + HEAD_DIM] = y.astype(BF16)
        if state:
            st_o[:, ST_KA[0] + lo:ST_KA[0] + lo + HEAD_DIM] = y
    va = seg(SEG_VA)
    va_o[...] = va.astype(BF16)

    cq = _rms(seg(SEG_CQ), qng_ref[...]).astype(BF16)
    qb = jnp.dot(cq, wq_ref[...], preferred_element_type=F32)
    qscale = (NOPE_DIM_B + ROPE_DIM_B) ** -0.5 * LOG2E
    for hd in range(HEADS_B):
        lo = hd * QB_PAD
        qb_o[:, lo:lo + 128] = (qb[:, lo:lo + 128] * qscale).astype(BF16)
        qb_o[:, lo + 128:lo + 256] = (rope_b(qb[:, lo + 128:lo + 256]) * qscale).astype(BF16)
    ckv = _rms(seg(SEG_CKV), kvng_ref[...])
    ckvb = ckv.astype(BF16)
    kr = rope_b(seg(SEG_KR))
    krb = kr.astype(BF16)
    kn = jnp.dot(ckvb, wkk_ref[...], preferred_element_type=F32)
    for hd in range(HEADS_B):
        kb_o[:, hd * QB_PAD:hd * QB_PAD + 128] = kn[:, hd * 128:(hd + 1) * 128].astype(BF16)
        kb_o[:, hd * QB_PAD + 128:(hd + 1) * QB_PAD] = krb
    vb_o[...] = jnp.dot(ckvb, wkv_ref[...], preferred_element_type=F32).astype(BF16)

    qc_o[...] = (seg(SEG_QC) * (HEAD_DIM ** -0.5 * LOG2E)).astype(BF16)
    kc = seg(SEG_KC)
    kc_o[...] = kc.astype(BF16)
    vc = seg(SEG_VC)
    vc_o[...] = vc.astype(BF16)

    if state:
        st_o[:, ST_VA[0]:ST_VA[1]] = va
        st_o[:, ST_CKV[0]:ST_CKV[1]] = ckv
        st_o[:, ST_KC[0]:ST_KC[1]] = kc
        st_o[:, ST_VC[0]:ST_VC[1]] = vc
        st_o[:, ST_KR[0]:ST_KR[1]] = kr


def _input_side(x, mod, layer, wts, rope_tabs, state):
    b, n, d = x.shape
    tm = min(n, PROJ_TILE)
    mb = mod.shape[0]
    bsel = (lambda i: i) if mb > 1 else (lambda i: 0)
    rope = rope_tabs is not None

    def modspec(k):
        return pl.BlockSpec((None, None, 1, d), lambda bi, i: (bsel(bi), k, 0, 0))

    def vec(w):
        return pl.BlockSpec((None, 1, w), lambda bi, i: (layer, 0, 0))

    in_specs = [
        pl.BlockSpec((None, tm, d), lambda bi, i: (bi, i, 0)),
        modspec(0), modspec(1), vec(d),
        _resident((None, d, IN_COLS_P), lambda bi, i: (layer, 0, 0)),
        vec(HEAD_DIM), vec(HEAD_DIM), vec(Q_RANK_B), vec(KV_RANK_B),
        _resident((None, Q_RANK_B, HEADS_B * QB_PAD), lambda bi, i: (layer, 0, 0)),
        _resident((None, KV_RANK_B, HEADS_B * 128), lambda bi, i: (layer, 0, 0)),
        _resident((None, KV_RANK_B, HEADS_B * 128), lambda bi, i: (layer, 0, 0)),
    ]
    args = [x, mod, mod, wts["norm1_g"], wts["w_in"], wts["qa_g"], wts["ka_g"], wts["qn_g"], wts["kvn_g"],
            wts["w_q_up"], wts["w_kv_k"], wts["w_kv_v"]]
    if rope:
        in_specs += [pl.BlockSpec((tm, 128), lambda bi, i: (i, 0))] * 4
        args += list(rope_tabs)

    widths = [WIDTH_A, KV_A, KV_A, HEADS_B * QB_PAD, HEADS_B * QB_PAD, WIDTH_B, WIDTH_C, WIDTH_C, WIDTH_C]
    out_shape = [jax.ShapeDtypeStruct((b, n, w), BF16) for w in widths]
    out_specs = [pl.BlockSpec((None, tm, w), lambda bi, i: (bi, i, 0)) for w in widths]
    if state:
        out_shape.append(jax.ShapeDtypeStruct((b, n, ST_COLS), F32))
        out_specs.append(pl.BlockSpec((None, tm, ST_COLS), lambda bi, i: (bi, i, 0)))

    return pl.pallas_call(
        functools.partial(_in_body, rope=rope, state=state),
        grid=(b, n // tm),
        in_specs=in_specs,
        out_specs=out_specs,
        out_shape=out_shape,
        compiler_params=_params("parallel", "parallel"),
        name="input_side",
    )(*args)


def _attn_body(*refs, cache):
    if cache:
        q_ref, ks_ref, vs_ref, kc_ref, vc_ref, o_ref = refs
    else:
        q_ref, ks_ref, vs_ref, o_ref = refs
    rows = min(q_ref.shape[0], TOKEN_TILE)
    groups = q_ref.shape[0] // rows

    def scores(c):
        q = q_ref[c * rows:(c + 1) * rows, :]
        s_self = lax.dot_general(q, ks_ref[...], NT_DIMS, preferred_element_type=F32)
        m = jnp.max(s_self, axis=-1, keepdims=True)
        s_ctx = None
        if cache:
            s_ctx = lax.dot_general(q, kc_ref[...], NT_DIMS, preferred_element_type=F32)
            m = jnp.maximum(m, jnp.max(s_ctx, axis=-1, keepdims=True))
        return s_self, s_ctx, m

    def finish(c, s_self, s_ctx, m):
        p = jnp.exp2(s_self - m)
        l = jnp.sum(p, axis=-1, keepdims=True)
        acc = jnp.dot(p.astype(BF16), vs_ref[...], preferred_element_type=F32)
        if cache:
            p = jnp.exp2(s_ctx - m)
            l = l + jnp.sum(p, axis=-1, keepdims=True)
            acc = acc + jnp.dot(p.astype(BF16), vc_ref[...], preferred_element_type=F32)
        o_ref[c * rows:(c + 1) * rows, :] = (acc / l).astype(o_ref.dtype)

    pending = scores(0)
    for c in range(groups):
        following = scores(c + 1) if c + 1 < groups else None
        finish(c, *pending)
        pending = following


def _ctx_attn_body(qa_ref, ka_ref, va_ref, qb_ref, kb_ref, vb_ref, qc_ref, kc_ref, vc_ref, oa_ref, ob_ref, oc_ref):
    def head(q, k, v):
        s = lax.dot_general(q, k, NT_DIMS, preferred_element_type=F32)
        p = jnp.exp2(s - jnp.max(s, axis=-1, keepdims=True))
        acc = jnp.dot(p.astype(BF16), v, preferred_element_type=F32)
        return (acc / jnp.sum(p, axis=-1, keepdims=True)).astype(BF16)

    def lanes(ref, i, w):
        return ref[:, i * w:(i + 1) * w]

    for h in range(HEADS_A):
        g = h // (HEADS_A // KV_HEADS_A)
        oa_ref[:, h * HEAD_DIM:(h + 1) * HEAD_DIM] = head(
            lanes(qa_ref, h, HEAD_DIM), lanes(ka_ref, g, HEAD_DIM), lanes(va_ref, g, HEAD_DIM))
    for h in range(HEADS_B):
        ob_ref[:, h * V_DIM_B:(h + 1) * V_DIM_B] = head(
            lanes(qb_ref, h, QB_PAD), lanes(kb_ref, h, QB_PAD), lanes(vb_ref, h, V_DIM_B))
    for h in range(HEADS_C):
        oc_ref[:, h * HEAD_DIM:(h + 1) * HEAD_DIM] = head(
            lanes(qc_ref, h, HEAD_DIM), lanes(kc_ref, h, HEAD_DIM), lanes(vc_ref, h, HEAD_DIM))


def _ctx_attention(qa, ka, va, qb, kb, vb, qc, kc, vc):
    b, n, _ = qa.shape
    args = [qa, ka, va, qb, kb, vb, qc, kc, vc]
    widths = [WIDTH_A, WIDTH_B, WIDTH_C]
    return pl.pallas_call(
        _ctx_attn_body,
        grid=(b,),
        in_specs=[pl.BlockSpec((None, n, a.shape[2]), lambda bi: (bi, 0, 0)) for a in args],
        out_specs=[pl.BlockSpec((None, n, w), lambda bi: (bi, 0, 0)) for w in widths],
        out_shape=[jax.ShapeDtypeStruct((b, n, w), BF16) for w in widths],
        compiler_params=_params("parallel"),
        name="ctx_attention",
    )(*args)


def _attention(q, ks, vs, kc, vc, layer, heads, group, dq, dv):
    b, n, _ = q.shape
    ms = ks.shape[1]
    tq = min(n, ATTN_GROUPS * TOKEN_TILE)
    cache = kc is not None
    in_specs = [
        pl.BlockSpec((None, tq, dq), lambda bi, h, i: (bi, i, h)),
        pl.BlockSpec((None, ms, dq), lambda bi, h, i: (bi, 0, h // group)),
        pl.BlockSpec((None, ms, dv), lambda bi, h, i: (bi, 0, h // group)),
    ]
    args = [q, ks, vs]
    if cache:
        mc = kc.shape[2]
        in_specs += [
            pl.BlockSpec((None, None, mc, dq), lambda bi, h, i: (bi, layer, 0, h // group)),
            pl.BlockSpec((None, None, mc, dv), lambda bi, h, i: (bi, layer, 0, h // group)),
        ]
        args += [kc, vc]
    return pl.pallas_call(
        functools.partial(_attn_body, cache=cache),
        grid=(b, heads, n // tq),
        in_specs=in_specs,
        out_specs=pl.BlockSpec((None, tq, dv), lambda bi, h, i: (bi, i, h)),
        out_shape=jax.ShapeDtypeStruct((b, n, heads * dv), BF16),
        compiler_params=_params("parallel", "parallel", "parallel"),
        name="attention",
    )(*args)


def _nat_body(q_ref, ks_ref, vs_ref, kc_ref, vc_ref, bias_ref, o_ref):
    tq = NAT_QROWS * GRID_W
    nk = NAT_KROWS * GRID_W
    groups = q_ref.shape[0] // tq
    key_rows = ks_ref.shape[0] // GRID_W
    last = ks_ref.shape[0] // tq - 1

    def scores(c):
        r = pl.program_id(2) * groups + c
        kr0 = jnp.clip(NAT_QROWS * r - WIN_ROWS_MAX // 2, 0, key_rows - NAT_KROWS)
        start = pl.multiple_of(kr0 * GRID_W, tq)
        kind = jnp.where(r == 0, 0, jnp.where(r == last, 2, 1))
        q = q_ref[c * tq:(c + 1) * tq, :]
        s_win = lax.dot_general(q, ks_ref[pl.ds(start, nk), :], NT_DIMS, preferred_element_type=F32)
        s_win = s_win + bias_ref[kind]
        s_ctx = lax.dot_general(q, kc_ref[...], NT_DIMS, preferred_element_type=F32)
        m = jnp.maximum(jnp.max(s_win, axis=-1, keepdims=True), jnp.max(s_ctx, axis=-1, keepdims=True))
        return s_win, s_ctx, m, start

    def finish(c, s_win, s_ctx, m, start):
        p_win = jnp.exp2(s_win - m)
        p_ctx = jnp.exp2(s_ctx - m)
        l = jnp.sum(p_win, axis=-1, keepdims=True) + jnp.sum(p_ctx, axis=-1, keepdims=True)
        acc = jnp.dot(p_win.astype(BF16), vs_ref[pl.ds(start, nk), :], preferred_element_type=F32)
        acc = acc + jnp.dot(p_ctx.astype(BF16), vc_ref[...], preferred_element_type=F32)
        o_ref[c * tq:(c + 1) * tq, :] = (acc / l).astype(o_ref.dtype)

    pending = scores(0)
    for c in range(groups):
        following = scores(c + 1) if c + 1 < groups else None
        finish(c, *pending)
        pending = following


N_DROW = 2 * WIN_ROWS_MAX - 1
N_DCOL = 2 * WIN_COLS - 1


def _nat_bias_body(b_ref, o_ref, *, rows):
    base = (pl.program_id(0) * HEADS_C + pl.program_id(1)) * (N_DROW * N_DCOL)
    c = lax.broadcasted_iota(jnp.int32, (GRID_W, GRID_W), 0)
    kc = lax.broadcasted_iota(jnp.int32, (GRID_W, GRID_W), 1)
    c0 = jnp.clip(c - WIN_COLS // 2, 0, GRID_W - WIN_COLS)
    in_window = (kc >= c0) & (kc < c0 + WIN_COLS)
    dc = kc - c + (WIN_COLS - 1)
    masked = jnp.full((GRID_W, GRID_W), MASK_VALUE, F32)
    tables = {}

    def table(dr):
        if dr not in tables:
            t = jnp.zeros((GRID_W, GRID_W), F32)
            for j in range(N_DCOL):
                t = jnp.where(dc == j, b_ref[base + dr * N_DCOL + j] * LOG2E, t)
            tables[dr] = jnp.where(in_window, t, MASK_VALUE)
        return tables[dr]

    kh = min(WIN_ROWS_MAX, rows)
    for ty, blk in enumerate((0, 1, rows // NAT_QROWS - 1)):
        r_first = NAT_QROWS * blk
        kr0 = min(max(r_first - WIN_ROWS_MAX // 2, 0), rows - NAT_KROWS)
        for a in range(NAT_QROWS):
            r = r_first + a
            r0 = min(max(r - kh // 2, 0), rows - kh)
            for i in range(NAT_KROWS):
                kr = kr0 + i
                blkval = table(kr - r + WIN_ROWS_MAX - 1) if r0 <= kr < r0 + kh else masked
                o_ref[ty, a * GRID_W:(a + 1) * GRID_W, i * GRID_W:(i + 1) * GRID_W] = blkval


def _nat_bias(na_bias, rows):
    depth = na_bias.shape[0]
    tq, nk = NAT_QROWS * GRID_W, NAT_KROWS * GRID_W
    return pl.pallas_call(
        functools.partial(_nat_bias_body, rows=rows),
        grid=(depth, HEADS_C),
        in_specs=[pl.BlockSpec(memory_space=pltpu.SMEM)],
        out_specs=pl.BlockSpec((None, None, 3, tq, nk), lambda l, h: (l, h, 0, 0, 0)),
        out_shape=jax.ShapeDtypeStruct((depth, HEADS_C, 3, tq, nk), F32),
        compiler_params=_params("parallel", "parallel"),
        name="nat_bias",
    )(na_bias.reshape(-1))


def _neighbourhood_attention(q, ks, vs, kc, vc, bias, layer):
    b, n, _ = q.shape
    nblk = n // (NAT_QROWS * GRID_W)
    groups = min(nblk, ATTN_GROUPS)
    tq = groups * NAT_QROWS * GRID_W
    mc = kc.shape[2]
    d = HEAD_DIM

    return pl.pallas_call(
        _nat_body,
        grid=(b, HEADS_C, nblk // groups),
        in_specs=[
            pl.BlockSpec((None, tq, d), lambda bi, h, r: (bi, r, h)),
            pl.BlockSpec((None, n, d), lambda bi, h, r: (bi, 0, h)),
            pl.BlockSpec((None, n, d), lambda bi, h, r: (bi, 0, h)),
            pl.BlockSpec((None, None, mc, d), lambda bi, h, r: (bi, layer, 0, h)),
            pl.BlockSpec((None, None, mc, d), lambda bi, h, r: (bi, layer, 0, h)),
            pl.BlockSpec((None, None, 3, NAT_QROWS * GRID_W, NAT_KROWS * GRID_W),
                         lambda bi, h, r: (layer, h, 0, 0, 0)),
        ],
        out_specs=pl.BlockSpec((None, tq, d), lambda bi, h, r: (bi, r, h)),
        out_shape=jax.ShapeDtypeStruct((b, n, HEADS_C * d), BF16),
        compiler_params=_params("parallel", "parallel", "arbitrary"),
        name="neighbourhood_attention",
    )(q, ks, vs, kc, vc, bias)


def _cache_kv_body(ckv_ref, kr_ref, wkk_ref, wkv_ref, kb_o, vb_o):
    ckvb = ckv_ref[...].astype(BF16)
    krb = kr_ref[...].astype(BF16)
    kn = jnp.dot(ckvb, wkk_ref[...], preferred_element_type=F32)
    zeros = jnp.zeros((krb.shape[0], QB_PAD - 128 - ROPE_DIM_B), BF16)
    for hd in range(HEADS_B):
        lo = hd * QB_PAD
        kb_o[:, lo:lo + 128] = kn[:, hd * 128:(hd + 1) * 128].astype(BF16)
        kb_o[:, lo + 128:lo + 128 + ROPE_DIM_B] = krb
        kb_o[:, lo + 128 + ROPE_DIM_B:lo + QB_PAD] = zeros
    vb_o[...] = jnp.dot(ckvb, wkv_ref[...], preferred_element_type=F32).astype(BF16)


def _cache_mla_kv(cache_ckv, cache_krope, w_kv_k, w_kv_v):
    b, depth, m, _ = cache_ckv.shape
    return pl.pallas_call(
        _cache_kv_body,
        grid=(b, depth),
        in_specs=[
            pl.BlockSpec((None, None, m, KV_RANK_B), lambda bi, l: (bi, l, 0, 0)),
            pl.BlockSpec((None, None, m, ROPE_DIM_B), lambda bi, l: (bi, l, 0, 0)),
            pl.BlockSpec((None, KV_RANK_B, HEADS_B * 128), lambda bi, l: (l, 0, 0)),
            pl.BlockSpec((None, KV_RANK_B, HEADS_B * 128), lambda bi, l: (l, 0, 0)),
        ],
        out_specs=[
            pl.BlockSpec((None, None, m, HEADS_B * QB_PAD), lambda bi, l: (bi, l, 0, 0)),
            pl.BlockSpec((None, None, m, WIDTH_B), lambda bi, l: (bi, l, 0, 0)),
        ],
        out_shape=[
            jax.ShapeDtypeStruct((b, depth, m, HEADS_B * QB_PAD), BF16),
            jax.ShapeDtypeStruct((b, depth, m, WIDTH_B), BF16),
        ],
        compiler_params=_params("parallel", "parallel"),
        name="cache_mla_kv",
    )(cache_ckv, cache_krope, w_kv_k, w_kv_v)


def _out_body(oa_ref, ob_ref, oc_ref, wa_ref, wb_ref, wc_ref, x_ref, gate_ref, sh_ref, sc_ref, g_ref, wr_ref,
              x1_o, h2_o, lg_o):
    o = jnp.dot(oa_ref[...], wa_ref[...], preferred_element_type=F32)
    o = o + jnp.dot(ob_ref[...], wb_ref[...], preferred_element_type=F32)
    o = o + jnp.dot(oc_ref[...], wc_ref[...], preferred_element_type=F32)
    x1 = x_ref[...] + gate_ref[...] * o
    x1_o[...] = x1
    h = _rms(x1, g_ref[...]) * (1.0 + sc_ref[...]) + sh_ref[...]
    hb = h.astype(BF16)
    h2_o[...] = hb
    lg = jnp.dot(hb, wr_ref[...], preferred_element_type=F32)
    for c in range(lg.shape[0] // 128):
        lg_o[:, c * 128:(c + 1) * 128] = lg[c * 128:(c + 1) * 128, :].T[:N_EXPERTS, :]


def _output_side(o_a, o_b, o_c, x, mod, layer, wts):
    b, n, d = x.shape
    tm = TOKEN_TILE
    mb = mod.shape[0]
    bsel = (lambda i: i) if mb > 1 else (lambda i: 0)

    def modspec(k):
        return pl.BlockSpec((None, None, 1, d), lambda bi, i: (bsel(bi), k, 0, 0))

    def tok(w):
        return pl.BlockSpec((None, tm, w), lambda bi, i: (bi, i, 0))

    return pl.pallas_call(
        _out_body,
        grid=(b, n // tm),
        in_specs=[
            tok(WIDTH_A), tok(WIDTH_B), tok(WIDTH_C),
            _resident((None, WIDTH_A, d), lambda bi, i: (layer, 0, 0)),
            _resident((None, WIDTH_B, d), lambda bi, i: (layer, 0, 0)),
            _resident((None, WIDTH_C, d), lambda bi, i: (layer, 0, 0)),
            tok(d), modspec(2), modspec(3), modspec(4),
            pl.BlockSpec((None, 1, d), lambda bi, i: (layer, 0, 0)),
            pl.BlockSpec((None, d, 128), lambda bi, i: (layer, 0, 0)),
        ],
        out_specs=[tok(d), tok(d), pl.BlockSpec((None, N_EXPERTS, tm), lambda bi, i: (bi, 0, i))],
        out_shape=[
            jax.ShapeDtypeStruct((b, n, d), F32),
            jax.ShapeDtypeStruct((b, n, d), BF16),
            jax.ShapeDtypeStruct((b, N_EXPERTS, n), F32),
        ],
        compiler_params=_params("parallel", "parallel"),
        name="output_side",
    )(o_a, o_b, o_c, wts["w_out_a"], wts["w_out_b"], wts["w_out_c"], x, mod, mod, mod, wts["norm2_g"],
      wts["w_router"])


def _prefix_exclusive(mask):
    e, n = mask.shape
    ones = jnp.where(mask, 1.0, 0.0)
    rr = lax.broadcasted_iota(jnp.int32, (128, 128), 0)
    cc = lax.broadcasted_iota(jnp.int32, (128, 128), 1)
    tri = jnp.where(rr <= cc, 1.0, 0.0).astype(BF16)
    carry = jnp.zeros((e, 1), F32)
    outs = []
    for c in range(n // 128):
        blk = ones[:, c * 128:(c + 1) * 128]
        inc = jnp.dot(blk.astype(BF16), tri, preferred_element_type=F32)
        outs.append(inc - blk + carry)
        carry = carry + inc[:, 127:128]
    return jnp.concatenate(outs, axis=1)


def _router_body(lg_ref, rank_o, rank_t_o, aff_t_o, cum_o, *, cap, chunk):
    bb, e, n = lg_ref.shape
    lg = lg_ref[...]
    ex = jnp.exp(lg - jnp.max(lg, axis=1, keepdims=True))
    aff = (ex / jnp.sum(ex, axis=1, keepdims=True)).reshape(bb * e, n)
    key = pltpu.bitcast(aff, jnp.int32)

    def step(i, t):
        cand = t | lax.shift_left(jnp.int32(1), 30 - i)
        cnt = jnp.sum(jnp.where(key >= cand, 1.0, 0.0), axis=1, keepdims=True)
        return jnp.where(cnt >= cap, cand, t)

    thr = lax.fori_loop(0, 31, step, jnp.zeros((bb * e, 1), jnp.int32))
    above = key > thr
    tied = key == thr
    need = cap - jnp.sum(jnp.where(above, 1.0, 0.0), axis=1, keepdims=True)
    chosen = above | (tied & (_prefix_exclusive(tied) < need))
    before = _prefix_exclusive(chosen)
    rank = jnp.where(chosen, before, -1.0)
    rank_o[...] = rank.astype(jnp.int32).reshape(bb, e, n)

    lane = lax.broadcasted_iota(jnp.int32, (bb * e, 128), 1)
    cum = jnp.full((bb * e, 128), float(cap), F32)
    for k in range(n // chunk):
        cum = jnp.where(lane == k, before[:, k * chunk:k * chunk + 1], cum)
    cum_o[...] = cum.astype(jnp.int32).reshape(bb, e, 128)

    fill = jnp.full((128 - e, n), -1.0, F32)
    for bi in range(bb):
        rank_p = jnp.concatenate([rank[bi * e:(bi + 1) * e], fill], axis=0)
        aff_p = jnp.concatenate([aff[bi * e:(bi + 1) * e], fill], axis=0)
        for c in range(n // 128):
            rank_t_o[bi, c * 128:(c + 1) * 128, :] = rank_p[:, c * 128:(c + 1) * 128].T
            aff_t_o[bi, c * 128:(c + 1) * 128, :] = aff_p[:, c * 128:(c + 1) * 128].T


def _route(logits_t):
    b, e, n = logits_t.shape
    cap = CAPACITY_FACTOR * n // e
    bb = max(1, min(b, ROUTE_TOKENS // n))
    return pl.pallas_call(
        functools.partial(_router_body, cap=cap, chunk=SCATTER_TOKENS),
        grid=(b // bb,),
        in_specs=[pl.BlockSpec((bb, e, n), lambda bi: (bi, 0, 0))],
        out_specs=[
            pl.BlockSpec((bb, e, n), lambda bi: (bi, 0, 0)),
            pl.BlockSpec((bb, n, 128), lambda bi: (bi, 0, 0)),
            pl.BlockSpec((bb, n, 128), lambda bi: (bi, 0, 0)),
            pl.BlockSpec((bb, e, 128), lambda bi: (bi, 0, 0)),
        ],
        out_shape=[
            jax.ShapeDtypeStruct((b, e, n), jnp.int32),
            jax.ShapeDtypeStruct((b, n, 128), F32),
            jax.ShapeDtypeStruct((b, n, 128), F32),
            jax.ShapeDtypeStruct((b, e, 128), jnp.int32),
        ],
        compiler_params=_params("parallel"),
        name="route",
    )(logits_t)


def _gather_body(cum_ref, h_ref, rank_ref, xe_o, *, chunk, rows):
    n, d = h_ref.shape
    experts, cap, _ = xe_o.shape
    nch = n // chunk
    per_chunk = chunk // SCATTER_TOKENS
    first = (pl.program_id(0) * pl.num_programs(1) + pl.program_id(1)) * experts
    for k in range(experts):
        rank = rank_ref[k]
        if nch == 1 and cap == rows:
            slot = lax.broadcasted_iota(jnp.int32, (rows, chunk), 0)
            onehot = jnp.where(slot == rank, 1.0, 0.0).astype(BF16)
            xe_o[k] = jnp.dot(onehot, h_ref[...], preferred_element_type=F32).astype(xe_o.dtype)
            continue
        base = (first + k) * (n // SCATTER_TOKENS + 1)
        xe_o[k] = jnp.zeros((cap, d), xe_o.dtype)
        for c in range(nch):
            lo = cum_ref[base + c * per_chunk]
            hi = cum_ref[base + (c + 1) * per_chunk]
            rank_c = rank[:, c * chunk:(c + 1) * chunk]
            for jb in range(cap // rows):

                @pl.when((lo < (jb + 1) * rows) & (hi > jb * rows))
                def _():
                    slot = lax.broadcasted_iota(jnp.int32, (rows, chunk), 0) + jb * rows
                    onehot = jnp.where(slot == rank_c, 1.0, 0.0).astype(BF16)
                    picked = jnp.dot(onehot, h_ref[c * chunk:(c + 1) * chunk, :], preferred_element_type=F32)
                    xe_o[k, jb * rows:(jb + 1) * rows, :] += picked.astype(xe_o.dtype)


def _gather(h2, rank, cum_flat):
    b, n, d = h2.shape
    e = rank.shape[1]
    cap = CAPACITY_FACTOR * n // e
    chunk = min(n, GATHER_CHUNK)
    per_step = e if n <= chunk else 1
    return pl.pallas_call(
        functools.partial(_gather_body, chunk=chunk, rows=min(cap, GATHER_ROWS)),
        grid_spec=pltpu.PrefetchScalarGridSpec(
            num_scalar_prefetch=1,
            grid=(b, e // per_step),
            in_specs=[
                pl.BlockSpec((None, n, d), lambda bi, ei, cum_ref: (bi, 0, 0)),
                pl.BlockSpec((None, per_step, 1, n), lambda bi, ei, cum_ref: (bi, ei, 0, 0)),
            ],
            out_specs=pl.BlockSpec((per_step, None, cap, d), lambda bi, ei, cum_ref: (ei, bi, 0, 0)),
        ),
        out_shape=jax.ShapeDtypeStruct((e, b, cap, d), BF16),
        compiler_params=_params("parallel", "parallel"),
        name="gather",
    )(cum_flat, h2, rank)


def _ffn_body(xe_ref, wg_ref, wu_ref, wd_ref, ye_o):
    bb, cap, d = xe_ref.shape
    xe = xe_ref[...].reshape(bb * cap, d)
    g = jnp.dot(xe, wg_ref[...], preferred_element_type=F32)
    u = jnp.dot(xe, wu_ref[...], preferred_element_type=F32)
    hid = (g / (1.0 + jnp.exp(-g)) * u).astype(BF16)
    ye = jnp.dot(hid, wd_ref[...], preferred_element_type=F32)
    ye_o[...] = ye.astype(ye_o.dtype).reshape(bb, cap, d)


def _expert_ffn(xe, w_gate, w_up, w_down):
    e, b, cap, d = xe.shape
    ff = w_gate.shape[-1]
    bb = max(1, min(b, 512 // cap))
    return pl.pallas_call(
        _ffn_body,
        grid=(e, b // bb),
        in_specs=[
            pl.BlockSpec((None, bb, cap, d), lambda ei, bi: (ei, bi, 0, 0)),
            pl.BlockSpec((None, d, ff), lambda ei, bi: (ei, 0, 0)),
            pl.BlockSpec((None, d, ff), lambda ei, bi: (ei, 0, 0)),
            pl.BlockSpec((None, ff, d), lambda ei, bi: (ei, 0, 0)),
        ],
        out_specs=pl.BlockSpec((None, bb, cap, d), lambda ei, bi: (ei, bi, 0, 0)),
        out_shape=jax.ShapeDtypeStruct((e, b, cap, d), BF16),
        compiler_params=_params("parallel", "arbitrary"),
        name="expert_ffn",
    )(xe, w_gate, w_up, w_down)


def _ffn_cast_body(xe_ref, wg_ref, wu_ref, wd_ref, ye_o, wg_o, wu_o, wd_o, acc):
    f = pl.program_id(1)
    bb, cap, d = xe_ref.shape
    wg = wg_ref[...].astype(BF16)
    wu = wu_ref[...].astype(BF16)
    wd = wd_ref[...].astype(BF16)
    wg_o[...] = wg
    wu_o[...] = wu
    wd_o[...] = wd
    xe = xe_ref[...].reshape(bb * cap, d)
    g = jnp.dot(xe, wg, preferred_element_type=F32)
    u = jnp.dot(xe, wu, preferred_element_type=F32)
    hid = (g / (1.0 + jnp.exp(-g)) * u).astype(BF16)
    part = jnp.dot(hid, wd, preferred_element_type=F32)

    @pl.when(f == 0)
    def _():
        acc[...] = part

    @pl.when(f > 0)
    def _():
        acc[...] += part

    @pl.when(f == pl.num_programs(1) - 1)
    def _():
        ye_o[...] = acc[...].astype(ye_o.dtype).reshape(bb, cap, d)


def _expert_ffn_cast(xe, layer, w_gate, w_up, w_down):
    e, b, cap, d = xe.shape
    ff = w_gate.shape[-1]
    ffs = FF_SLICE
    return pl.pallas_call(
        _ffn_cast_body,
        grid=(e, ff // ffs),
        in_specs=[
            pl.BlockSpec((None, b, cap, d), lambda ei, f: (ei, 0, 0, 0)),
            pl.BlockSpec((None, None, d, ffs), lambda ei, f: (layer, ei, 0, f)),
            pl.BlockSpec((None, None, d, ffs), lambda ei, f: (layer, ei, 0, f)),
            pl.BlockSpec((None, None, ffs, d), lambda ei, f: (layer, ei, f, 0)),
        ],
        out_specs=[
            pl.BlockSpec((None, b, cap, d), lambda ei, f: (ei, 0, 0, 0)),
            pl.BlockSpec((None, d, ffs), lambda ei, f: (ei, 0, f)),
            pl.BlockSpec((None, d, ffs), lambda ei, f: (ei, 0, f)),
            pl.BlockSpec((None, ffs, d), lambda ei, f: (ei, f, 0)),
        ],
        out_shape=[
            jax.ShapeDtypeStruct((e, b, cap, d), BF16),
            jax.ShapeDtypeStruct((e, d, ff), BF16),
            jax.ShapeDtypeStruct((e, d, ff), BF16),
            jax.ShapeDtypeStruct((e, ff, d), BF16),
        ],
        scratch_shapes=[pltpu.VMEM((b * cap, d), F32)],
        compiler_params=_params("parallel", "arbitrary"),
        name="expert_ffn_cast",
    )(xe, w_gate, w_up, w_down)


def _scatter_body(cum_ref, ye_ref, x_ref, rank_ref, aff_ref, gate_ref, o_ref):
    e, cap, dh = ye_ref.shape
    tn = x_ref.shape[0]
    sub_tiles = tn // SCATTER_TOKENS
    win = min(cap, 2 * SCATTER_TOKENS)
    per_expert = pl.num_programs(2) * sub_tiles + 1
    col = lax.broadcasted_iota(jnp.int32, (SCATTER_TOKENS, win), 1).astype(F32)
    for sub in range(sub_tiles):
        rows = slice(sub * SCATTER_TOKENS, (sub + 1) * SCATTER_TOKENS)
        acc = jnp.zeros((SCATTER_TOKENS, dh), F32)
        for ei in range(e):
            rank = rank_ref[rows, ei:ei + 1]
            if cap > win:
                first = cum_ref[(pl.program_id(0) * e + ei) * per_expert + pl.program_id(2) * sub_tiles + sub]
                start = jnp.clip((first // SCATTER_TOKENS) * SCATTER_TOKENS, 0, cap - win)
                start = pl.multiple_of(start, SCATTER_TOKENS)
                onehot = jnp.where(rank - start.astype(F32) == col, 1.0, 0.0).astype(BF16)
                contrib = jnp.dot(onehot, ye_ref[ei, pl.ds(start, win), :], preferred_element_type=F32)
            else:
                onehot = jnp.where(rank == col, 1.0, 0.0).astype(BF16)
                contrib = jnp.dot(onehot, ye_ref[ei], preferred_element_type=F32)
            acc = acc + contrib * aff_ref[rows, ei:ei + 1]
        o_ref[rows, :] = x_ref[rows, :] + gate_ref[...] * acc


def _scatter(ye, x1, rank_t, aff_t, mod, cum_flat):
    e, b, cap, d = ye.shape
    n = x1.shape[1]
    tn = min(n, 512)
    dh = d // 2
    mb = mod.shape[0]
    bsel = (lambda i: i) if mb > 1 else (lambda i: 0)
    return pl.pallas_call(
        _scatter_body,
        grid_spec=pltpu.PrefetchScalarGridSpec(
            num_scalar_prefetch=1,
            grid=(b, 2, n // tn),
            in_specs=[
                pl.BlockSpec((e, None, cap, dh), lambda bi, j, i, cum_ref: (0, bi, 0, j)),
                pl.BlockSpec((None, tn, dh), lambda bi, j, i, cum_ref: (bi, i, j)),
                pl.BlockSpec((None, tn, 128), lambda bi, j, i, cum_ref: (bi, i, 0)),
                pl.BlockSpec((None, tn, 128), lambda bi, j, i, cum_ref: (bi, i, 0)),
                pl.BlockSpec((None, None, 1, dh), lambda bi, j, i, cum_ref: (bsel(bi), 5, 0, j)),
            ],
            out_specs=pl.BlockSpec((None, tn, dh), lambda bi, j, i, cum_ref: (bi, i, j)),
        ),
        out_shape=jax.ShapeDtypeStruct(x1.shape, F32),
        compiler_params=_params("parallel", "parallel", "parallel"),
        name="scatter",
    )(cum_flat, ye, x1, rank_t, aff_t, mod)


def _final_norm_body(x_ref, g_ref, o_ref):
    o_ref[...] = _rms(x_ref[...], g_ref[...])


def _final_norm(x, g):
    b, n, d = x.shape
    tm = TOKEN_TILE
    return pl.pallas_call(
        _final_norm_body,
        grid=(b, n // tm),
        in_specs=[pl.BlockSpec((None, tm, d), lambda bi, i: (bi, i, 0)), pl.BlockSpec((1, d), lambda bi, i: (0, 0))],
        out_specs=pl.BlockSpec((None, tm, d), lambda bi, i: (bi, i, 0)),
        out_shape=jax.ShapeDtypeStruct(x.shape, F32),
        compiler_params=_params("parallel", "parallel"),
        name="final_norm",
    )(x, g.reshape(1, d))


def _rope_tables(n_tokens, rot_dim):
    t = jnp.arange(n_tokens, dtype=jnp.int32)
    row = (t // GRID_W).astype(F32)
    col = (t % GRID_W).astype(F32)
    axis_dim = rot_dim // 2
    freqs = ROPE_THETA ** (-jnp.arange(0, axis_dim, 2, dtype=F32) / axis_dim)
    ang = jnp.concatenate([row[:, None] * freqs[None, :], col[:, None] * freqs[None, :]], axis=-1)
    cos, sin = jnp.cos(ang), jnp.sin(ang)
    cos2 = jnp.repeat(cos, 2, axis=-1)
    sin2 = jnp.stack([-sin, sin], axis=-1).reshape(n_tokens, rot_dim)
    pad = 128 - rot_dim
    if pad:
        cos2 = jnp.concatenate([cos2, jnp.ones((n_tokens, pad), F32)], axis=-1)
        sin2 = jnp.concatenate([sin2, jnp.zeros((n_tokens, pad), F32)], axis=-1)
    return cos2, sin2


def _prepare_weights(norm1_g, norm2_g, w_in, qa_norm_g, ka_norm_g, q_norm_b, kv_norm_b, w_q_up, w_kv_up, w_out,
                     w_router):
    depth, d, _ = w_in.shape
    kr_lo = SEG_CKV[1]
    w_in_p = jnp.concatenate(
        [w_in[:, :, :kr_lo], w_in[:, :, kr_lo + ROPE_DIM_B:], w_in[:, :, kr_lo:kr_lo + ROPE_DIM_B],
         jnp.zeros((depth, d, 128 - ROPE_DIM_B), w_in.dtype)], axis=-1).astype(BF16)
    wq = w_q_up.reshape(depth, Q_RANK_B, HEADS_B, NOPE_DIM_B + ROPE_DIM_B)
    wq = jnp.pad(wq, ((0, 0), (0, 0), (0, 0), (0, QB_PAD - NOPE_DIM_B - ROPE_DIM_B)))
    wkv = w_kv_up.reshape(depth, KV_RANK_B, HEADS_B, NOPE_DIM_B + V_DIM_B)
    return {
        "norm1_g": norm1_g.reshape(depth, 1, d),
        "norm2_g": norm2_g.reshape(depth, 1, d),
        "w_in": w_in_p,
        "qa_g": qa_norm_g.reshape(depth, 1, HEAD_DIM),
        "ka_g": ka_norm_g.reshape(depth, 1, HEAD_DIM),
        "qn_g": q_norm_b.reshape(depth, 1, Q_RANK_B),
        "kvn_g": kv_norm_b.reshape(depth, 1, KV_RANK_B),
        "w_q_up": wq.reshape(depth, Q_RANK_B, HEADS_B * QB_PAD).astype(BF16),
        "w_kv_k": wkv[..., :NOPE_DIM_B].reshape(depth, KV_RANK_B, HEADS_B * NOPE_DIM_B).astype(BF16),
        "w_kv_v": wkv[..., NOPE_DIM_B:].reshape(depth, KV_RANK_B, HEADS_B * V_DIM_B).astype(BF16),
        "w_out_a": w_out[:, :WIDTH_A].astype(BF16),
        "w_out_b": w_out[:, WIDTH_A:WIDTH_A + WIDTH_B].astype(BF16),
        "w_out_c": w_out[:, WIDTH_A + WIDTH_B:].astype(BF16),
        "w_router": jnp.pad(w_router, ((0, 0), (0, 0), (0, 128 - N_EXPERTS))).astype(BF16),
    }


def _moe_select(h2, logits_t):
    rank, rank_t, aff_t, cum = _route(logits_t)
    n = rank.shape[2]
    cum_flat = cum[:, :, :n // SCATTER_TOKENS + 1].reshape(-1)
    xe = _gather(h2, rank.reshape(rank.shape[0], rank.shape[1], 1, n), cum_flat)
    return xe, rank_t, aff_t, cum_flat


def kernel(x_prompt, x_sample, cache_a_k, cache_a_v, cache_b_ckv, cache_b_krope, cache_c_k, cache_c_v, c, c_ctx, w_ada, b_ada, norm1_g, norm2_g, w_in, qa_norm_g, ka_norm_g, q_norm_b, kv_norm_b, w_q_up, w_kv_up, na_bias, w_out, w_router, w_gate, w_up, w_down, final_norm_g):
    depth, d, _ = w_in.shape
    bp, seq, _ = x_prompt.shape
    bs, n_lat, _ = x_sample.shape
    past = cache_a_k.shape[2]
    rows = n_lat // GRID_W

    wts = _prepare_weights(norm1_g, norm2_g, w_in, qa_norm_g, ka_norm_g, q_norm_b, kv_norm_b, w_q_up, w_kv_up,
                           w_out, w_router)
    rope_tabs = _rope_tables(n_lat, HEAD_DIM) + _rope_tables(n_lat, ROPE_DIM_B)

    cond8 = jnp.concatenate([c_ctx[None], c, jnp.zeros((8 - 1 - bs, d), F32)], axis=0)
    mods = _modulation(cond8, w_ada, b_ada)
    mods = mods.reshape(depth, 8, 6, 1, d)

    ca_k = cache_a_k.reshape(bs, depth, past, KV_A).astype(BF16)
    ca_v = cache_a_v.reshape(bs, depth, past, KV_A).astype(BF16)
    cc_k = cache_c_k.reshape(bs, depth, past, WIDTH_C).astype(BF16)
    cc_v = cache_c_v.reshape(bs, depth, past, WIDTH_C).astype(BF16)
    cb_k, cb_v = _cache_mla_kv(cache_b_ckv, cache_b_krope, wts["w_kv_k"], wts["w_kv_v"])
    nat_bias = _nat_bias(na_bias, rows)

    xp, xs = x_prompt, x_sample
    states = []
    for l in range(depth):
        mod_c = mods[l, 0:1]
        mod_l = mods[l, 1:1 + bs]

        qa, ka, va, qb, kb, vb, qc, kc, vc, st = _input_side(xp, mod_c, l, wts, None, True)
        o_a, o_b, o_c = _ctx_attention(qa, ka, va, qb, kb, vb, qc, kc, vc)
        x1, h2, lg = _output_side(o_a, o_b, o_c, xp, mod_c, l, wts)
        xe, rank_t, aff_t, cum_flat = _moe_select(h2, lg)
        ye, wg16, wu16, wd16 = _expert_ffn_cast(xe, l, w_gate, w_up, w_down)
        xp = _scatter(ye, x1, rank_t, aff_t, mod_c, cum_flat)
        states.append(st)

        qa, ka, va, qb, kb, vb, qc, kc, vc = _input_side(xs, mod_l, l, wts, rope_tabs, False)
        o_a = _attention(qa, ka, va, ca_k, ca_v, l, HEADS_A, HEADS_A // KV_HEADS_A, HEAD_DIM, HEAD_DIM)
        o_b = _attention(qb, kb, vb, cb_k, cb_v, l, HEADS_B, 1, QB_PAD, V_DIM_B)
        o_c = _neighbourhood_attention(qc, kc, vc, cc_k, cc_v, nat_bias, l)
        x1, h2, lg = _output_side(o_a, o_b, o_c, xs, mod_l, l, wts)
        xe, rank_t, aff_t, cum_flat = _moe_select(h2, lg)
        ye = _expert_ffn(xe, wg16, wu16, wd16)
        xs = _scatter(ye, x1, rank_t, aff_t, mod_l, cum_flat)

    y_prompt = _final_norm(xp, final_norm_g)
    y_sample = _final_norm(xs, final_norm_g)
    st = jnp.stack(states, axis=1)
    new_a_k = st[..., ST_KA[0]:ST_KA[1]].reshape(bp, depth, seq, KV_HEADS_A, HEAD_DIM)
    new_a_v = st[..., ST_VA[0]:ST_VA[1]].reshape(bp, depth, seq, KV_HEADS_A, HEAD_DIM)
    new_b_ckv = st[..., ST_CKV[0]:ST_CKV[1]]
    new_b_krope = st[..., ST_KR[0]:ST_KR[0] + ROPE_DIM_B]
    new_c_k = st[..., ST_KC[0]:ST_KC[1]].reshape(bp, depth, seq, HEADS_C, HEAD_DIM)
    new_c_v = st[..., ST_VC[0]:ST_VC[1]].reshape(bp, depth, seq, HEADS_C, HEAD_DIM)
    return (y_prompt, y_sample, new_a_k, new_a_v, new_b_ckv, new_b_krope, new_c_k, new_c_v)
```

```python
import functools

import jax
import jax.numpy as jnp
from jax import lax
from jax.experimental import pallas as pl
from jax.experimental.pallas import tpu as pltpu

F32 = jnp.float32
BF16 = jnp.bfloat16

GRID_W = 64
HEAD_DIM = 128
HEADS_A = 6
KV_HEADS_A = 2
HEADS_B = 5
Q_RANK_B = 512
KV_RANK_B = 256
NOPE_DIM_B = 128
ROPE_DIM_B = 64
V_DIM_B = 128
HEADS_C = 5
WIN_ROWS_MAX = 8
WIN_COLS = 16
N_EXPERTS = 16
CAPACITY_FACTOR = 2
ROPE_THETA = 10000.0
EPS = 1e-6

QB_PAD = 256
WIDTH_A = HEADS_A * HEAD_DIM
WIDTH_B = HEADS_B * V_DIM_B
WIDTH_C = HEADS_C * HEAD_DIM
KV_A = KV_HEADS_A * HEAD_DIM

SEG_QA = (0, WIDTH_A)
SEG_KA = (SEG_QA[1], SEG_QA[1] + KV_A)
SEG_VA = (SEG_KA[1], SEG_KA[1] + KV_A)
SEG_CQ = (SEG_VA[1], SEG_VA[1] + Q_RANK_B)
SEG_CKV = (SEG_CQ[1], SEG_CQ[1] + KV_RANK_B)
SEG_QC = (SEG_CKV[1], SEG_CKV[1] + WIDTH_C)
SEG_KC = (SEG_QC[1], SEG_QC[1] + WIDTH_C)
SEG_VC = (SEG_KC[1], SEG_KC[1] + WIDTH_C)
SEG_KR = (SEG_VC[1], SEG_VC[1] + 128)
IN_COLS_P = SEG_KR[1]

ST_KA = (0, KV_A)
ST_VA = (ST_KA[1], ST_KA[1] + KV_A)
ST_CKV = (ST_VA[1], ST_VA[1] + KV_RANK_B)
ST_KC = (ST_CKV[1], ST_CKV[1] + WIDTH_C)
ST_VC = (ST_KC[1], ST_KC[1] + WIDTH_C)
ST_KR = (ST_VC[1], ST_VC[1] + 128)
ST_COLS = ST_KR[1]

NAT_QROWS = 4
NAT_KROWS = 12
MASK_VALUE = -1e30
LOG2E = 1.4426950408889634

TOKEN_TILE = 256
PROJ_TILE = 512
ATTN_ROWS = 512
ATTN_GROUPS = 2
NAT_GROUPS = 4
GATHER_CHUNK = 512
GATHER_ROWS = 128
SCATTER_TOKENS = 128
ROUTE_TOKENS = 4096
FF_SLICE = 256
VMEM_LIMIT = 56 * 1024 * 1024
NT_DIMS = (((1,), (1,)), ((), ()))


def _params(*sem):
    return pltpu.CompilerParams(dimension_semantics=sem, vmem_limit_bytes=VMEM_LIMIT)


def _resident(block_shape, index_map):
    return pl.BlockSpec(block_shape, index_map, pipeline_mode=pl.Buffered(1))


def _rms(x, g):
    ms = jnp.mean(x * x, axis=-1, keepdims=True)
    return x * lax.rsqrt(ms + EPS) * g


def _swap_pairs(x):
    lane = lax.broadcasted_iota(jnp.int32, x.shape, x.ndim - 1)
    nxt = pltpu.roll(x, x.shape[-1] - 1, x.ndim - 1)
    prv = pltpu.roll(x, 1, x.ndim - 1)
    return jnp.where((lane & 1) == 0, nxt, prv)


def _rope(x, cos, sin_signed):
    return x * cos + _swap_pairs(x) * sin_signed


def _mod_body(c_ref, w_ref, b_ref, o_ref):
    c = c_ref[...]
    s = (c / (1.0 + jnp.exp(-c))).astype(BF16)
    o_ref[...] = jnp.dot(s, w_ref[...].astype(BF16), preferred_element_type=F32) + b_ref[...]


def _modulation(cond8, w_ada, b_ada):
    depth, d, cols = w_ada.shape
    tn = 1024
    return pl.pallas_call(
        _mod_body,
        grid=(depth, cols // tn),
        in_specs=[
            pl.BlockSpec((8, d), lambda l, j: (0, 0)),
            pl.BlockSpec((None, d, tn), lambda l, j: (l, 0, j)),
            pl.BlockSpec((None, 1, tn), lambda l, j: (l, 0, j)),
        ],
        out_specs=pl.BlockSpec((None, 8, tn), lambda l, j: (l, 0, j)),
        out_shape=jax.ShapeDtypeStruct((depth, 8, cols), F32),
        compiler_params=_params("parallel", "parallel"),
        name="modulation",
    )(cond8, w_ada, b_ada.reshape(depth, 1, cols))


def _in_body(*refs, rope, state):
    it = iter(refs)
    x_ref, sh_ref, sc_ref, g_ref, win_ref = (next(it) for _ in range(5))
    qag_ref, kag_ref, qng_ref, kvng_ref = (next(it) for _ in range(4))
    wq_ref, wkk_ref, wkv_ref = (next(it) for _ in range(3))
    if rope:
        ca_ref, sa_ref, cb_ref, sb_ref = (next(it) for _ in range(4))
    qa_o, ka_o, va_o, qb_o, kb_o, vb_o, qc_o, kc_o, vc_o = (next(it) for _ in range(9))
    st_o = next(it) if state else None

    h = _rms(x_ref[...], g_ref[...]) * (1.0 + sc_ref[...]) + sh_ref[...]
    hb = h.astype(BF16)

    def seg(bounds):
        return jnp.dot(hb, win_ref[:, bounds[0]:bounds[1]], preferred_element_type=F32)

    def rope_a(y):
        return _rope(y, ca_ref[...], sa_ref[...]) if rope else y

    def rope_b(y):
        return _rope(y, cb_ref[...], sb_ref[...]) if rope else y

    qa = seg(SEG_QA)
    for hd in range(HEADS_A):
        lo = hd * HEAD_DIM
        y = rope_a(_rms(qa[:, lo:lo + HEAD_DIM], qag_ref[...]))
        qa_o[:, lo:lo + HEAD_DIM] = (y * (HEAD_DIM ** -0.5 * LOG2E)).astype(BF16)
    ka = seg(SEG_KA)
    for hd in range(KV_HEADS_A):
        lo = hd * HEAD_DIM
        y = rope_a(_rms(ka[:, lo:lo + HEAD_DIM], kag_ref[...]))
        ka_o[:, lo:lo + HEAD_DIM] = y.astype(BF16)
        if state:
            st_o[:, ST_KA[0] + lo:ST_KA[0] + lo + HEAD_DIM] = y
    va = seg(SEG_VA)
    va_o[...] = (va if state else va.T).astype(BF16)

    cq = _rms(seg(SEG_CQ), qng_ref[...]).astype(BF16)
    qb = jnp.dot(cq, wq_ref[...], preferred_element_type=F32)
    qscale = (NOPE_DIM_B + ROPE_DIM_B) ** -0.5 * LOG2E
    for hd in range(HEADS_B):
        lo = hd * QB_PAD
        qb_o[:, lo:lo + 128] = (qb[:, lo:lo + 128] * qscale).astype(BF16)
        qb_o[:, lo + 128:lo + 256] = (rope_b(qb[:, lo + 128:lo + 256]) * qscale).astype(BF16)
    ckv = _rms(seg(SEG_CKV), kvng_ref[...])
    ckvb = ckv.astype(BF16)
    kr = rope_b(seg(SEG_KR))
    krb = kr.astype(BF16)
    kn = jnp.dot(ckvb, wkk_ref[...], preferred_element_type=F32)
    for hd in range(HEADS_B):
        kb_o[:, hd * QB_PAD:hd * QB_PAD + 128] = kn[:, hd * 128:(hd + 1) * 128].astype(BF16)
        kb_o[:, hd * QB_PAD + 128:(hd + 1) * QB_PAD] = krb
    vb = jnp.dot(ckvb, wkv_ref[...], preferred_element_type=F32)
    vb_o[...] = (vb if state else vb.T).astype(BF16)

    qc_o[...] = (seg(SEG_QC) * (HEAD_DIM ** -0.5 * LOG2E)).astype(BF16)
    kc = seg(SEG_KC)
    kc_o[...] = kc.astype(BF16)
    vc = seg(SEG_VC)
    vc_o[...] = vc.astype(BF16)

    if state:
        st_o[:, ST_VA[0]:ST_VA[1]] = va
        st_o[:, ST_CKV[0]:ST_CKV[1]] = ckv
        st_o[:, ST_KC[0]:ST_KC[1]] = kc
        st_o[:, ST_VC[0]:ST_VC[1]] = vc
        st_o[:, ST_KR[0]:ST_KR[1]] = kr


def _input_side(x, mod, layer, wts, rope_tabs, state):
    b, n, d = x.shape
    tm = min(n, PROJ_TILE)
    mb = mod.shape[0]
    bsel = (lambda i: i) if mb > 1 else (lambda i: 0)
    rope = rope_tabs is not None

    def modspec(k):
        return pl.BlockSpec((None, None, 1, d), lambda bi, i: (bsel(bi), k, 0, 0))

    def vec(w):
        return pl.BlockSpec((None, 1, w), lambda bi, i: (layer, 0, 0))

    in_specs = [
        pl.BlockSpec((None, tm, d), lambda bi, i: (bi, i, 0)),
        modspec(0), modspec(1), vec(d),
        _resident((None, d, IN_COLS_P), lambda bi, i: (layer, 0, 0)),
        vec(HEAD_DIM), vec(HEAD_DIM), vec(Q_RANK_B), vec(KV_RANK_B),
        _resident((None, Q_RANK_B, HEADS_B * QB_PAD), lambda bi, i: (layer, 0, 0)),
        _resident((None, KV_RANK_B, HEADS_B * 128), lambda bi, i: (layer, 0, 0)),
        _resident((None, KV_RANK_B, HEADS_B * 128), lambda bi, i: (layer, 0, 0)),
    ]
    args = [x, mod, mod, wts["norm1_g"], wts["w_in"], wts["qa_g"], wts["ka_g"], wts["qn_g"], wts["kvn_g"],
            wts["w_q_up"], wts["w_kv_k"], wts["w_kv_v"]]
    if rope:
        in_specs += [pl.BlockSpec((tm, 128), lambda bi, i: (i, 0))] * 4
        args += list(rope_tabs)

    widths = [WIDTH_A, KV_A, KV_A, HEADS_B * QB_PAD, HEADS_B * QB_PAD, WIDTH_B, WIDTH_C, WIDTH_C, WIDTH_C]
    out_shape = [jax.ShapeDtypeStruct((b, n, w), BF16) for w in widths]
    out_specs = [pl.BlockSpec((None, tm, w), lambda bi, i: (bi, i, 0)) for w in widths]
    if not state:
        for k in (2, 5):
            out_shape[k] = jax.ShapeDtypeStruct((b, widths[k], n), BF16)
            out_specs[k] = pl.BlockSpec((None, widths[k], tm), lambda bi, i: (bi, 0, i))
    if state:
        out_shape.append(jax.ShapeDtypeStruct((b, n, ST_COLS), F32))
        out_specs.append(pl.BlockSpec((None, tm, ST_COLS), lambda bi, i: (bi, i, 0)))

    return pl.pallas_call(
        functools.partial(_in_body, rope=rope, state=state),
        grid=(b, n // tm),
        in_specs=in_specs,
        out_specs=out_specs,
        out_shape=out_shape,
        compiler_params=_params("parallel", "parallel"),
        name="input_side",
    )(*args)


def _attn_body(q_ref, ks_ref, vs_ref, kc_ref, vc_ref, o_ref):
    rows = min(q_ref.shape[0], ATTN_ROWS)
    groups = q_ref.shape[0] // rows

    def scores(c):
        q = q_ref[c * rows:(c + 1) * rows, :]
        s_self = lax.dot_general(ks_ref[...], q, NT_DIMS, preferred_element_type=F32)
        s_ctx = lax.dot_general(kc_ref[...], q, NT_DIMS, preferred_element_type=F32)
        m = jnp.maximum(jnp.max(s_self, axis=0, keepdims=True), jnp.max(s_ctx, axis=0, keepdims=True))
        return s_self, s_ctx, m

    def finish(c, s_self, s_ctx, m):
        p_self = jnp.exp2(s_self - m)
        p_ctx = jnp.exp2(s_ctx - m)
        l = jnp.sum(p_self, axis=0, keepdims=True) + jnp.sum(p_ctx, axis=0, keepdims=True)
        acc = jnp.dot(vs_ref[...], p_self.astype(BF16), preferred_element_type=F32)
        acc = acc + jnp.dot(vc_ref[...], p_ctx.astype(BF16), preferred_element_type=F32)
        o_ref[c * rows:(c + 1) * rows, :] = (acc / l).T.astype(o_ref.dtype)

    pending = scores(0)
    for c in range(groups):
        following = scores(c + 1) if c + 1 < groups else None
        finish(c, *pending)
        pending = following


def _ctx_attn_body(qa_ref, ka_ref, va_ref, qb_ref, kb_ref, vb_ref, qc_ref, kc_ref, vc_ref, oa_ref, ob_ref, oc_ref):
    def head(q, k, v):
        s = lax.dot_general(q, k, NT_DIMS, preferred_element_type=F32)
        p = jnp.exp2(s - jnp.max(s, axis=-1, keepdims=True))
        acc = jnp.dot(p.astype(BF16), v, preferred_element_type=F32)
        return (acc / jnp.sum(p, axis=-1, keepdims=True)).astype(BF16)

    def lanes(ref, i, w):
        return ref[:, i * w:(i + 1) * w]

    for h in range(HEADS_A):
        g = h // (HEADS_A // KV_HEADS_A)
        oa_ref[:, h * HEAD_DIM:(h + 1) * HEAD_DIM] = head(
            lanes(qa_ref, h, HEAD_DIM), lanes(ka_ref, g, HEAD_DIM), lanes(va_ref, g, HEAD_DIM))
    for h in range(HEADS_B):
        ob_ref[:, h * V_DIM_B:(h + 1) * V_DIM_B] = head(
            lanes(qb_ref, h, QB_PAD), lanes(kb_ref, h, QB_PAD), lanes(vb_ref, h, V_DIM_B))
    for h in range(HEADS_C):
        oc_ref[:, h * HEAD_DIM:(h + 1) * HEAD_DIM] = head(
            lanes(qc_ref, h, HEAD_DIM), lanes(kc_ref, h, HEAD_DIM), lanes(vc_ref, h, HEAD_DIM))


def _ctx_attention(qa, ka, va, qb, kb, vb, qc, kc, vc):
    b, n, _ = qa.shape
    args = [qa, ka, va, qb, kb, vb, qc, kc, vc]
    widths = [WIDTH_A, WIDTH_B, WIDTH_C]
    return pl.pallas_call(
        _ctx_attn_body,
        grid=(b,),
        in_specs=[pl.BlockSpec((None, n, a.shape[2]), lambda bi: (bi, 0, 0)) for a in args],
        out_specs=[pl.BlockSpec((None, n, w), lambda bi: (bi, 0, 0)) for w in widths],
        out_shape=[jax.ShapeDtypeStruct((b, n, w), BF16) for w in widths],
        compiler_params=_params("parallel"),
        name="ctx_attention",
    )(*args)


def _attention(q, ks, vs_t, kc, vc_t, layer, heads, group, dq, dv):
    b, n, _ = q.shape
    ms = ks.shape[1]
    mc = kc.shape[2]
    tq = min(n, ATTN_GROUPS * ATTN_ROWS)
    return pl.pallas_call(
        _attn_body,
        grid=(b, heads, n // tq),
        in_specs=[
            pl.BlockSpec((None, tq, dq), lambda bi, h, i: (bi, i, h)),
            pl.BlockSpec((None, ms, dq), lambda bi, h, i: (bi, 0, h // group)),
            pl.BlockSpec((None, dv, ms), lambda bi, h, i: (bi, h // group, 0)),
            pl.BlockSpec((None, None, mc, dq), lambda bi, h, i: (bi, layer, 0, h // group)),
            pl.BlockSpec((None, None, dv, mc), lambda bi, h, i: (bi, layer, h // group, 0)),
        ],
        out_specs=pl.BlockSpec((None, tq, dv), lambda bi, h, i: (bi, i, h)),
        out_shape=jax.ShapeDtypeStruct((b, n, heads * dv), BF16),
        compiler_params=_params("parallel", "parallel", "parallel"),
        name="attention",
    )(q, ks, vs_t, kc, vc_t)


def _nat_body(q_ref, ks_ref, vs_ref, kc_ref, vc_ref, bias_ref, o_ref):
    tq = NAT_QROWS * GRID_W
    nk = NAT_KROWS * GRID_W
    groups = q_ref.shape[0] // tq
    key_rows = ks_ref.shape[0] // GRID_W
    last = ks_ref.shape[0] // tq - 1

    def scores(c):
        r = pl.program_id(2) * groups + c
        kr0 = jnp.clip(NAT_QROWS * r - WIN_ROWS_MAX // 2, 0, key_rows - NAT_KROWS)
        start = pl.multiple_of(kr0 * GRID_W, tq)
        kind = jnp.where(r == 0, 0, jnp.where(r == last, 2, 1))
        q = q_ref[c * tq:(c + 1) * tq, :]
        s_win = lax.dot_general(q, ks_ref[pl.ds(start, nk), :], NT_DIMS, preferred_element_type=F32)
        s_win = s_win + bias_ref[kind]
        s_ctx = lax.dot_general(q, kc_ref[...], NT_DIMS, preferred_element_type=F32)
        m = jnp.maximum(jnp.max(s_win, axis=-1, keepdims=True), jnp.max(s_ctx, axis=-1, keepdims=True))
        return s_win, s_ctx, m, start

    def finish(c, s_win, s_ctx, m, start):
        p_win = jnp.exp2(s_win - m)
        p_ctx = jnp.exp2(s_ctx - m)
        l = jnp.sum(p_win, axis=-1, keepdims=True) + jnp.sum(p_ctx, axis=-1, keepdims=True)
        acc = jnp.dot(p_win.astype(BF16), vs_ref[pl.ds(start, nk), :], preferred_element_type=F32)
        acc = acc + jnp.dot(p_ctx.astype(BF16), vc_ref[...], preferred_element_type=F32)
        o_ref[c * tq:(c + 1) * tq, :] = (acc / l).astype(o_ref.dtype)

    pending = scores(0)
    for c in range(groups):
        following = scores(c + 1) if c + 1 < groups else None
        finish(c, *pending)
        pending = following


N_DROW = 2 * WIN_ROWS_MAX - 1
N_DCOL = 2 * WIN_COLS - 1


def _nat_bias_body(b_ref, o_ref, *, rows):
    base = (pl.program_id(0) * HEADS_C + pl.program_id(1)) * (N_DROW * N_DCOL)
    c = lax.broadcasted_iota(jnp.int32, (GRID_W, GRID_W), 0)
    kc = lax.broadcasted_iota(jnp.int32, (GRID_W, GRID_W), 1)
    c0 = jnp.clip(c - WIN_COLS // 2, 0, GRID_W - WIN_COLS)
    in_window = (kc >= c0) & (kc < c0 + WIN_COLS)
    dc = kc - c + (WIN_COLS - 1)
    masked = jnp.full((GRID_W, GRID_W), MASK_VALUE, F32)
    tables = {}

    def table(dr):
        if dr not in tables:
            t = jnp.zeros((GRID_W, GRID_W), F32)
            for j in range(N_DCOL):
                t = jnp.where(dc == j, b_ref[base + dr * N_DCOL + j] * LOG2E, t)
            tables[dr] = jnp.where(in_window, t, MASK_VALUE)
        return tables[dr]

    kh = min(WIN_ROWS_MAX, rows)
    for ty, blk in enumerate((0, 1, rows // NAT_QROWS - 1)):
        r_first = NAT_QROWS * blk
        kr0 = min(max(r_first - WIN_ROWS_MAX // 2, 0), rows - NAT_KROWS)
        for a in range(NAT_QROWS):
            r = r_first + a
            r0 = min(max(r - kh // 2, 0), rows - kh)
            for i in range(NAT_KROWS):
                kr = kr0 + i
                blkval = table(kr - r + WIN_ROWS_MAX - 1) if r0 <= kr < r0 + kh else masked
                o_ref[ty, a * GRID_W:(a + 1) * GRID_W, i * GRID_W:(i + 1) * GRID_W] = blkval


def _nat_bias(na_bias, rows):
    depth = na_bias.shape[0]
    tq, nk = NAT_QROWS * GRID_W, NAT_KROWS * GRID_W
    return pl.pallas_call(
        functools.partial(_nat_bias_body, rows=rows),
        grid=(depth, HEADS_C),
        in_specs=[pl.BlockSpec(memory_space=pltpu.SMEM)],
        out_specs=pl.BlockSpec((None, None, 3, tq, nk), lambda l, h: (l, h, 0, 0, 0)),
        out_shape=jax.ShapeDtypeStruct((depth, HEADS_C, 3, tq, nk), F32),
        compiler_params=_params("parallel", "parallel"),
        name="nat_bias",
    )(na_bias.reshape(-1))


def _neighbourhood_attention(q, ks, vs, kc, vc, bias, layer):
    b, n, _ = q.shape
    nblk = n // (NAT_QROWS * GRID_W)
    groups = min(nblk, NAT_GROUPS)
    tq = groups * NAT_QROWS * GRID_W
    mc = kc.shape[2]
    d = HEAD_DIM

    return pl.pallas_call(
        _nat_body,
        grid=(b, HEADS_C, nblk // groups),
        in_specs=[
            pl.BlockSpec((None, tq, d), lambda bi, h, r: (bi, r, h)),
            pl.BlockSpec((None, n, d), lambda bi, h, r: (bi, 0, h)),
            pl.BlockSpec((None, n, d), lambda bi, h, r: (bi, 0, h)),
            pl.BlockSpec((None, None, mc, d), lambda bi, h, r: (bi, layer, 0, h)),
            pl.BlockSpec((None, None, mc, d), lambda bi, h, r: (bi, layer, 0, h)),
            pl.BlockSpec((None, None, 3, NAT_QROWS * GRID_W, NAT_KROWS * GRID_W),
                         lambda bi, h, r: (layer, h, 0, 0, 0)),
        ],
        out_specs=pl.BlockSpec((None, tq, d), lambda bi, h, r: (bi, r, h)),
        out_shape=jax.ShapeDtypeStruct((b, n, HEADS_C * d), BF16),
        compiler_params=_params("parallel", "parallel", "arbitrary"),
        name="neighbourhood_attention",
    )(q, ks, vs, kc, vc, bias)


def _cache_kv_body(ckv_ref, kr_ref, wkk_ref, wkv_ref, kb_o, vb_o):
    ckvb = ckv_ref[...].astype(BF16)
    krb = kr_ref[...].astype(BF16)
    kn = jnp.dot(ckvb, wkk_ref[...], preferred_element_type=F32)
    zeros = jnp.zeros((krb.shape[0], QB_PAD - 128 - ROPE_DIM_B), BF16)
    for hd in range(HEADS_B):
        lo = hd * QB_PAD
        kb_o[:, lo:lo + 128] = kn[:, hd * 128:(hd + 1) * 128].astype(BF16)
        kb_o[:, lo + 128:lo + 128 + ROPE_DIM_B] = krb
        kb_o[:, lo + 128 + ROPE_DIM_B:lo + QB_PAD] = zeros
    vb_o[...] = jnp.dot(ckvb, wkv_ref[...], preferred_element_type=F32).T.astype(BF16)


def _cache_mla_kv(cache_ckv, cache_krope, w_kv_k, w_kv_v):
    b, depth, m, _ = cache_ckv.shape
    return pl.pallas_call(
        _cache_kv_body,
        grid=(b, depth),
        in_specs=[
            pl.BlockSpec((None, None, m, KV_RANK_B), lambda bi, l: (bi, l, 0, 0)),
            pl.BlockSpec((None, None, m, ROPE_DIM_B), lambda bi, l: (bi, l, 0, 0)),
            pl.BlockSpec((None, KV_RANK_B, HEADS_B * 128), lambda bi, l: (l, 0, 0)),
            pl.BlockSpec((None, KV_RANK_B, HEADS_B * 128), lambda bi, l: (l, 0, 0)),
        ],
        out_specs=[
            pl.BlockSpec((None, None, m, HEADS_B * QB_PAD), lambda bi, l: (bi, l, 0, 0)),
            pl.BlockSpec((None, None, WIDTH_B, m), lambda bi, l: (bi, l, 0, 0)),
        ],
        out_shape=[
            jax.ShapeDtypeStruct((b, depth, m, HEADS_B * QB_PAD), BF16),
            jax.ShapeDtypeStruct((b, depth, WIDTH_B, m), BF16),
        ],
        compiler_params=_params("parallel", "parallel"),
        name="cache_mla_kv",
    )(cache_ckv, cache_krope, w_kv_k, w_kv_v)


def _out_body(oa_ref, ob_ref, oc_ref, wa_ref, wb_ref, wc_ref, x_ref, gate_ref, sh_ref, sc_ref, g_ref, wr_ref,
              x1_o, h2_o, lg_o):
    o = jnp.dot(oa_ref[...], wa_ref[...], preferred_element_type=F32)
    o = o + jnp.dot(ob_ref[...], wb_ref[...], preferred_element_type=F32)
    o = o + jnp.dot(oc_ref[...], wc_ref[...], preferred_element_type=F32)
    x1 = x_ref[...] + gate_ref[...] * o
    x1_o[...] = x1
    h = _rms(x1, g_ref[...]) * (1.0 + sc_ref[...]) + sh_ref[...]
    hb = h.astype(BF16)
    h2_o[...] = hb
    lg = jnp.dot(hb, wr_ref[...], preferred_element_type=F32)
    for c in range(lg.shape[0] // 128):
        lg_o[:, c * 128:(c + 1) * 128] = lg[c * 128:(c + 1) * 128, :].T[:N_EXPERTS, :]


def _output_side(o_a, o_b, o_c, x, mod, layer, wts):
    b, n, d = x.shape
    tm = TOKEN_TILE
    mb = mod.shape[0]
    bsel = (lambda i: i) if mb > 1 else (lambda i: 0)

    def modspec(k):
        return pl.BlockSpec((None, None, 1, d), lambda bi, i: (bsel(bi), k, 0, 0))

    def tok(w):
        return pl.BlockSpec((None, tm, w), lambda bi, i: (bi, i, 0))

    return pl.pallas_call(
        _out_body,
        grid=(b, n // tm),
        in_specs=[
            tok(WIDTH_A), tok(WIDTH_B), tok(WIDTH_C),
            _resident((None, WIDTH_A, d), lambda bi, i: (layer, 0, 0)),
            _resident((None, WIDTH_B, d), lambda bi, i: (layer, 0, 0)),
            _resident((None, WIDTH_C, d), lambda bi, i: (layer, 0, 0)),
            tok(d), modspec(2), modspec(3), modspec(4),
            pl.BlockSpec((None, 1, d), lambda bi, i: (layer, 0, 0)),
            pl.BlockSpec((None, d, 128), lambda bi, i: (layer, 0, 0)),
        ],
        out_specs=[tok(d), tok(d), pl.BlockSpec((None, N_EXPERTS, tm), lambda bi, i: (bi, 0, i))],
        out_shape=[
            jax.ShapeDtypeStruct((b, n, d), F32),
            jax.ShapeDtypeStruct((b, n, d), BF16),
            jax.ShapeDtypeStruct((b, N_EXPERTS, n), F32),
        ],
        compiler_params=_params("parallel", "parallel"),
        name="output_side",
    )(o_a, o_b, o_c, wts["w_out_a"], wts["w_out_b"], wts["w_out_c"], x, mod, mod, mod, wts["norm2_g"],
      wts["w_router"])


def _prefix_exclusive(mask):
    e, n = mask.shape
    ones = jnp.where(mask, 1.0, 0.0)
    rr = lax.broadcasted_iota(jnp.int32, (128, 128), 0)
    cc = lax.broadcasted_iota(jnp.int32, (128, 128), 1)
    tri = jnp.where(rr <= cc, 1.0, 0.0).astype(BF16)
    carry = jnp.zeros((e, 1), F32)
    outs = []
    for c in range(n // 128):
        blk = ones[:, c * 128:(c + 1) * 128]
        inc = jnp.dot(blk.astype(BF16), tri, preferred_element_type=F32)
        outs.append(inc - blk + carry)
        carry = carry + inc[:, 127:128]
    return jnp.concatenate(outs, axis=1)


def _router_body(lg_ref, rank_o, rank_t_o, aff_t_o, cum_o, *, cap, chunk):
    bb, e, n = lg_ref.shape
    lg = lg_ref[...]
    ex = jnp.exp(lg - jnp.max(lg, axis=1, keepdims=True))
    aff = (ex / jnp.sum(ex, axis=1, keepdims=True)).reshape(bb * e, n)
    key = pltpu.bitcast(aff, jnp.int32)

    def step(i, t):
        cand = t | lax.shift_left(jnp.int32(1), 30 - i)
        cnt = jnp.sum(jnp.where(key >= cand, 1.0, 0.0), axis=1, keepdims=True)
        return jnp.where(cnt >= cap, cand, t)

    thr = lax.fori_loop(0, 31, step, jnp.zeros((bb * e, 1), jnp.int32))
    above = key > thr
    tied = key == thr
    need = cap - jnp.sum(jnp.where(above, 1.0, 0.0), axis=1, keepdims=True)
    chosen = above | (tied & (_prefix_exclusive(tied) < need))
    before = _prefix_exclusive(chosen)
    rank = jnp.where(chosen, before, -1.0)
    rank_o[...] = rank.astype(jnp.int32).reshape(bb, e, n)

    lane = lax.broadcasted_iota(jnp.int32, (bb * e, 128), 1)
    cum = jnp.full((bb * e, 128), float(cap), F32)
    for k in range(n // chunk):
        cum = jnp.where(lane == k, before[:, k * chunk:k * chunk + 1], cum)
    cum_o[...] = cum.astype(jnp.int32).reshape(bb, e, 128)

    fill = jnp.full((128 - e, n), -1.0, F32)
    for bi in range(bb):
        rank_p = jnp.concatenate([rank[bi * e:(bi + 1) * e], fill], axis=0)
        aff_p = jnp.concatenate([aff[bi * e:(bi + 1) * e], fill], axis=0)
        for c in range(n // 128):
            rank_t_o[bi, c * 128:(c + 1) * 128, :] = rank_p[:, c * 128:(c + 1) * 128].T
            aff_t_o[bi, c * 128:(c + 1) * 128, :] = aff_p[:, c * 128:(c + 1) * 128].T


def _route(logits_t):
    b, e, n = logits_t.shape
    cap = CAPACITY_FACTOR * n // e
    bb = max(1, min(b, ROUTE_TOKENS // n))
    return pl.pallas_call(
        functools.partial(_router_body, cap=cap, chunk=SCATTER_TOKENS),
        grid=(b // bb,),
        in_specs=[pl.BlockSpec((bb, e, n), lambda bi: (bi, 0, 0))],
        out_specs=[
            pl.BlockSpec((bb, e, n), lambda bi: (bi, 0, 0)),
            pl.BlockSpec((bb, n, 128), lambda bi: (bi, 0, 0)),
            pl.BlockSpec((bb, n, 128), lambda bi: (bi, 0, 0)),
            pl.BlockSpec((bb, e, 128), lambda bi: (bi, 0, 0)),
        ],
        out_shape=[
            jax.ShapeDtypeStruct((b, e, n), jnp.int32),
            jax.ShapeDtypeStruct((b, n, 128), F32),
            jax.ShapeDtypeStruct((b, n, 128), F32),
            jax.ShapeDtypeStruct((b, e, 128), jnp.int32),
        ],
        compiler_params=_params("parallel"),
        name="route",
    )(logits_t)


def _gather_body(cum_ref, h_ref, rank_ref, xe_o, *, chunk, rows):
    n, d = h_ref.shape
    experts, cap, _ = xe_o.shape
    nch = n // chunk
    per_chunk = chunk // SCATTER_TOKENS
    first = (pl.program_id(0) * pl.num_programs(1) + pl.program_id(1)) * experts
    for k in range(experts):
        rank = rank_ref[k]
        if nch == 1 and cap == rows:
            slot = lax.broadcasted_iota(jnp.int32, (rows, chunk), 0)
            onehot = jnp.where(slot == rank, 1.0, 0.0).astype(BF16)
            xe_o[k] = jnp.dot(onehot, h_ref[...], preferred_element_type=F32).astype(xe_o.dtype)
            continue
        base = (first + k) * (n // SCATTER_TOKENS + 1)
        xe_o[k] = jnp.zeros((cap, d), xe_o.dtype)
        for c in range(nch):
            lo = cum_ref[base + c * per_chunk]
            hi = cum_ref[base + (c + 1) * per_chunk]
            rank_c = rank[:, c * chunk:(c + 1) * chunk]
            for jb in range(cap // rows):

                @pl.when((lo < (jb + 1) * rows) & (hi > jb * rows))
                def _():
                    slot = lax.broadcasted_iota(jnp.int32, (rows, chunk), 0) + jb * rows
                    onehot = jnp.where(slot == rank_c, 1.0, 0.0).astype(BF16)
                    picked = jnp.dot(onehot, h_ref[c * chunk:(c + 1) * chunk, :], preferred_element_type=F32)
                    xe_o[k, jb * rows:(jb + 1) * rows, :] += picked.astype(xe_o.dtype)


def _gather(h2, rank, cum_flat):
    b, n, d = h2.shape
    e = rank.shape[1]
    cap = CAPACITY_FACTOR * n // e
    chunk = min(n, GATHER_CHUNK)
    per_step = e if n <= chunk else 1
    return pl.pallas_call(
        functools.partial(_gather_body, chunk=chunk, rows=min(cap, GATHER_ROWS)),
        grid_spec=pltpu.PrefetchScalarGridSpec(
            num_scalar_prefetch=1,
            grid=(b, e // per_step),
            in_specs=[
                pl.BlockSpec((None, n, d), lambda bi, ei, cum_ref: (bi, 0, 0)),
                pl.BlockSpec((None, per_step, 1, n), lambda bi, ei, cum_ref: (bi, ei, 0, 0)),
            ],
            out_specs=pl.BlockSpec((per_step, None, cap, d), lambda bi, ei, cum_ref: (ei, bi, 0, 0)),
        ),
        out_shape=jax.ShapeDtypeStruct((e, b, cap, d), BF16),
        compiler_params=_params("parallel", "parallel"),
        name="gather",
    )(cum_flat, h2, rank)


def _ffn_body(xe_ref, wg_ref, wu_ref, wd_ref, ye_o):
    bb, cap, d = xe_ref.shape
    xe = xe_ref[...].reshape(bb * cap, d)
    g = jnp.dot(xe, wg_ref[...], preferred_element_type=F32)
    u = jnp.dot(xe, wu_ref[...], preferred_element_type=F32)
    hid = (g / (1.0 + jnp.exp(-g)) * u).astype(BF16)
    ye = jnp.dot(hid, wd_ref[...], preferred_element_type=F32)
    ye_o[...] = ye.astype(ye_o.dtype).reshape(bb, cap, d)


def _expert_ffn(xe, w_gate, w_up, w_down):
    e, b, cap, d = xe.shape
    ff = w_gate.shape[-1]
    bb = max(1, min(b, 512 // cap))
    return pl.pallas_call(
        _ffn_body,
        grid=(e, b // bb),
        in_specs=[
            pl.BlockSpec((None, bb, cap, d), lambda ei, bi: (ei, bi, 0, 0)),
            pl.BlockSpec((None, d, ff), lambda ei, bi: (ei, 0, 0)),
            pl.BlockSpec((None, d, ff), lambda ei, bi: (ei, 0, 0)),
            pl.BlockSpec((None, ff, d), lambda ei, bi: (ei, 0, 0)),
        ],
        out_specs=pl.BlockSpec((None, bb, cap, d), lambda ei, bi: (ei, bi, 0, 0)),
        out_shape=jax.ShapeDtypeStruct((e, b, cap, d), BF16),
        compiler_params=_params("parallel", "arbitrary"),
        name="expert_ffn",
    )(xe, w_gate, w_up, w_down)


def _ffn_cast_body(xe_ref, wg_ref, wu_ref, wd_ref, ye_o, wg_o, wu_o, wd_o, acc):
    f = pl.program_id(1)
    bb, cap, d = xe_ref.shape
    wg = wg_ref[...].astype(BF16)
    wu = wu_ref[...].astype(BF16)
    wd = wd_ref[...].astype(BF16)
    wg_o[...] = wg
    wu_o[...] = wu
    wd_o[...] = wd
    xe = xe_ref[...].reshape(bb * cap, d)
    g = jnp.dot(xe, wg, preferred_element_type=F32)
    u = jnp.dot(xe, wu, preferred_element_type=F32)
    hid = (g / (1.0 + jnp.exp(-g)) * u).astype(BF16)
    part = jnp.dot(hid, wd, preferred_element_type=F32)

    @pl.when(f == 0)
    def _():
        acc[...] = part

    @pl.when(f > 0)
    def _():
        acc[...] += part

    @pl.when(f == pl.num_programs(1) - 1)
    def _():
        ye_o[...] = acc[...].astype(ye_o.dtype).reshape(bb, cap, d)


def _expert_ffn_cast(xe, layer, w_gate, w_up, w_down):
    e, b, cap, d = xe.shape
    ff = w_gate.shape[-1]
    ffs = FF_SLICE
    return pl.pallas_call(
        _ffn_cast_body,
        grid=(e, ff // ffs),
        in_specs=[
            pl.BlockSpec((None, b, cap, d), lambda ei, f: (ei, 0, 0, 0)),
            pl.BlockSpec((None, None, d, ffs), lambda ei, f: (layer, ei, 0, f)),
            pl.BlockSpec((None, None, d, ffs), lambda ei, f: (layer, ei, 0, f)),
            pl.BlockSpec((None, None, ffs, d), lambda ei, f: (layer, ei, f, 0)),
        ],
        out_specs=[
            pl.BlockSpec((None, b, cap, d), lambda ei, f: (ei, 0, 0, 0)),
            pl.BlockSpec((None, d, ffs), lambda ei, f: (ei, 0, f)),
            pl.BlockSpec((None, d, ffs), lambda ei, f: (ei, 0, f)),
            pl.BlockSpec((None, ffs, d), lambda ei, f: (ei, f, 0)),
        ],
        out_shape=[
            jax.ShapeDtypeStruct((e, b, cap, d), BF16),
            jax.ShapeDtypeStruct((e, d, ff), BF16),
            jax.ShapeDtypeStruct((e, d, ff), BF16),
            jax.ShapeDtypeStruct((e, ff, d), BF16),
        ],
        scratch_shapes=[pltpu.VMEM((b * cap, d), F32)],
        compiler_params=_params("parallel", "arbitrary"),
        name="expert_ffn_cast",
    )(xe, w_gate, w_up, w_down)


def _scatter_body(cum_ref, ye_ref, x_ref, rank_ref, aff_ref, gate_ref, o_ref):
    e, cap, dh = ye_ref.shape
    tn = x_ref.shape[0]
    sub_tiles = tn // SCATTER_TOKENS
    win = min(cap, 2 * SCATTER_TOKENS)
    per_expert = pl.num_programs(2) * sub_tiles + 1
    col = lax.broadcasted_iota(jnp.int32, (SCATTER_TOKENS, win), 1).astype(F32)
    for sub in range(sub_tiles):
        rows = slice(sub * SCATTER_TOKENS, (sub + 1) * SCATTER_TOKENS)
        acc = jnp.zeros((SCATTER_TOKENS, dh), F32)
        for ei in range(e):
            rank = rank_ref[rows, ei:ei + 1]
            if cap > win:
                first = cum_ref[(pl.program_id(0) * e + ei) * per_expert + pl.program_id(2) * sub_tiles + sub]
                start = jnp.clip((first // SCATTER_TOKENS) * SCATTER_TOKENS, 0, cap - win)
                start = pl.multiple_of(start, SCATTER_TOKENS)
                onehot = jnp.where(rank - start.astype(F32) == col, 1.0, 0.0).astype(BF16)
                contrib = jnp.dot(onehot, ye_ref[ei, pl.ds(start, win), :], preferred_element_type=F32)
            else:
                onehot = jnp.where(rank == col, 1.0, 0.0).astype(BF16)
                contrib = jnp.dot(onehot, ye_ref[ei], preferred_element_type=F32)
            acc = acc + contrib * aff_ref[rows, ei:ei + 1]
        o_ref[rows, :] = x_ref[rows, :] + gate_ref[...] * acc


def _scatter(ye, x1, rank_t, aff_t, mod, cum_flat):
    e, b, cap, d = ye.shape
    n = x1.shape[1]
    tn = min(n, 512)
    dh = d // 2
    mb = mod.shape[0]
    bsel = (lambda i: i) if mb > 1 else (lambda i: 0)
    return pl.pallas_call(
        _scatter_body,
        grid_spec=pltpu.PrefetchScalarGridSpec(
            num_scalar_prefetch=1,
            grid=(b, 2, n // tn),
            in_specs=[
                pl.BlockSpec((e, None, cap, dh), lambda bi, j, i, cum_ref: (0, bi, 0, j)),
                pl.BlockSpec((None, tn, dh), lambda bi, j, i, cum_ref: (bi, i, j)),
                pl.BlockSpec((None, tn, 128), lambda bi, j, i, cum_ref: (bi, i, 0)),
                pl.BlockSpec((None, tn, 128), lambda bi, j, i, cum_ref: (bi, i, 0)),
                pl.BlockSpec((None, None, 1, dh), lambda bi, j, i, cum_ref: (bsel(bi), 5, 0, j)),
            ],
            out_specs=pl.BlockSpec((None, tn, dh), lambda bi, j, i, cum_ref: (bi, i, j)),
        ),
        out_shape=jax.ShapeDtypeStruct(x1.shape, F32),
        compiler_params=_params("parallel", "parallel", "parallel"),
        name="scatter",
    )(cum_flat, ye, x1, rank_t, aff_t, mod)


def _final_norm_body(x_ref, g_ref, o_ref):
    o_ref[...] = _rms(x_ref[...], g_ref[...])


def _final_norm(x, g):
    b, n, d = x.shape
    tm = TOKEN_TILE
    return pl.pallas_call(
        _final_norm_body,
        grid=(b, n // tm),
        in_specs=[pl.BlockSpec((None, tm, d), lambda bi, i: (bi, i, 0)), pl.BlockSpec((1, d), lambda bi, i: (0, 0))],
        out_specs=pl.BlockSpec((None, tm, d), lambda bi, i: (bi, i, 0)),
        out_shape=jax.ShapeDtypeStruct(x.shape, F32),
        compiler_params=_params("parallel", "parallel"),
        name="final_norm",
    )(x, g.reshape(1, d))


def _rope_tables(n_tokens, rot_dim):
    t = jnp.arange(n_tokens, dtype=jnp.int32)
    row = (t // GRID_W).astype(F32)
    col = (t % GRID_W).astype(F32)
    axis_dim = rot_dim // 2
    freqs = ROPE_THETA ** (-jnp.arange(0, axis_dim, 2, dtype=F32) / axis_dim)
    ang = jnp.concatenate([row[:, None] * freqs[None, :], col[:, None] * freqs[None, :]], axis=-1)
    cos, sin = jnp.cos(ang), jnp.sin(ang)
    cos2 = jnp.repeat(cos, 2, axis=-1)
    sin2 = jnp.stack([-sin, sin], axis=-1).reshape(n_tokens, rot_dim)
    pad = 128 - rot_dim
    if pad:
        cos2 = jnp.concatenate([cos2, jnp.ones((n_tokens, pad), F32)], axis=-1)
        sin2 = jnp.concatenate([sin2, jnp.zeros((n_tokens, pad), F32)], axis=-1)
    return cos2, sin2


def _prepare_weights(norm1_g, norm2_g, w_in, qa_norm_g, ka_norm_g, q_norm_b, kv_norm_b, w_q_up, w_kv_up, w_out,
                     w_router):
    depth, d, _ = w_in.shape
    kr_lo = SEG_CKV[1]
    w_in_p = jnp.concatenate(
        [w_in[:, :, :kr_lo], w_in[:, :, kr_lo + ROPE_DIM_B:], w_in[:, :, kr_lo:kr_lo + ROPE_DIM_B],
         jnp.zeros((depth, d, 128 - ROPE_DIM_B), w_in.dtype)], axis=-1).astype(BF16)
    wq = w_q_up.reshape(depth, Q_RANK_B, HEADS_B, NOPE_DIM_B + ROPE_DIM_B)
    wq = jnp.pad(wq, ((0, 0), (0, 0), (0, 0), (0, QB_PAD - NOPE_DIM_B - ROPE_DIM_B)))
    wkv = w_kv_up.reshape(depth, KV_RANK_B, HEADS_B, NOPE_DIM_B + V_DIM_B)
    return {
        "norm1_g": norm1_g.reshape(depth, 1, d),
        "norm2_g": norm2_g.reshape(depth, 1, d),
        "w_in": w_in_p,
        "qa_g": qa_norm_g.reshape(depth, 1, HEAD_DIM),
        "ka_g": ka_norm_g.reshape(depth, 1, HEAD_DIM),
        "qn_g": q_norm_b.reshape(depth, 1, Q_RANK_B),
        "kvn_g": kv_norm_b.reshape(depth, 1, KV_RANK_B),
        "w_q_up": wq.reshape(depth, Q_RANK_B, HEADS_B * QB_PAD).astype(BF16),
        "w_kv_k": wkv[..., :NOPE_DIM_B].reshape(depth, KV_RANK_B, HEADS_B * NOPE_DIM_B).astype(BF16),
        "w_kv_v": wkv[..., NOPE_DIM_B:].reshape(depth, KV_RANK_B, HEADS_B * V_DIM_B).astype(BF16),
        "w_out_a": w_out[:, :WIDTH_A].astype(BF16),
        "w_out_b": w_out[:, WIDTH_A:WIDTH_A + WIDTH_B].astype(BF16),
        "w_out_c": w_out[:, WIDTH_A + WIDTH_B:].astype(BF16),
        "w_router": jnp.pad(w_router, ((0, 0), (0, 0), (0, 128 - N_EXPERTS))).astype(BF16),
    }


def _moe_select(h2, logits_t):
    rank, rank_t, aff_t, cum = _route(logits_t)
    n = rank.shape[2]
    cum_flat = cum[:, :, :n // SCATTER_TOKENS + 1].reshape(-1)
    xe = _gather(h2, rank.reshape(rank.shape[0], rank.shape[1], 1, n), cum_flat)
    return xe, rank_t, aff_t, cum_flat


def kernel(x_prompt, x_sample, cache_a_k, cache_a_v, cache_b_ckv, cache_b_krope, cache_c_k, cache_c_v, c, c_ctx, w_ada, b_ada, norm1_g, norm2_g, w_in, qa_norm_g, ka_norm_g, q_norm_b, kv_norm_b, w_q_up, w_kv_up, na_bias, w_out, w_router, w_gate, w_up, w_down, final_norm_g):
    depth, d, _ = w_in.shape
    bp, seq, _ = x_prompt.shape
    bs, n_lat, _ = x_sample.shape
    past = cache_a_k.shape[2]
    rows = n_lat // GRID_W

    wts = _prepare_weights(norm1_g, norm2_g, w_in, qa_norm_g, ka_norm_g, q_norm_b, kv_norm_b, w_q_up, w_kv_up,
                           w_out, w_router)
    rope_tabs = _rope_tables(n_lat, HEAD_DIM) + _rope_tables(n_lat, ROPE_DIM_B)

    cond8 = jnp.concatenate([c_ctx[None], c, jnp.zeros((8 - 1 - bs, d), F32)], axis=0)
    mods = _modulation(cond8, w_ada, b_ada)
    mods = mods.reshape(depth, 8, 6, 1, d)

    ca_k = cache_a_k.reshape(bs, depth, past, KV_A).astype(BF16)
    ca_v = jnp.swapaxes(cache_a_v.reshape(bs, depth, past, KV_A), 2, 3).astype(BF16)
    cc_k = cache_c_k.reshape(bs, depth, past, WIDTH_C).astype(BF16)
    cc_v = cache_c_v.reshape(bs, depth, past, WIDTH_C).astype(BF16)
    cb_k, cb_v = _cache_mla_kv(cache_b_ckv, cache_b_krope, wts["w_kv_k"], wts["w_kv_v"])
    nat_bias = _nat_bias(na_bias, rows)

    xp, xs = x_prompt, x_sample
    states = []
    for l in range(depth):
        mod_c = mods[l, 0:1]
        mod_l = mods[l, 1:1 + bs]

        qa, ka, va, qb, kb, vb, qc, kc, vc, st = _input_side(xp, mod_c, l, wts, None, True)
        o_a, o_b, o_c = _ctx_attention(qa, ka, va, qb, kb, vb, qc, kc, vc)
        x1, h2, lg = _output_side(o_a, o_b, o_c, xp, mod_c, l, wts)
        xe, rank_t, aff_t, cum_flat = _moe_select(h2, lg)
        ye, wg16, wu16, wd16 = _expert_ffn_cast(xe, l, w_gate, w_up, w_down)
        xp = _scatter(ye, x1, rank_t, aff_t, mod_c, cum_flat)
        states.append(st)

        qa, ka, va, qb, kb, vb, qc, kc, vc = _input_side(xs, mod_l, l, wts, rope_tabs, False)
        o_a = _attention(qa, ka, va, ca_k, ca_v, l, HEADS_A, HEADS_A // KV_HEADS_A, HEAD_DIM, HEAD_DIM)
        o_b = _attention(qb, kb, vb, cb_k, cb_v, l, HEADS_B, 1, QB_PAD, V_DIM_B)
        o_c = _neighbourhood_attention(qc, kc, vc, cc_k, cc_v, nat_bias, l)
        x1, h2, lg = _output_side(o_a, o_b, o_c, xs, mod_l, l, wts)
        xe, rank_t, aff_t, cum_flat = _moe_select(h2, lg)
        ye = _expert_ffn(xe, wg16, wu16, wd16)
        xs = _scatter(ye, x1, rank_t, aff_t, mod_l, cum_flat)

    y_prompt = _final_norm(xp, final_norm_g)
    y_sample = _final_norm(xs, final_norm_g)
    st = jnp.stack(states, axis=1)
    new_a_k = st[..., ST_KA[0]:ST_KA[1]].reshape(bp, depth, seq, KV_HEADS_A, HEAD_DIM)
    new_a_v = st[..., ST_VA[0]:ST_VA[1]].reshape(bp, depth, seq, KV_HEADS_A, HEAD_DIM)
    new_b_ckv = st[..., ST_CKV[0]:ST_CKV[1]]
    new_b_krope = st[..., ST_KR[0]:ST_KR[0] + ROPE_DIM_B]
    new_c_k = st[..., ST_KC[0]:ST_KC[1]].reshape(bp, depth, seq, HEADS_C, HEAD_DIM)
    new_c_v = st[..., ST_VC[0]:ST_VC[1]].reshape(bp, depth, seq, HEADS_C, HEAD_DIM)
    return (y_prompt, y_sample, new_a_k, new_a_v, new_b_ckv, new_b_krope, new_c_k, new_c_v)
```

```python
import functools

import jax
import jax.numpy as jnp
from jax import lax
from jax.experimental import pallas as pl
from jax.experimental.pallas import tpu as pltpu

F32 = jnp.float32
BF16 = jnp.bfloat16

GRID_W = 64
HEAD_DIM = 128
HEADS_A = 6
KV_HEADS_A = 2
HEADS_B = 5
Q_RANK_B = 512
KV_RANK_B = 256
NOPE_DIM_B = 128
ROPE_DIM_B = 64
V_DIM_B = 128
HEADS_C = 5
WIN_ROWS_MAX = 8
WIN_COLS = 16
N_EXPERTS = 16
CAPACITY_FACTOR = 2
ROPE_THETA = 10000.0
EPS = 1e-6

QB_PAD = 256
WIDTH_A = HEADS_A * HEAD_DIM
WIDTH_B = HEADS_B * V_DIM_B
WIDTH_C = HEADS_C * HEAD_DIM
KV_A = KV_HEADS_A * HEAD_DIM

SEG_QA = (0, WIDTH_A)
SEG_KA = (SEG_QA[1], SEG_QA[1] + KV_A)
SEG_VA = (SEG_KA[1], SEG_KA[1] + KV_A)
SEG_CQ = (SEG_VA[1], SEG_VA[1] + Q_RANK_B)
SEG_CKV = (SEG_CQ[1], SEG_CQ[1] + KV_RANK_B)
SEG_QC = (SEG_CKV[1], SEG_CKV[1] + WIDTH_C)
SEG_KC = (SEG_QC[1], SEG_QC[1] + WIDTH_C)
SEG_VC = (SEG_KC[1], SEG_KC[1] + WIDTH_C)
SEG_KR = (SEG_VC[1], SEG_VC[1] + 128)
IN_COLS_P = SEG_KR[1]

STATE_WIDTHS = (KV_A, KV_A, KV_RANK_B, ROPE_DIM_B, WIDTH_C, WIDTH_C)

NAT_QROWS = 4
NAT_KROWS = 12
MASK_VALUE = -1e30
LOG2E = 1.4426950408889634

TOKEN_TILE = 256
PROJ_TILE = 512
ATTN_ROWS = 512
ATTN_GROUPS = 2
NAT_GROUPS = 4
GATHER_CHUNK = 512
GATHER_ROWS = 128
SCATTER_TOKENS = 128
ROUTE_TOKENS = 4096
SCATTER_YE_BYTES = 16 * 1024 * 1024
FF_SLICE = 512
VMEM_LIMIT = 56 * 1024 * 1024
NT_DIMS = (((1,), (1,)), ((), ()))


def _params(*sem):
    return pltpu.CompilerParams(dimension_semantics=sem, vmem_limit_bytes=VMEM_LIMIT)


def _resident(block_shape, index_map):
    return pl.BlockSpec(block_shape, index_map, pipeline_mode=pl.Buffered(1))


def _rms(x, g):
    ms = jnp.mean(x * x, axis=-1, keepdims=True)
    return x * lax.rsqrt(ms + EPS) * g


def _swap_pairs(x):
    lane = lax.broadcasted_iota(jnp.int32, x.shape, x.ndim - 1)
    nxt = pltpu.roll(x, x.shape[-1] - 1, x.ndim - 1)
    prv = pltpu.roll(x, 1, x.ndim - 1)
    return jnp.where((lane & 1) == 0, nxt, prv)


def _rope(x, cos, sin_signed):
    return x * cos + _swap_pairs(x) * sin_signed


def _mod_body(c_ref, w_ref, b_ref, o_ref):
    c = c_ref[...]
    s = (c / (1.0 + jnp.exp(-c))).astype(BF16)
    o_ref[...] = jnp.dot(s, w_ref[...].astype(BF16), preferred_element_type=F32) + b_ref[...]


def _modulation(cond8, w_ada, b_ada):
    depth, d, cols = w_ada.shape
    tn = 1024
    return pl.pallas_call(
        _mod_body,
        grid=(depth, cols // tn),
        in_specs=[
            pl.BlockSpec((8, d), lambda l, j: (0, 0)),
            pl.BlockSpec((None, d, tn), lambda l, j: (l, 0, j)),
            pl.BlockSpec((None, 1, tn), lambda l, j: (l, 0, j)),
        ],
        out_specs=pl.BlockSpec((None, 8, tn), lambda l, j: (l, 0, j)),
        out_shape=jax.ShapeDtypeStruct((depth, 8, cols), F32),
        compiler_params=_params("parallel", "parallel"),
        name="modulation",
    )(cond8, w_ada, b_ada.reshape(depth, 1, cols))


def _in_body(*refs, rope, state, carried):
    it = iter(refs)
    x_ref, sh_ref, sc_ref, g_ref, win_ref = (next(it) for _ in range(5))
    qag_ref, kag_ref, qng_ref, kvng_ref = (next(it) for _ in range(4))
    wq_ref, wkk_ref, wkv_ref = (next(it) for _ in range(3))
    if rope:
        ca_ref, sa_ref, cb_ref, sb_ref = (next(it) for _ in range(4))
    for _ in range(carried):
        next(it)
    qa_o, ka_o, va_o, qb_o, kb_o, vb_o, qc_o, kc_o, vc_o = (next(it) for _ in range(9))
    if state:
        st_ka, st_va, st_ckv, st_kr, st_kc, st_vc = (next(it) for _ in range(6))

    h = _rms(x_ref[...], g_ref[...]) * (1.0 + sc_ref[...]) + sh_ref[...]
    hb = h.astype(BF16)

    def seg(bounds):
        return jnp.dot(hb, win_ref[:, bounds[0]:bounds[1]], preferred_element_type=F32)

    def rope_a(y):
        return _rope(y, ca_ref[...], sa_ref[...]) if rope else y

    def rope_b(y):
        return _rope(y, cb_ref[...], sb_ref[...]) if rope else y

    qa = seg(SEG_QA)
    for hd in range(HEADS_A):
        lo = hd * HEAD_DIM
        y = rope_a(_rms(qa[:, lo:lo + HEAD_DIM], qag_ref[...]))
        qa_o[:, lo:lo + HEAD_DIM] = (y * (HEAD_DIM ** -0.5 * LOG2E)).astype(BF16)
    ka = seg(SEG_KA)
    for hd in range(KV_HEADS_A):
        lo = hd * HEAD_DIM
        y = rope_a(_rms(ka[:, lo:lo + HEAD_DIM], kag_ref[...]))
        ka_o[:, lo:lo + HEAD_DIM] = y.astype(BF16)
        if state:
            st_ka[:, lo:lo + HEAD_DIM] = y
    va = seg(SEG_VA)
    va_o[...] = (va if state else va.T).astype(BF16)

    cq = _rms(seg(SEG_CQ), qng_ref[...]).astype(BF16)
    qb = jnp.dot(cq, wq_ref[...], preferred_element_type=F32)
    qscale = (NOPE_DIM_B + ROPE_DIM_B) ** -0.5 * LOG2E
    for hd in range(HEADS_B):
        lo = hd * QB_PAD
        qb_o[:, lo:lo + 128] = (qb[:, lo:lo + 128] * qscale).astype(BF16)
        qb_o[:, lo + 128:lo + 256] = (rope_b(qb[:, lo + 128:lo + 256]) * qscale).astype(BF16)
    ckv = _rms(seg(SEG_CKV), kvng_ref[...])
    ckvb = ckv.astype(BF16)
    kr = rope_b(seg(SEG_KR))
    krb = kr.astype(BF16)
    kn = jnp.dot(ckvb, wkk_ref[...], preferred_element_type=F32)
    for hd in range(HEADS_B):
        kb_o[:, hd * QB_PAD:hd * QB_PAD + 128] = kn[:, hd * 128:(hd + 1) * 128].astype(BF16)
        kb_o[:, hd * QB_PAD + 128:(hd + 1) * QB_PAD] = krb
    vb = jnp.dot(ckvb, wkv_ref[...], preferred_element_type=F32)
    vb_o[...] = (vb if state else vb.T).astype(BF16)

    qc_o[...] = (seg(SEG_QC) * (HEAD_DIM ** -0.5 * LOG2E)).astype(BF16)
    kc = seg(SEG_KC)
    kc_o[...] = kc.astype(BF16)
    vc = seg(SEG_VC)
    vc_o[...] = vc.astype(BF16)

    if state:
        st_va[...] = va
        st_ckv[...] = ckv
        st_kr[...] = kr[:, :ROPE_DIM_B]
        st_kc[...] = kc
        st_vc[...] = vc


def _input_side(x, mod, layer, wts, rope_tabs, states):
    state = states is not None
    b, n, d = x.shape
    tm = min(n, PROJ_TILE)
    mb = mod.shape[0]
    bsel = (lambda i: i) if mb > 1 else (lambda i: 0)
    rope = rope_tabs is not None

    def modspec(k):
        return pl.BlockSpec((None, None, 1, d), lambda bi, i: (bsel(bi), k, 0, 0))

    def vec(w):
        return pl.BlockSpec((None, 1, w), lambda bi, i: (layer, 0, 0))

    in_specs = [
        pl.BlockSpec((None, tm, d), lambda bi, i: (bi, i, 0)),
        modspec(0), modspec(1), vec(d),
        _resident((None, d, IN_COLS_P), lambda bi, i: (layer, 0, 0)),
        vec(HEAD_DIM), vec(HEAD_DIM), vec(Q_RANK_B), vec(KV_RANK_B),
        _resident((None, Q_RANK_B, HEADS_B * QB_PAD), lambda bi, i: (layer, 0, 0)),
        _resident((None, KV_RANK_B, HEADS_B * 128), lambda bi, i: (layer, 0, 0)),
        _resident((None, KV_RANK_B, HEADS_B * 128), lambda bi, i: (layer, 0, 0)),
    ]
    args = [x, mod, mod, wts["norm1_g"], wts["w_in"], wts["qa_g"], wts["ka_g"], wts["qn_g"], wts["kvn_g"],
            wts["w_q_up"], wts["w_kv_k"], wts["w_kv_v"]]
    if rope:
        in_specs += [pl.BlockSpec((tm, 128), lambda bi, i: (i, 0))] * 4
        args += list(rope_tabs)

    widths = [WIDTH_A, KV_A, KV_A, HEADS_B * QB_PAD, HEADS_B * QB_PAD, WIDTH_B, WIDTH_C, WIDTH_C, WIDTH_C]
    out_shape = [jax.ShapeDtypeStruct((b, n, w), BF16) for w in widths]
    out_specs = [pl.BlockSpec((None, tm, w), lambda bi, i: (bi, i, 0)) for w in widths]
    if not state:
        for k in (2, 5):
            out_shape[k] = jax.ShapeDtypeStruct((b, widths[k], n), BF16)
            out_specs[k] = pl.BlockSpec((None, widths[k], tm), lambda bi, i: (bi, 0, i))
    aliases = {}
    if state:
        depth = wts["w_in"].shape[0]
        for w in STATE_WIDTHS:
            out_shape.append(jax.ShapeDtypeStruct((b, depth, n, w), F32))
            out_specs.append(pl.BlockSpec((None, None, tm, w), lambda bi, i: (bi, layer, i, 0)))
        for k, buf in enumerate(states):
            aliases[len(args)] = len(widths) + k
            in_specs.append(pl.BlockSpec(memory_space=pl.ANY))
            args.append(buf)

    return pl.pallas_call(
        functools.partial(_in_body, rope=rope, state=state, carried=len(aliases)),
        grid=(b, n // tm),
        in_specs=in_specs,
        out_specs=out_specs,
        out_shape=out_shape,
        input_output_aliases=aliases,
        compiler_params=_params("parallel", "parallel"),
        name="input_side",
    )(*args)


def _attn_body(q_ref, ks_ref, vs_ref, kc_ref, vc_ref, o_ref):
    rows = min(q_ref.shape[0], ATTN_ROWS)
    groups = q_ref.shape[0] // rows

    def scores(c):
        q = q_ref[c * rows:(c + 1) * rows, :]
        s_self = lax.dot_general(ks_ref[...], q, NT_DIMS, preferred_element_type=F32)
        s_ctx = lax.dot_general(kc_ref[...], q, NT_DIMS, preferred_element_type=F32)
        m = jnp.maximum(jnp.max(s_self, axis=0, keepdims=True), jnp.max(s_ctx, axis=0, keepdims=True))
        return s_self, s_ctx, m

    def finish(c, s_self, s_ctx, m):
        p_self = jnp.exp2(s_self - m)
        p_ctx = jnp.exp2(s_ctx - m)
        l = jnp.sum(p_self, axis=0, keepdims=True) + jnp.sum(p_ctx, axis=0, keepdims=True)
        acc = jnp.dot(vs_ref[...], p_self.astype(BF16), preferred_element_type=F32)
        acc = acc + jnp.dot(vc_ref[...], p_ctx.astype(BF16), preferred_element_type=F32)
        o_ref[c * rows:(c + 1) * rows, :] = (acc / l).T.astype(o_ref.dtype)

    pending = scores(0)
    for c in range(groups):
        following = scores(c + 1) if c + 1 < groups else None
        finish(c, *pending)
        pending = following


def _ctx_attn_body(qa_ref, ka_ref, va_ref, qb_ref, kb_ref, vb_ref, qc_ref, kc_ref, vc_ref, oa_ref, ob_ref, oc_ref):
    def head(q, k, v):
        s = lax.dot_general(q, k, NT_DIMS, preferred_element_type=F32)
        p = jnp.exp2(s - jnp.max(s, axis=-1, keepdims=True))
        acc = jnp.dot(p.astype(BF16), v, preferred_element_type=F32)
        return (acc / jnp.sum(p, axis=-1, keepdims=True)).astype(BF16)

    def lanes(ref, i, w):
        return ref[:, i * w:(i + 1) * w]

    for h in range(HEADS_A):
        g = h // (HEADS_A // KV_HEADS_A)
        oa_ref[:, h * HEAD_DIM:(h + 1) * HEAD_DIM] = head(
            lanes(qa_ref, h, HEAD_DIM), lanes(ka_ref, g, HEAD_DIM), lanes(va_ref, g, HEAD_DIM))
    for h in range(HEADS_B):
        ob_ref[:, h * V_DIM_B:(h + 1) * V_DIM_B] = head(
            lanes(qb_ref, h, QB_PAD), lanes(kb_ref, h, QB_PAD), lanes(vb_ref, h, V_DIM_B))
    for h in range(HEADS_C):
        oc_ref[:, h * HEAD_DIM:(h + 1) * HEAD_DIM] = head(
            lanes(qc_ref, h, HEAD_DIM), lanes(kc_ref, h, HEAD_DIM), lanes(vc_ref, h, HEAD_DIM))


def _ctx_attention(qa, ka, va, qb, kb, vb, qc, kc, vc):
    b, n, _ = qa.shape
    args = [qa, ka, va, qb, kb, vb, qc, kc, vc]
    widths = [WIDTH_A, WIDTH_B, WIDTH_C]
    return pl.pallas_call(
        _ctx_attn_body,
        grid=(b,),
        in_specs=[pl.BlockSpec((None, n, a.shape[2]), lambda bi: (bi, 0, 0)) for a in args],
        out_specs=[pl.BlockSpec((None, n, w), lambda bi: (bi, 0, 0)) for w in widths],
        out_shape=[jax.ShapeDtypeStruct((b, n, w), BF16) for w in widths],
        compiler_params=_params("parallel"),
        name="ctx_attention",
    )(*args)


def _attention(q, ks, vs_t, kc, vc_t, layer, heads, group, dq, dv):
    b, n, _ = q.shape
    ms = ks.shape[1]
    mc = kc.shape[2]
    tq = min(n, ATTN_GROUPS * ATTN_ROWS)
    return pl.pallas_call(
        _attn_body,
        grid=(b, heads, n // tq),
        in_specs=[
            pl.BlockSpec((None, tq, dq), lambda bi, h, i: (bi, i, h)),
            pl.BlockSpec((None, ms, dq), lambda bi, h, i: (bi, 0, h // group)),
            pl.BlockSpec((None, dv, ms), lambda bi, h, i: (bi, h // group, 0)),
            pl.BlockSpec((None, None, mc, dq), lambda bi, h, i: (bi, layer, 0, h // group)),
            pl.BlockSpec((None, None, dv, mc), lambda bi, h, i: (bi, layer, h // group, 0)),
        ],
        out_specs=pl.BlockSpec((None, tq, dv), lambda bi, h, i: (bi, i, h)),
        out_shape=jax.ShapeDtypeStruct((b, n, heads * dv), BF16),
        compiler_params=_params("parallel", "parallel", "parallel"),
        name="attention",
    )(q, ks, vs_t, kc, vc_t)


def _nat_body(q_ref, ks_ref, vs_ref, kc_ref, vc_ref, bias_ref, o_ref):
    tq = NAT_QROWS * GRID_W
    nk = NAT_KROWS * GRID_W
    groups = q_ref.shape[0] // tq
    key_rows = ks_ref.shape[0] // GRID_W
    last = ks_ref.shape[0] // tq - 1

    def scores(c):
        r = pl.program_id(2) * groups + c
        kr0 = jnp.clip(NAT_QROWS * r - WIN_ROWS_MAX // 2, 0, key_rows - NAT_KROWS)
        start = pl.multiple_of(kr0 * GRID_W, tq)
        kind = jnp.where(r == 0, 0, jnp.where(r == last, 2, 1))
        q = q_ref[c * tq:(c + 1) * tq, :]
        s_win = lax.dot_general(q, ks_ref[pl.ds(start, nk), :], NT_DIMS, preferred_element_type=F32)
        s_win = s_win + bias_ref[kind]
        s_ctx = lax.dot_general(q, kc_ref[...], NT_DIMS, preferred_element_type=F32)
        m = jnp.maximum(jnp.max(s_win, axis=-1, keepdims=True), jnp.max(s_ctx, axis=-1, keepdims=True))
        return s_win, s_ctx, m, start

    def finish(c, s_win, s_ctx, m, start):
        p_win = jnp.exp2(s_win - m)
        p_ctx = jnp.exp2(s_ctx - m)
        l = jnp.sum(p_win, axis=-1, keepdims=True) + jnp.sum(p_ctx, axis=-1, keepdims=True)
        acc = jnp.dot(p_win.astype(BF16), vs_ref[pl.ds(start, nk), :], preferred_element_type=F32)
        acc = acc + jnp.dot(p_ctx.astype(BF16), vc_ref[...], preferred_element_type=F32)
        o_ref[c * tq:(c + 1) * tq, :] = (acc / l).astype(o_ref.dtype)

    pending = scores(0)
    for c in range(groups):
        following = scores(c + 1) if c + 1 < groups else None
        finish(c, *pending)
        pending = following


N_DROW = 2 * WIN_ROWS_MAX - 1
N_DCOL = 2 * WIN_COLS - 1


def _nat_bias_body(b_ref, o_ref, *, rows):
    base = (pl.program_id(0) * HEADS_C + pl.program_id(1)) * (N_DROW * N_DCOL)
    c = lax.broadcasted_iota(jnp.int32, (GRID_W, GRID_W), 0)
    kc = lax.broadcasted_iota(jnp.int32, (GRID_W, GRID_W), 1)
    c0 = jnp.clip(c - WIN_COLS // 2, 0, GRID_W - WIN_COLS)
    in_window = (kc >= c0) & (kc < c0 + WIN_COLS)
    dc = kc - c + (WIN_COLS - 1)
    masked = jnp.full((GRID_W, GRID_W), MASK_VALUE, F32)
    tables = {}

    def table(dr):
        if dr not in tables:
            t = jnp.zeros((GRID_W, GRID_W), F32)
            for j in range(N_DCOL):
                t = jnp.where(dc == j, b_ref[base + dr * N_DCOL + j] * LOG2E, t)
            tables[dr] = jnp.where(in_window, t, MASK_VALUE)
        return tables[dr]

    kh = min(WIN_ROWS_MAX, rows)
    for ty, blk in enumerate((0, 1, rows // NAT_QROWS - 1)):
        r_first = NAT_QROWS * blk
        kr0 = min(max(r_first - WIN_ROWS_MAX // 2, 0), rows - NAT_KROWS)
        for a in range(NAT_QROWS):
            r = r_first + a
            r0 = min(max(r - kh // 2, 0), rows - kh)
            for i in range(NAT_KROWS):
                kr = kr0 + i
                blkval = table(kr - r + WIN_ROWS_MAX - 1) if r0 <= kr < r0 + kh else masked
                o_ref[ty, a * GRID_W:(a + 1) * GRID_W, i * GRID_W:(i + 1) * GRID_W] = blkval


def _nat_bias(na_bias, rows):
    depth = na_bias.shape[0]
    tq, nk = NAT_QROWS * GRID_W, NAT_KROWS * GRID_W
    return pl.pallas_call(
        functools.partial(_nat_bias_body, rows=rows),
        grid=(depth, HEADS_C),
        in_specs=[pl.BlockSpec(memory_space=pltpu.SMEM)],
        out_specs=pl.BlockSpec((None, None, 3, tq, nk), lambda l, h: (l, h, 0, 0, 0)),
        out_shape=jax.ShapeDtypeStruct((depth, HEADS_C, 3, tq, nk), F32),
        compiler_params=_params("parallel", "parallel"),
        name="nat_bias",
    )(na_bias.reshape(-1))


def _neighbourhood_attention(q, ks, vs, kc, vc, bias, layer):
    b, n, _ = q.shape
    nblk = n // (NAT_QROWS * GRID_W)
    groups = min(nblk, NAT_GROUPS)
    tq = groups * NAT_QROWS * GRID_W
    mc = kc.shape[2]
    d = HEAD_DIM

    return pl.pallas_call(
        _nat_body,
        grid=(b, HEADS_C, nblk // groups),
        in_specs=[
            pl.BlockSpec((None, tq, d), lambda bi, h, r: (bi, r, h)),
            pl.BlockSpec((None, n, d), lambda bi, h, r: (bi, 0, h)),
            pl.BlockSpec((None, n, d), lambda bi, h, r: (bi, 0, h)),
            pl.BlockSpec((None, None, mc, d), lambda bi, h, r: (bi, layer, 0, h)),
            pl.BlockSpec((None, None, mc, d), lambda bi, h, r: (bi, layer, 0, h)),
            pl.BlockSpec((None, None, 3, NAT_QROWS * GRID_W, NAT_KROWS * GRID_W),
                         lambda bi, h, r: (layer, h, 0, 0, 0)),
        ],
        out_specs=pl.BlockSpec((None, tq, d), lambda bi, h, r: (bi, r, h)),
        out_shape=jax.ShapeDtypeStruct((b, n, HEADS_C * d), BF16),
        compiler_params=_params("parallel", "parallel", "arbitrary"),
        name="neighbourhood_attention",
    )(q, ks, vs, kc, vc, bias)


def _cache_kv_body(ckv_ref, kr_ref, wkk_ref, wkv_ref, kb_o, vb_o):
    ckvb = ckv_ref[...].astype(BF16)
    krb = kr_ref[...].astype(BF16)
    kn = jnp.dot(ckvb, wkk_ref[...], preferred_element_type=F32)
    zeros = jnp.zeros((krb.shape[0], QB_PAD - 128 - ROPE_DIM_B), BF16)
    for hd in range(HEADS_B):
        lo = hd * QB_PAD
        kb_o[:, lo:lo + 128] = kn[:, hd * 128:(hd + 1) * 128].astype(BF16)
        kb_o[:, lo + 128:lo + 128 + ROPE_DIM_B] = krb
        kb_o[:, lo + 128 + ROPE_DIM_B:lo + QB_PAD] = zeros
    vb_o[...] = jnp.dot(ckvb, wkv_ref[...], preferred_element_type=F32).T.astype(BF16)


def _cache_mla_kv(cache_ckv, cache_krope, w_kv_k, w_kv_v):
    b, depth, m, _ = cache_ckv.shape
    return pl.pallas_call(
        _cache_kv_body,
        grid=(b, depth),
        in_specs=[
            pl.BlockSpec((None, None, m, KV_RANK_B), lambda bi, l: (bi, l, 0, 0)),
            pl.BlockSpec((None, None, m, ROPE_DIM_B), lambda bi, l: (bi, l, 0, 0)),
            pl.BlockSpec((None, KV_RANK_B, HEADS_B * 128), lambda bi, l: (l, 0, 0)),
            pl.BlockSpec((None, KV_RANK_B, HEADS_B * 128), lambda bi, l: (l, 0, 0)),
        ],
        out_specs=[
            pl.BlockSpec((None, None, m, HEADS_B * QB_PAD), lambda bi, l: (bi, l, 0, 0)),
            pl.BlockSpec((None, None, WIDTH_B, m), lambda bi, l: (bi, l, 0, 0)),
        ],
        out_shape=[
            jax.ShapeDtypeStruct((b, depth, m, HEADS_B * QB_PAD), BF16),
            jax.ShapeDtypeStruct((b, depth, WIDTH_B, m), BF16),
        ],
        compiler_params=_params("parallel", "parallel"),
        name="cache_mla_kv",
    )(cache_ckv, cache_krope, w_kv_k, w_kv_v)


def _out_body(oa_ref, ob_ref, oc_ref, wa_ref, wb_ref, wc_ref, x_ref, gate_ref, sh_ref, sc_ref, g_ref, wr_ref,
              x1_o, h2_o, lg_o):
    o = jnp.dot(oa_ref[...], wa_ref[...], preferred_element_type=F32)
    o = o + jnp.dot(ob_ref[...], wb_ref[...], preferred_element_type=F32)
    o = o + jnp.dot(oc_ref[...], wc_ref[...], preferred_element_type=F32)
    x1 = x_ref[...] + gate_ref[...] * o
    x1_o[...] = x1
    h = _rms(x1, g_ref[...]) * (1.0 + sc_ref[...]) + sh_ref[...]
    hb = h.astype(BF16)
    h2_o[...] = hb
    lg = jnp.dot(hb, wr_ref[...], preferred_element_type=F32)
    for c in range(lg.shape[0] // 128):
        lg_o[:, c * 128:(c + 1) * 128] = lg[c * 128:(c + 1) * 128, :].T[:N_EXPERTS, :]


def _output_side(o_a, o_b, o_c, x, mod, layer, wts):
    b, n, d = x.shape
    tm = TOKEN_TILE
    mb = mod.shape[0]
    bsel = (lambda i: i) if mb > 1 else (lambda i: 0)

    def modspec(k):
        return pl.BlockSpec((None, None, 1, d), lambda bi, i: (bsel(bi), k, 0, 0))

    def tok(w):
        return pl.BlockSpec((None, tm, w), lambda bi, i: (bi, i, 0))

    return pl.pallas_call(
        _out_body,
        grid=(b, n // tm),
        in_specs=[
            tok(WIDTH_A), tok(WIDTH_B), tok(WIDTH_C),
            _resident((None, WIDTH_A, d), lambda bi, i: (layer, 0, 0)),
            _resident((None, WIDTH_B, d), lambda bi, i: (layer, 0, 0)),
            _resident((None, WIDTH_C, d), lambda bi, i: (layer, 0, 0)),
            tok(d), modspec(2), modspec(3), modspec(4),
            pl.BlockSpec((None, 1, d), lambda bi, i: (layer, 0, 0)),
            pl.BlockSpec((None, d, 128), lambda bi, i: (layer, 0, 0)),
        ],
        out_specs=[tok(d), tok(d), pl.BlockSpec((None, N_EXPERTS, tm), lambda bi, i: (bi, 0, i))],
        out_shape=[
            jax.ShapeDtypeStruct((b, n, d), F32),
            jax.ShapeDtypeStruct((b, n, d), BF16),
            jax.ShapeDtypeStruct((b, N_EXPERTS, n), F32),
        ],
        compiler_params=_params("parallel", "parallel"),
        name="output_side",
    )(o_a, o_b, o_c, wts["w_out_a"], wts["w_out_b"], wts["w_out_c"], x, mod, mod, mod, wts["norm2_g"],
      wts["w_router"])


def _prefix_exclusive(mask):
    e, n = mask.shape
    ones = jnp.where(mask, 1.0, 0.0)
    rr = lax.broadcasted_iota(jnp.int32, (128, 128), 0)
    cc = lax.broadcasted_iota(jnp.int32, (128, 128), 1)
    tri = jnp.where(rr <= cc, 1.0, 0.0).astype(BF16)
    carry = jnp.zeros((e, 1), F32)
    outs = []
    for c in range(n // 128):
        blk = ones[:, c * 128:(c + 1) * 128]
        inc = jnp.dot(blk.astype(BF16), tri, preferred_element_type=F32)
        outs.append(inc - blk + carry)
        carry = carry + inc[:, 127:128]
    return jnp.concatenate(outs, axis=1)


def _router_body(lg_ref, rank_o, rank_t_o, aff_t_o, cum_o, *, cap, chunk):
    bb, e, n = lg_ref.shape
    lg = lg_ref[...]
    ex = jnp.exp(lg - jnp.max(lg, axis=1, keepdims=True))
    aff = (ex / jnp.sum(ex, axis=1, keepdims=True)).reshape(bb * e, n)
    key = pltpu.bitcast(aff, jnp.int32)

    def step(i, t):
        cand = t | lax.shift_left(jnp.int32(1), 30 - i)
        cnt = jnp.sum(jnp.where(key >= cand, 1.0, 0.0), axis=1, keepdims=True)
        return jnp.where(cnt >= cap, cand, t)

    thr = lax.fori_loop(0, 31, step, jnp.zeros((bb * e, 1), jnp.int32))
    above = key > thr
    tied = key == thr
    need = cap - jnp.sum(jnp.where(above, 1.0, 0.0), axis=1, keepdims=True)
    chosen = above | (tied & (_prefix_exclusive(tied) < need))
    before = _prefix_exclusive(chosen)
    rank = jnp.where(chosen, before, -1.0)
    rank_o[...] = rank.astype(jnp.int32).reshape(bb, e, n)

    lane = lax.broadcasted_iota(jnp.int32, (bb * e, 128), 1)
    cum = jnp.full((bb * e, 128), float(cap), F32)
    for k in range(n // chunk):
        cum = jnp.where(lane == k, before[:, k * chunk:k * chunk + 1], cum)
    cum_o[...] = cum.astype(jnp.int32).reshape(bb, e, 128)

    fill = jnp.full((128 - e, n), -1.0, F32)
    for bi in range(bb):
        rank_p = jnp.concatenate([rank[bi * e:(bi + 1) * e], fill], axis=0)
        aff_p = jnp.concatenate([aff[bi * e:(bi + 1) * e], fill], axis=0)
        for c in range(n // 128):
            rank_t_o[bi, c * 128:(c + 1) * 128, :] = rank_p[:, c * 128:(c + 1) * 128].T
            aff_t_o[bi, c * 128:(c + 1) * 128, :] = aff_p[:, c * 128:(c + 1) * 128].T


def _route(logits_t):
    b, e, n = logits_t.shape
    cap = CAPACITY_FACTOR * n // e
    bb = max(1, min(b, ROUTE_TOKENS // n))
    return pl.pallas_call(
        functools.partial(_router_body, cap=cap, chunk=SCATTER_TOKENS),
        grid=(b // bb,),
        in_specs=[pl.BlockSpec((bb, e, n), lambda bi: (bi, 0, 0))],
        out_specs=[
            pl.BlockSpec((bb, e, n), lambda bi: (bi, 0, 0)),
            pl.BlockSpec((bb, n, 128), lambda bi: (bi, 0, 0)),
            pl.BlockSpec((bb, n, 128), lambda bi: (bi, 0, 0)),
            pl.BlockSpec((bb, e, 128), lambda bi: (bi, 0, 0)),
        ],
        out_shape=[
            jax.ShapeDtypeStruct((b, e, n), jnp.int32),
            jax.ShapeDtypeStruct((b, n, 128), F32),
            jax.ShapeDtypeStruct((b, n, 128), F32),
            jax.ShapeDtypeStruct((b, e, 128), jnp.int32),
        ],
        compiler_params=_params("parallel"),
        name="route",
    )(logits_t)


def _gather_body(cum_ref, h_ref, rank_ref, xe_o, *, chunk, rows):
    n, d = h_ref.shape
    experts, cap, _ = xe_o.shape
    nch = n // chunk
    per_chunk = chunk // SCATTER_TOKENS
    first = (pl.program_id(0) * pl.num_programs(1) + pl.program_id(1)) * experts
    for k in range(experts):
        rank = rank_ref[k]
        if nch == 1 and cap == rows:
            slot = lax.broadcasted_iota(jnp.int32, (rows, chunk), 0)
            onehot = jnp.where(slot == rank, 1.0, 0.0).astype(BF16)
            xe_o[k] = jnp.dot(onehot, h_ref[...], preferred_element_type=F32).astype(xe_o.dtype)
            continue
        base = (first + k) * (n // SCATTER_TOKENS + 1)
        xe_o[k] = jnp.zeros((cap, d), xe_o.dtype)
        for c in range(nch):
            lo = cum_ref[base + c * per_chunk]
            hi = cum_ref[base + (c + 1) * per_chunk]
            rank_c = rank[:, c * chunk:(c + 1) * chunk]
            for jb in range(cap // rows):

                @pl.when((lo < (jb + 1) * rows) & (hi > jb * rows))
                def _():
                    slot = lax.broadcasted_iota(jnp.int32, (rows, chunk), 0) + jb * rows
                    onehot = jnp.where(slot == rank_c, 1.0, 0.0).astype(BF16)
                    picked = jnp.dot(onehot, h_ref[c * chunk:(c + 1) * chunk, :], preferred_element_type=F32)
                    xe_o[k, jb * rows:(jb + 1) * rows, :] += picked.astype(xe_o.dtype)


def _gather(h2, rank, cum_flat):
    b, n, d = h2.shape
    e = rank.shape[1]
    cap = CAPACITY_FACTOR * n // e
    chunk = min(n, GATHER_CHUNK)
    per_step = e if n <= chunk else 1
    return pl.pallas_call(
        functools.partial(_gather_body, chunk=chunk, rows=min(cap, GATHER_ROWS)),
        grid_spec=pltpu.PrefetchScalarGridSpec(
            num_scalar_prefetch=1,
            grid=(b, e // per_step),
            in_specs=[
                pl.BlockSpec((None, n, d), lambda bi, ei, cum_ref: (bi, 0, 0)),
                pl.BlockSpec((None, per_step, 1, n), lambda bi, ei, cum_ref: (bi, ei, 0, 0)),
            ],
            out_specs=pl.BlockSpec((per_step, None, cap, d), lambda bi, ei, cum_ref: (ei, bi, 0, 0)),
        ),
        out_shape=jax.ShapeDtypeStruct((e, b, cap, d), BF16),
        compiler_params=_params("parallel", "parallel"),
        name="gather",
    )(cum_flat, h2, rank)


def _ffn_body(xe_ref, wg_ref, wu_ref, wd_ref, ye_o):
    bb, cap, d = xe_ref.shape
    xe = xe_ref[...].reshape(bb * cap, d)
    g = jnp.dot(xe, wg_ref[...], preferred_element_type=F32)
    u = jnp.dot(xe, wu_ref[...], preferred_element_type=F32)
    hid = (g / (1.0 + jnp.exp(-g)) * u).astype(BF16)
    ye = jnp.dot(hid, wd_ref[...], preferred_element_type=F32)
    ye_o[...] = ye.astype(ye_o.dtype).reshape(bb, cap, d)


def _expert_ffn(xe, w_gate, w_up, w_down):
    e, b, cap, d = xe.shape
    ff = w_gate.shape[-1]
    bb = max(1, min(b, 512 // cap))
    return pl.pallas_call(
        _ffn_body,
        grid=(e, b // bb),
        in_specs=[
            pl.BlockSpec((None, bb, cap, d), lambda ei, bi: (ei, bi, 0, 0)),
            pl.BlockSpec((None, d, ff), lambda ei, bi: (ei, 0, 0)),
            pl.BlockSpec((None, d, ff), lambda ei, bi: (ei, 0, 0)),
            pl.BlockSpec((None, ff, d), lambda ei, bi: (ei, 0, 0)),
        ],
        out_specs=pl.BlockSpec((None, bb, cap, d), lambda ei, bi: (ei, bi, 0, 0)),
        out_shape=jax.ShapeDtypeStruct((e, b, cap, d), BF16),
        compiler_params=_params("parallel", "arbitrary"),
        name="expert_ffn",
    )(xe, w_gate, w_up, w_down)


def _ffn_cast_body(xe_ref, wg_ref, wu_ref, wd_ref, ye_o, wg_o, wu_o, wd_o, acc):
    f = pl.program_id(1)
    bb, cap, d = xe_ref.shape
    wg = wg_ref[...].astype(BF16)
    wu = wu_ref[...].astype(BF16)
    wd = wd_ref[...].astype(BF16)
    wg_o[...] = wg
    wu_o[...] = wu
    wd_o[...] = wd
    xe = xe_ref[...].reshape(bb * cap, d)
    g = jnp.dot(xe, wg, preferred_element_type=F32)
    u = jnp.dot(xe, wu, preferred_element_type=F32)
    hid = (g / (1.0 + jnp.exp(-g)) * u).astype(BF16)
    part = jnp.dot(hid, wd, preferred_element_type=F32)

    @pl.when(f == 0)
    def _():
        acc[...] = part

    @pl.when(f > 0)
    def _():
        acc[...] += part

    @pl.when(f == pl.num_programs(1) - 1)
    def _():
        ye_o[...] = acc[...].astype(ye_o.dtype).reshape(bb, cap, d)


def _expert_ffn_cast(xe, layer, w_gate, w_up, w_down):
    e, b, cap, d = xe.shape
    ff = w_gate.shape[-1]
    ffs = FF_SLICE
    return pl.pallas_call(
        _ffn_cast_body,
        grid=(e, ff // ffs),
        in_specs=[
            pl.BlockSpec((None, b, cap, d), lambda ei, f: (ei, 0, 0, 0)),
            pl.BlockSpec((None, None, d, ffs), lambda ei, f: (layer, ei, 0, f)),
            pl.BlockSpec((None, None, d, ffs), lambda ei, f: (layer, ei, 0, f)),
            pl.BlockSpec((None, None, ffs, d), lambda ei, f: (layer, ei, f, 0)),
        ],
        out_specs=[
            pl.BlockSpec((None, b, cap, d), lambda ei, f: (ei, 0, 0, 0)),
            pl.BlockSpec((None, d, ffs), lambda ei, f: (ei, 0, f)),
            pl.BlockSpec((None, d, ffs), lambda ei, f: (ei, 0, f)),
            pl.BlockSpec((None, ffs, d), lambda ei, f: (ei, f, 0)),
        ],
        out_shape=[
            jax.ShapeDtypeStruct((e, b, cap, d), BF16),
            jax.ShapeDtypeStruct((e, d, ff), BF16),
            jax.ShapeDtypeStruct((e, d, ff), BF16),
            jax.ShapeDtypeStruct((e, ff, d), BF16),
        ],
        scratch_shapes=[pltpu.VMEM((b * cap, d), F32)],
        compiler_params=_params("parallel", "arbitrary"),
        name="expert_ffn_cast",
    )(xe, w_gate, w_up, w_down)


def _scatter_body(cum_ref, ye_ref, x_ref, rank_ref, aff_ref, gate_ref, o_ref):
    e, cap, dh = ye_ref.shape
    tn = x_ref.shape[0]
    sub_tiles = tn // SCATTER_TOKENS
    win = min(cap, 2 * SCATTER_TOKENS)
    per_expert = pl.num_programs(2) * sub_tiles + 1
    col = lax.broadcasted_iota(jnp.int32, (SCATTER_TOKENS, win), 1).astype(F32)
    for sub in range(sub_tiles):
        rows = slice(sub * SCATTER_TOKENS, (sub + 1) * SCATTER_TOKENS)
        acc = jnp.zeros((SCATTER_TOKENS, dh), F32)
        for ei in range(e):
            rank = rank_ref[rows, ei:ei + 1]
            if cap > win:
                first = cum_ref[(pl.program_id(0) * e + ei) * per_expert + pl.program_id(2) * sub_tiles + sub]
                start = jnp.clip((first // SCATTER_TOKENS) * SCATTER_TOKENS, 0, cap - win)
                start = pl.multiple_of(start, SCATTER_TOKENS)
                onehot = jnp.where(rank - start.astype(F32) == col, 1.0, 0.0).astype(BF16)
                contrib = jnp.dot(onehot, ye_ref[ei, pl.ds(start, win), :], preferred_element_type=F32)
            else:
                onehot = jnp.where(rank == col, 1.0, 0.0).astype(BF16)
                contrib = jnp.dot(onehot, ye_ref[ei], preferred_element_type=F32)
            acc = acc + contrib * aff_ref[rows, ei:ei + 1]
        o_ref[rows, :] = x_ref[rows, :] + gate_ref[...] * acc


def _scatter(ye, x1, rank_t, aff_t, mod, cum_flat):
    e, b, cap, d = ye.shape
    n = x1.shape[1]
    tn = min(n, 512)
    dh = d if e * cap * d * 2 <= SCATTER_YE_BYTES else d // 2
    mb = mod.shape[0]
    bsel = (lambda i: i) if mb > 1 else (lambda i: 0)
    return pl.pallas_call(
        _scatter_body,
        grid_spec=pltpu.PrefetchScalarGridSpec(
            num_scalar_prefetch=1,
            grid=(b, d // dh, n // tn),
            in_specs=[
                pl.BlockSpec((e, None, cap, dh), lambda bi, j, i, cum_ref: (0, bi, 0, j)),
                pl.BlockSpec((None, tn, dh), lambda bi, j, i, cum_ref: (bi, i, j)),
                pl.BlockSpec((None, tn, 128), lambda bi, j, i, cum_ref: (bi, i, 0)),
                pl.BlockSpec((None, tn, 128), lambda bi, j, i, cum_ref: (bi, i, 0)),
                pl.BlockSpec((None, None, 1, dh), lambda bi, j, i, cum_ref: (bsel(bi), 5, 0, j)),
            ],
            out_specs=pl.BlockSpec((None, tn, dh), lambda bi, j, i, cum_ref: (bi, i, j)),
        ),
        out_shape=jax.ShapeDtypeStruct(x1.shape, F32),
        compiler_params=_params("parallel", "parallel", "parallel"),
        name="scatter",
    )(cum_flat, ye, x1, rank_t, aff_t, mod)


def _final_norm_body(x_ref, g_ref, o_ref):
    o_ref[...] = _rms(x_ref[...], g_ref[...])


def _final_norm(x, g):
    b, n, d = x.shape
    tm = TOKEN_TILE
    return pl.pallas_call(
        _final_norm_body,
        grid=(b, n // tm),
        in_specs=[pl.BlockSpec((None, tm, d), lambda bi, i: (bi, i, 0)), pl.BlockSpec((1, d), lambda bi, i: (0, 0))],
        out_specs=pl.BlockSpec((None, tm, d), lambda bi, i: (bi, i, 0)),
        out_shape=jax.ShapeDtypeStruct(x.shape, F32),
        compiler_params=_params("parallel", "parallel"),
        name="final_norm",
    )(x, g.reshape(1, d))


def _rope_tables(n_tokens, rot_dim):
    t = jnp.arange(n_tokens, dtype=jnp.int32)
    row = (t // GRID_W).astype(F32)
    col = (t % GRID_W).astype(F32)
    axis_dim = rot_dim // 2
    freqs = ROPE_THETA ** (-jnp.arange(0, axis_dim, 2, dtype=F32) / axis_dim)
    ang = jnp.concatenate([row[:, None] * freqs[None, :], col[:, None] * freqs[None, :]], axis=-1)
    cos, sin = jnp.cos(ang), jnp.sin(ang)
    cos2 = jnp.repeat(cos, 2, axis=-1)
    sin2 = jnp.stack([-sin, sin], axis=-1).reshape(n_tokens, rot_dim)
    pad = 128 - rot_dim
    if pad:
        cos2 = jnp.concatenate([cos2, jnp.ones((n_tokens, pad), F32)], axis=-1)
        sin2 = jnp.concatenate([sin2, jnp.zeros((n_tokens, pad), F32)], axis=-1)
    return cos2, sin2


def _prepare_weights(norm1_g, norm2_g, w_in, qa_norm_g, ka_norm_g, q_norm_b, kv_norm_b, w_q_up, w_kv_up, w_out,
                     w_router):
    depth, d, _ = w_in.shape
    kr_lo = SEG_CKV[1]
    w_in16 = w_in.astype(BF16)
    w_in_p = jnp.concatenate(
        [w_in16[:, :, :kr_lo], w_in16[:, :, kr_lo + ROPE_DIM_B:], w_in16[:, :, kr_lo:kr_lo + ROPE_DIM_B],
         jnp.zeros((depth, d, 128 - ROPE_DIM_B), BF16)], axis=-1)
    wq = w_q_up.reshape(depth, Q_RANK_B, HEADS_B, NOPE_DIM_B + ROPE_DIM_B)
    wq = jnp.pad(wq, ((0, 0), (0, 0), (0, 0), (0, QB_PAD - NOPE_DIM_B - ROPE_DIM_B)))
    wkv = w_kv_up.reshape(depth, KV_RANK_B, HEADS_B, NOPE_DIM_B + V_DIM_B)
    return {
        "norm1_g": norm1_g.reshape(depth, 1, d),
        "norm2_g": norm2_g.reshape(depth, 1, d),
        "w_in": w_in_p,
        "qa_g": qa_norm_g.reshape(depth, 1, HEAD_DIM),
        "ka_g": ka_norm_g.reshape(depth, 1, HEAD_DIM),
        "qn_g": q_norm_b.reshape(depth, 1, Q_RANK_B),
        "kvn_g": kv_norm_b.reshape(depth, 1, KV_RANK_B),
        "w_q_up": wq.reshape(depth, Q_RANK_B, HEADS_B * QB_PAD).astype(BF16),
        "w_kv_k": wkv[..., :NOPE_DIM_B].reshape(depth, KV_RANK_B, HEADS_B * NOPE_DIM_B).astype(BF16),
        "w_kv_v": wkv[..., NOPE_DIM_B:].reshape(depth, KV_RANK_B, HEADS_B * V_DIM_B).astype(BF16),
        "w_out_a": w_out[:, :WIDTH_A].astype(BF16),
        "w_out_b": w_out[:, WIDTH_A:WIDTH_A + WIDTH_B].astype(BF16),
        "w_out_c": w_out[:, WIDTH_A + WIDTH_B:].astype(BF16),
        "w_router": jnp.pad(w_router, ((0, 0), (0, 0), (0, 128 - N_EXPERTS))).astype(BF16),
    }


def _moe_select(h2, logits_t):
    rank, rank_t, aff_t, cum = _route(logits_t)
    n = rank.shape[2]
    cum_flat = cum[:, :, :n // SCATTER_TOKENS + 1].reshape(-1)
    xe = _gather(h2, rank.reshape(rank.shape[0], rank.shape[1], 1, n), cum_flat)
    return xe, rank_t, aff_t, cum_flat


def kernel(x_prompt, x_sample, cache_a_k, cache_a_v, cache_b_ckv, cache_b_krope, cache_c_k, cache_c_v, c, c_ctx, w_ada, b_ada, norm1_g, norm2_g, w_in, qa_norm_g, ka_norm_g, q_norm_b, kv_norm_b, w_q_up, w_kv_up, na_bias, w_out, w_router, w_gate, w_up, w_down, final_norm_g):
    depth, d, _ = w_in.shape
    bp, seq, _ = x_prompt.shape
    bs, n_lat, _ = x_sample.shape
    past = cache_a_k.shape[2]
    rows = n_lat // GRID_W

    wts = _prepare_weights(norm1_g, norm2_g, w_in, qa_norm_g, ka_norm_g, q_norm_b, kv_norm_b, w_q_up, w_kv_up,
                           w_out, w_router)
    rope_tabs = _rope_tables(n_lat, HEAD_DIM) + _rope_tables(n_lat, ROPE_DIM_B)

    cond8 = jnp.concatenate([c_ctx[None], c, jnp.zeros((8 - 1 - bs, d), F32)], axis=0)
    mods = _modulation(cond8, w_ada, b_ada)
    mods = mods.reshape(depth, 8, 6, 1, d)

    ca_k = cache_a_k.reshape(bs, depth, past, KV_A).astype(BF16)
    ca_v = jnp.swapaxes(cache_a_v.reshape(bs, depth, past, KV_A), 2, 3).astype(BF16)
    cc_k = cache_c_k.reshape(bs, depth, past, WIDTH_C).astype(BF16)
    cc_v = cache_c_v.reshape(bs, depth, past, WIDTH_C).astype(BF16)
    cb_k, cb_v = _cache_mla_kv(cache_b_ckv, cache_b_krope, wts["w_kv_k"], wts["w_kv_v"])
    nat_bias = _nat_bias(na_bias, rows)

    xp, xs = x_prompt, x_sample
    states = ()
    for l in range(depth):
        mod_c = mods[l, 0:1]
        mod_l = mods[l, 1:1 + bs]

        qa, ka, va, qb, kb, vb, qc, kc, vc, *states = _input_side(xp, mod_c, l, wts, None, states)
        o_a, o_b, o_c = _ctx_attention(qa, ka, va, qb, kb, vb, qc, kc, vc)
        x1, h2, lg = _output_side(o_a, o_b, o_c, xp, mod_c, l, wts)
        xe, rank_t, aff_t, cum_flat = _moe_select(h2, lg)
        ye, wg16, wu16, wd16 = _expert_ffn_cast(xe, l, w_gate, w_up, w_down)
        xp = _scatter(ye, x1, rank_t, aff_t, mod_c, cum_flat)

        qa, ka, va, qb, kb, vb, qc, kc, vc = _input_side(xs, mod_l, l, wts, rope_tabs, None)
        o_a = _attention(qa, ka, va, ca_k, ca_v, l, HEADS_A, HEADS_A // KV_HEADS_A, HEAD_DIM, HEAD_DIM)
        o_b = _attention(qb, kb, vb, cb_k, cb_v, l, HEADS_B, 1, QB_PAD, V_DIM_B)
        o_c = _neighbourhood_attention(qc, kc, vc, cc_k, cc_v, nat_bias, l)
        x1, h2, lg = _output_side(o_a, o_b, o_c, xs, mod_l, l, wts)
        xe, rank_t, aff_t, cum_flat = _moe_select(h2, lg)
        ye = _expert_ffn(xe, wg16, wu16, wd16)
        xs = _scatter(ye, x1, rank_t, aff_t, mod_l, cum_flat)

    y_prompt = _final_norm(xp, final_norm_g)
    y_sample = _final_norm(xs, final_norm_g)
    st_ka, st_va, new_b_ckv, new_b_krope, st_kc, st_vc = states
    new_a_k = st_ka.reshape(bp, depth, seq, KV_HEADS_A, HEAD_DIM)
    new_a_v = st_va.reshape(bp, depth, seq, KV_HEADS_A, HEAD_DIM)
    new_c_k = st_kc.reshape(bp, depth, seq, HEADS_C, HEAD_DIM)
    new_c_v = st_vc.reshape(bp, depth, seq, HEADS_C, HEAD_DIM)
    return (y_prompt, y_sample, new_a_k, new_a_v, new_b_ckv, new_b_krope, new_c_k, new_c_v)
```

```python
import functools

import jax
import jax.numpy as jnp
from jax import lax
from jax.experimental import pallas as pl
from jax.experimental.pallas import tpu as pltpu

F32 = jnp.float32
BF16 = jnp.bfloat16

GRID_W = 64
HEAD_DIM = 128
HEADS_A = 6
KV_HEADS_A = 2
HEADS_B = 5
Q_RANK_B = 512
KV_RANK_B = 256
NOPE_DIM_B = 128
ROPE_DIM_B = 64
V_DIM_B = 128
HEADS_C = 5
WIN_ROWS_MAX = 8
WIN_COLS = 16
N_EXPERTS = 16
CAPACITY_FACTOR = 2
ROPE_THETA = 10000.0
EPS = 1e-6

QB_PAD = 256
WIDTH_A = HEADS_A * HEAD_DIM
WIDTH_B = HEADS_B * V_DIM_B
WIDTH_C = HEADS_C * HEAD_DIM
KV_A = KV_HEADS_A * HEAD_DIM

SEG_QA = (0, WIDTH_A)
SEG_KA = (SEG_QA[1], SEG_QA[1] + KV_A)
SEG_VA = (SEG_KA[1], SEG_KA[1] + KV_A)
SEG_CQ = (SEG_VA[1], SEG_VA[1] + Q_RANK_B)
SEG_CKV = (SEG_CQ[1], SEG_CQ[1] + KV_RANK_B)
SEG_QC = (SEG_CKV[1], SEG_CKV[1] + WIDTH_C)
SEG_KC = (SEG_QC[1], SEG_QC[1] + WIDTH_C)
SEG_VC = (SEG_KC[1], SEG_KC[1] + WIDTH_C)
SEG_KR = (SEG_VC[1], SEG_VC[1] + 128)
IN_COLS_P = SEG_KR[1]

STATE_WIDTHS = (KV_A, KV_A, KV_RANK_B, ROPE_DIM_B, WIDTH_C, WIDTH_C)

NAT_QROWS = 4
NAT_KROWS = 12
MASK_VALUE = -1e30
LOG2E = 1.4426950408889634

TOKEN_TILE = 256
PROJ_TILE = 512
ATTN_ROWS = 512
ATTN_GROUPS = 2
NAT_GROUPS = 16
GATHER_CHUNK = 512
GATHER_ROWS = 128
SCATTER_TOKENS = 128
ROUTE_TOKENS = 4096
SCATTER_YE_BYTES = 16 * 1024 * 1024
FF_SLICE = 512
VMEM_LIMIT = 56 * 1024 * 1024
NT_DIMS = (((1,), (1,)), ((), ()))


def _params(*sem):
    return pltpu.CompilerParams(dimension_semantics=sem, vmem_limit_bytes=VMEM_LIMIT)


def _resident(block_shape, index_map):
    return pl.BlockSpec(block_shape, index_map, pipeline_mode=pl.Buffered(1))


def _rms(x, g):
    ms = jnp.mean(x * x, axis=-1, keepdims=True)
    return x * lax.rsqrt(ms + EPS) * g


def _swap_pairs(x):
    lane = lax.broadcasted_iota(jnp.int32, x.shape, x.ndim - 1)
    nxt = pltpu.roll(x, x.shape[-1] - 1, x.ndim - 1)
    prv = pltpu.roll(x, 1, x.ndim - 1)
    return jnp.where((lane & 1) == 0, nxt, prv)


def _rope(x, cos, sin_signed):
    return x * cos + _swap_pairs(x) * sin_signed


def _mod_body(c_ref, w_ref, b_ref, o_ref):
    c = c_ref[...]
    s = (c / (1.0 + jnp.exp(-c))).astype(BF16)
    o_ref[...] = jnp.dot(s, w_ref[...].astype(BF16), preferred_element_type=F32) + b_ref[...]


def _modulation(cond8, w_ada, b_ada):
    depth, d, cols = w_ada.shape
    tn = 1024
    return pl.pallas_call(
        _mod_body,
        grid=(depth, cols // tn),
        in_specs=[
            pl.BlockSpec((8, d), lambda l, j: (0, 0)),
            pl.BlockSpec((None, d, tn), lambda l, j: (l, 0, j)),
            pl.BlockSpec((None, 1, tn), lambda l, j: (l, 0, j)),
        ],
        out_specs=pl.BlockSpec((None, 8, tn), lambda l, j: (l, 0, j)),
        out_shape=jax.ShapeDtypeStruct((depth, 8, cols), F32),
        compiler_params=_params("parallel", "parallel"),
        name="modulation",
    )(cond8, w_ada, b_ada.reshape(depth, 1, cols))


def _in_body(*refs, rope, state, carried):
    it = iter(refs)
    x_ref, sh_ref, sc_ref, g_ref, win_ref = (next(it) for _ in range(5))
    qag_ref, kag_ref, qng_ref, kvng_ref = (next(it) for _ in range(4))
    wq_ref, wkk_ref, wkv_ref = (next(it) for _ in range(3))
    if rope:
        ca_ref, sa_ref, cb_ref, sb_ref = (next(it) for _ in range(4))
    for _ in range(carried):
        next(it)
    qa_o, ka_o, va_o, qb_o, kb_o, vb_o, qc_o, kc_o, vc_o = (next(it) for _ in range(9))
    if state:
        st_ka, st_va, st_ckv, st_kr, st_kc, st_vc = (next(it) for _ in range(6))

    h = _rms(x_ref[...], g_ref[...]) * (1.0 + sc_ref[...]) + sh_ref[...]
    hb = h.astype(BF16)

    def seg(bounds):
        return jnp.dot(hb, win_ref[:, bounds[0]:bounds[1]], preferred_element_type=F32)

    def rope_a(y):
        return _rope(y, ca_ref[...], sa_ref[...]) if rope else y

    def rope_b(y):
        return _rope(y, cb_ref[...], sb_ref[...]) if rope else y

    qa = seg(SEG_QA)
    for hd in range(HEADS_A):
        lo = hd * HEAD_DIM
        y = rope_a(_rms(qa[:, lo:lo + HEAD_DIM], qag_ref[...]))
        qa_o[:, lo:lo + HEAD_DIM] = (y * (HEAD_DIM ** -0.5 * LOG2E)).astype(BF16)
    ka = seg(SEG_KA)
    for hd in range(KV_HEADS_A):
        lo = hd * HEAD_DIM
        y = rope_a(_rms(ka[:, lo:lo + HEAD_DIM], kag_ref[...]))
        ka_o[:, lo:lo + HEAD_DIM] = y.astype(BF16)
        if state:
            st_ka[:, lo:lo + HEAD_DIM] = y
    va = seg(SEG_VA)
    va_o[...] = (va if state else va.T).astype(BF16)

    cq = _rms(seg(SEG_CQ), qng_ref[...]).astype(BF16)
    qb = jnp.dot(cq, wq_ref[...], preferred_element_type=F32)
    qscale = (NOPE_DIM_B + ROPE_DIM_B) ** -0.5 * LOG2E
    for hd in range(HEADS_B):
        lo = hd * QB_PAD
        qb_o[:, lo:lo + 128] = (qb[:, lo:lo + 128] * qscale).astype(BF16)
        qb_o[:, lo + 128:lo + 256] = (rope_b(qb[:, lo + 128:lo + 256]) * qscale).astype(BF16)
    ckv = _rms(seg(SEG_CKV), kvng_ref[...])
    ckvb = ckv.astype(BF16)
    kr = rope_b(seg(SEG_KR))
    krb = kr.astype(BF16)
    kn = jnp.dot(ckvb, wkk_ref[...], preferred_element_type=F32)
    for hd in range(HEADS_B):
        kb_o[:, hd * QB_PAD:hd * QB_PAD + 128] = kn[:, hd * 128:(hd + 1) * 128].astype(BF16)
        kb_o[:, hd * QB_PAD + 128:(hd + 1) * QB_PAD] = krb
    vb = jnp.dot(ckvb, wkv_ref[...], preferred_element_type=F32)
    vb_o[...] = (vb if state else vb.T).astype(BF16)

    qc_o[...] = (seg(SEG_QC) * (HEAD_DIM ** -0.5 * LOG2E)).astype(BF16)
    kc = seg(SEG_KC)
    kc_o[...] = kc.astype(BF16)
    vc = seg(SEG_VC)
    vc_o[...] = vc.astype(BF16)

    if state:
        st_va[...] = va
        st_ckv[...] = ckv
        st_kr[...] = kr[:, :ROPE_DIM_B]
        st_kc[...] = kc
        st_vc[...] = vc


def _input_side(x, mod, layer, wts, rope_tabs, states):
    state = states is not None
    b, n, d = x.shape
    tm = min(n, PROJ_TILE)
    mb = mod.shape[0]
    bsel = (lambda i: i) if mb > 1 else (lambda i: 0)
    rope = rope_tabs is not None

    def modspec(k):
        return pl.BlockSpec((None, None, 1, d), lambda bi, i: (bsel(bi), k, 0, 0))

    def vec(w):
        return pl.BlockSpec((None, 1, w), lambda bi, i: (layer, 0, 0))

    in_specs = [
        pl.BlockSpec((None, tm, d), lambda bi, i: (bi, i, 0)),
        modspec(0), modspec(1), vec(d),
        _resident((None, d, IN_COLS_P), lambda bi, i: (layer, 0, 0)),
        vec(HEAD_DIM), vec(HEAD_DIM), vec(Q_RANK_B), vec(KV_RANK_B),
        _resident((None, Q_RANK_B, HEADS_B * QB_PAD), lambda bi, i: (layer, 0, 0)),
        _resident((None, KV_RANK_B, HEADS_B * 128), lambda bi, i: (layer, 0, 0)),
        _resident((None, KV_RANK_B, HEADS_B * 128), lambda bi, i: (layer, 0, 0)),
    ]
    args = [x, mod, mod, wts["norm1_g"], wts["w_in"], wts["qa_g"], wts["ka_g"], wts["qn_g"], wts["kvn_g"],
            wts["w_q_up"], wts["w_kv_k"], wts["w_kv_v"]]
    if rope:
        in_specs += [pl.BlockSpec((tm, 128), lambda bi, i: (i, 0))] * 4
        args += list(rope_tabs)

    widths = [WIDTH_A, KV_A, KV_A, HEADS_B * QB_PAD, HEADS_B * QB_PAD, WIDTH_B, WIDTH_C, WIDTH_C, WIDTH_C]
    out_shape = [jax.ShapeDtypeStruct((b, n, w), BF16) for w in widths]
    out_specs = [pl.BlockSpec((None, tm, w), lambda bi, i: (bi, i, 0)) for w in widths]
    if not state:
        for k in (2, 5):
            out_shape[k] = jax.ShapeDtypeStruct((b, widths[k], n), BF16)
            out_specs[k] = pl.BlockSpec((None, widths[k], tm), lambda bi, i: (bi, 0, i))
    aliases = {}
    if state:
        depth = wts["w_in"].shape[0]
        for w in STATE_WIDTHS:
            out_shape.append(jax.ShapeDtypeStruct((b, depth, n, w), F32))
            out_specs.append(pl.BlockSpec((None, None, tm, w), lambda bi, i: (bi, layer, i, 0)))
        for k, buf in enumerate(states):
            aliases[len(args)] = len(widths) + k
            in_specs.append(pl.BlockSpec(memory_space=pl.ANY))
            args.append(buf)

    return pl.pallas_call(
        functools.partial(_in_body, rope=rope, state=state, carried=len(aliases)),
        grid=(b, n // tm),
        in_specs=in_specs,
        out_specs=out_specs,
        out_shape=out_shape,
        input_output_aliases=aliases,
        compiler_params=_params("parallel", "parallel"),
        name="input_side",
    )(*args)


def _attn_body(q_ref, ks_ref, vs_ref, kc_ref, vc_ref, o_ref):
    rows = min(q_ref.shape[0], ATTN_ROWS)
    groups = q_ref.shape[0] // rows

    def scores(c):
        q = q_ref[c * rows:(c + 1) * rows, :]
        s_self = lax.dot_general(ks_ref[...], q, NT_DIMS, preferred_element_type=F32)
        s_ctx = lax.dot_general(kc_ref[...], q, NT_DIMS, preferred_element_type=F32)
        m = jnp.maximum(jnp.max(s_self, axis=0, keepdims=True), jnp.max(s_ctx, axis=0, keepdims=True))
        return s_self, s_ctx, m

    def finish(c, s_self, s_ctx, m):
        p_self = jnp.exp2(s_self - m)
        p_ctx = jnp.exp2(s_ctx - m)
        l = jnp.sum(p_self, axis=0, keepdims=True) + jnp.sum(p_ctx, axis=0, keepdims=True)
        acc = jnp.dot(vs_ref[...], p_self.astype(BF16), preferred_element_type=F32)
        acc = acc + jnp.dot(vc_ref[...], p_ctx.astype(BF16), preferred_element_type=F32)
        o_ref[c * rows:(c + 1) * rows, :] = (acc / l).T.astype(o_ref.dtype)

    pending = scores(0)
    for c in range(groups):
        following = scores(c + 1) if c + 1 < groups else None
        finish(c, *pending)
        pending = following


def _ctx_attn_body(qa_ref, ka_ref, va_ref, qb_ref, kb_ref, vb_ref, qc_ref, kc_ref, vc_ref, oa_ref, ob_ref, oc_ref):
    def head(q, k, v):
        s = lax.dot_general(q, k, NT_DIMS, preferred_element_type=F32)
        p = jnp.exp2(s - jnp.max(s, axis=-1, keepdims=True))
        acc = jnp.dot(p.astype(BF16), v, preferred_element_type=F32)
        return (acc / jnp.sum(p, axis=-1, keepdims=True)).astype(BF16)

    def lanes(ref, i, w):
        return ref[:, i * w:(i + 1) * w]

    for h in range(HEADS_A):
        g = h // (HEADS_A // KV_HEADS_A)
        oa_ref[:, h * HEAD_DIM:(h + 1) * HEAD_DIM] = head(
            lanes(qa_ref, h, HEAD_DIM), lanes(ka_ref, g, HEAD_DIM), lanes(va_ref, g, HEAD_DIM))
    for h in range(HEADS_B):
        ob_ref[:, h * V_DIM_B:(h + 1) * V_DIM_B] = head(
            lanes(qb_ref, h, QB_PAD), lanes(kb_ref, h, QB_PAD), lanes(vb_ref, h, V_DIM_B))
    for h in range(HEADS_C):
        oc_ref[:, h * HEAD_DIM:(h + 1) * HEAD_DIM] = head(
            lanes(qc_ref, h, HEAD_DIM), lanes(kc_ref, h, HEAD_DIM), lanes(vc_ref, h, HEAD_DIM))


def _ctx_attention(qa, ka, va, qb, kb, vb, qc, kc, vc):
    b, n, _ = qa.shape
    args = [qa, ka, va, qb, kb, vb, qc, kc, vc]
    widths = [WIDTH_A, WIDTH_B, WIDTH_C]
    return pl.pallas_call(
        _ctx_attn_body,
        grid=(b,),
        in_specs=[pl.BlockSpec((None, n, a.shape[2]), lambda bi: (bi, 0, 0)) for a in args],
        out_specs=[pl.BlockSpec((None, n, w), lambda bi: (bi, 0, 0)) for w in widths],
        out_shape=[jax.ShapeDtypeStruct((b, n, w), BF16) for w in widths],
        compiler_params=_params("parallel"),
        name="ctx_attention",
    )(*args)


def _attention(q, ks, vs_t, kc, vc_t, layer, heads, group, dq, dv):
    b, n, _ = q.shape
    ms = ks.shape[1]
    mc = kc.shape[2]
    tq = min(n, ATTN_GROUPS * ATTN_ROWS)
    return pl.pallas_call(
        _attn_body,
        grid=(b, heads, n // tq),
        in_specs=[
            pl.BlockSpec((None, tq, dq), lambda bi, h, i: (bi, i, h)),
            pl.BlockSpec((None, ms, dq), lambda bi, h, i: (bi, 0, h // group)),
            pl.BlockSpec((None, dv, ms), lambda bi, h, i: (bi, h // group, 0)),
            pl.BlockSpec((None, None, mc, dq), lambda bi, h, i: (bi, layer, 0, h // group)),
            pl.BlockSpec((None, None, dv, mc), lambda bi, h, i: (bi, layer, h // group, 0)),
        ],
        out_specs=pl.BlockSpec((None, tq, dv), lambda bi, h, i: (bi, i, h)),
        out_shape=jax.ShapeDtypeStruct((b, n, heads * dv), BF16),
        compiler_params=_params("parallel", "parallel", "parallel"),
        name="attention",
    )(q, ks, vs_t, kc, vc_t)


def _nat_body(q_ref, ks_ref, vs_ref, kc_ref, vc_ref, bias_ref, o_ref):
    tq = NAT_QROWS * GRID_W
    nk = NAT_KROWS * GRID_W
    groups = q_ref.shape[0] // tq
    key_rows = ks_ref.shape[0] // GRID_W
    last = ks_ref.shape[0] // tq - 1

    def scores(c):
        r = pl.program_id(2) * groups + c
        kr0 = jnp.clip(NAT_QROWS * r - WIN_ROWS_MAX // 2, 0, key_rows - NAT_KROWS)
        start = pl.multiple_of(kr0 * GRID_W, tq)
        kind = jnp.where(r == 0, 0, jnp.where(r == last, 2, 1))
        q = q_ref[c * tq:(c + 1) * tq, :]
        s_win = lax.dot_general(q, ks_ref[pl.ds(start, nk), :], NT_DIMS, preferred_element_type=F32)
        s_win = s_win + bias_ref[kind]
        s_ctx = lax.dot_general(q, kc_ref[...], NT_DIMS, preferred_element_type=F32)
        m = jnp.maximum(jnp.max(s_win, axis=-1, keepdims=True), jnp.max(s_ctx, axis=-1, keepdims=True))
        return s_win, s_ctx, m, start

    def finish(c, s_win, s_ctx, m, start):
        p_win = jnp.exp2(s_win - m)
        p_ctx = jnp.exp2(s_ctx - m)
        l = jnp.sum(p_win, axis=-1, keepdims=True) + jnp.sum(p_ctx, axis=-1, keepdims=True)
        acc = jnp.dot(p_win.astype(BF16), vs_ref[pl.ds(start, nk), :], preferred_element_type=F32)
        acc = acc + jnp.dot(p_ctx.astype(BF16), vc_ref[...], preferred_element_type=F32)
        o_ref[c * tq:(c + 1) * tq, :] = (acc / l).astype(o_ref.dtype)

    pending = scores(0)
    for c in range(groups):
        following = scores(c + 1) if c + 1 < groups else None
        finish(c, *pending)
        pending = following


N_DROW = 2 * WIN_ROWS_MAX - 1
N_DCOL = 2 * WIN_COLS - 1


def _nat_bias_body(b_ref, o_ref, *, rows):
    base = (pl.program_id(0) * HEADS_C + pl.program_id(1)) * (N_DROW * N_DCOL)
    c = lax.broadcasted_iota(jnp.int32, (GRID_W, GRID_W), 0)
    kc = lax.broadcasted_iota(jnp.int32, (GRID_W, GRID_W), 1)
    c0 = jnp.clip(c - WIN_COLS // 2, 0, GRID_W - WIN_COLS)
    in_window = (kc >= c0) & (kc < c0 + WIN_COLS)
    dc = kc - c + (WIN_COLS - 1)
    masked = jnp.full((GRID_W, GRID_W), MASK_VALUE, F32)
    tables = {}

    def table(dr):
        if dr not in tables:
            t = jnp.zeros((GRID_W, GRID_W), F32)
            for j in range(N_DCOL):
                t = jnp.where(dc == j, b_ref[base + dr * N_DCOL + j] * LOG2E, t)
            tables[dr] = jnp.where(in_window, t, MASK_VALUE)
        return tables[dr]

    kh = min(WIN_ROWS_MAX, rows)
    for ty, blk in enumerate((0, 1, rows // NAT_QROWS - 1)):
        r_first = NAT_QROWS * blk
        kr0 = min(max(r_first - WIN_ROWS_MAX // 2, 0), rows - NAT_KROWS)
        for a in range(NAT_QROWS):
            r = r_first + a
            r0 = min(max(r - kh // 2, 0), rows - kh)
            for i in range(NAT_KROWS):
                kr = kr0 + i
                blkval = table(kr - r + WIN_ROWS_MAX - 1) if r0 <= kr < r0 + kh else masked
                o_ref[ty, a * GRID_W:(a + 1) * GRID_W, i * GRID_W:(i + 1) * GRID_W] = blkval


def _nat_bias(na_bias, rows):
    depth = na_bias.shape[0]
    tq, nk = NAT_QROWS * GRID_W, NAT_KROWS * GRID_W
    return pl.pallas_call(
        functools.partial(_nat_bias_body, rows=rows),
        grid=(depth, HEADS_C),
        in_specs=[pl.BlockSpec(memory_space=pltpu.SMEM)],
        out_specs=pl.BlockSpec((None, None, 3, tq, nk), lambda l, h: (l, h, 0, 0, 0)),
        out_shape=jax.ShapeDtypeStruct((depth, HEADS_C, 3, tq, nk), F32),
        compiler_params=_params("parallel", "parallel"),
        name="nat_bias",
    )(na_bias.reshape(-1))


def _neighbourhood_attention(q, ks, vs, kc, vc, bias, layer):
    b, n, _ = q.shape
    nblk = n // (NAT_QROWS * GRID_W)
    groups = min(nblk, NAT_GROUPS)
    tq = groups * NAT_QROWS * GRID_W
    mc = kc.shape[2]
    d = HEAD_DIM

    return pl.pallas_call(
        _nat_body,
        grid=(b, HEADS_C, nblk // groups),
        in_specs=[
            pl.BlockSpec((None, tq, d), lambda bi, h, r: (bi, r, h)),
            pl.BlockSpec((None, n, d), lambda bi, h, r: (bi, 0, h)),
            pl.BlockSpec((None, n, d), lambda bi, h, r: (bi, 0, h)),
            pl.BlockSpec((None, None, mc, d), lambda bi, h, r: (bi, layer, 0, h)),
            pl.BlockSpec((None, None, mc, d), lambda bi, h, r: (bi, layer, 0, h)),
            pl.BlockSpec((None, None, 3, NAT_QROWS * GRID_W, NAT_KROWS * GRID_W),
                         lambda bi, h, r: (layer, h, 0, 0, 0)),
        ],
        out_specs=pl.BlockSpec((None, tq, d), lambda bi, h, r: (bi, r, h)),
        out_shape=jax.ShapeDtypeStruct((b, n, HEADS_C * d), BF16),
        compiler_params=_params("parallel", "parallel", "arbitrary"),
        name="neighbourhood_attention",
    )(q, ks, vs, kc, vc, bias)


def _cache_kv_body(ckv_ref, kr_ref, wkk_ref, wkv_ref, kb_o, vb_o):
    ckvb = ckv_ref[...].astype(BF16)
    krb = kr_ref[...].astype(BF16)
    kn = jnp.dot(ckvb, wkk_ref[...], preferred_element_type=F32)
    zeros = jnp.zeros((krb.shape[0], QB_PAD - 128 - ROPE_DIM_B), BF16)
    for hd in range(HEADS_B):
        lo = hd * QB_PAD
        kb_o[:, lo:lo + 128] = kn[:, hd * 128:(hd + 1) * 128].astype(BF16)
        kb_o[:, lo + 128:lo + 128 + ROPE_DIM_B] = krb
        kb_o[:, lo + 128 + ROPE_DIM_B:lo + QB_PAD] = zeros
    vb_o[...] = jnp.dot(ckvb, wkv_ref[...], preferred_element_type=F32).T.astype(BF16)


def _cache_mla_kv(cache_ckv, cache_krope, w_kv_k, w_kv_v):
    b, depth, m, _ = cache_ckv.shape
    return pl.pallas_call(
        _cache_kv_body,
        grid=(b, depth),
        in_specs=[
            pl.BlockSpec((None, None, m, KV_RANK_B), lambda bi, l: (bi, l, 0, 0)),
            pl.BlockSpec((None, None, m, ROPE_DIM_B), lambda bi, l: (bi, l, 0, 0)),
            pl.BlockSpec((None, KV_RANK_B, HEADS_B * 128), lambda bi, l: (l, 0, 0)),
            pl.BlockSpec((None, KV_RANK_B, HEADS_B * 128), lambda bi, l: (l, 0, 0)),
        ],
        out_specs=[
            pl.BlockSpec((None, None, m, HEADS_B * QB_PAD), lambda bi, l: (bi, l, 0, 0)),
            pl.BlockSpec((None, None, WIDTH_B, m), lambda bi, l: (bi, l, 0, 0)),
        ],
        out_shape=[
            jax.ShapeDtypeStruct((b, depth, m, HEADS_B * QB_PAD), BF16),
            jax.ShapeDtypeStruct((b, depth, WIDTH_B, m), BF16),
        ],
        compiler_params=_params("parallel", "parallel"),
        name="cache_mla_kv",
    )(cache_ckv, cache_krope, w_kv_k, w_kv_v)


def _out_body(oa_ref, ob_ref, oc_ref, wo_ref, x_ref, gate_ref, sh_ref, sc_ref, g_ref, wr_ref, x1_o, h2_o, lg_o):
    lo_b, lo_c = WIDTH_A, WIDTH_A + WIDTH_B
    o = jnp.dot(oa_ref[...], wo_ref[:lo_b, :], preferred_element_type=F32)
    o = o + jnp.dot(ob_ref[...], wo_ref[lo_b:lo_c, :], preferred_element_type=F32)
    o = o + jnp.dot(oc_ref[...], wo_ref[lo_c:, :], preferred_element_type=F32)
    x1 = x_ref[...] + gate_ref[...] * o
    x1_o[...] = x1
    h = _rms(x1, g_ref[...]) * (1.0 + sc_ref[...]) + sh_ref[...]
    hb = h.astype(BF16)
    h2_o[...] = hb
    lg = jnp.dot(hb, wr_ref[...], preferred_element_type=F32)
    for c in range(lg.shape[0] // 128):
        lg_o[:, c * 128:(c + 1) * 128] = lg[c * 128:(c + 1) * 128, :].T[:N_EXPERTS, :]


def _output_side(o_a, o_b, o_c, x, mod, layer, wts):
    b, n, d = x.shape
    tm = TOKEN_TILE
    mb = mod.shape[0]
    bsel = (lambda i: i) if mb > 1 else (lambda i: 0)

    def modspec(k):
        return pl.BlockSpec((None, None, 1, d), lambda bi, i: (bsel(bi), k, 0, 0))

    def tok(w):
        return pl.BlockSpec((None, tm, w), lambda bi, i: (bi, i, 0))

    return pl.pallas_call(
        _out_body,
        grid=(b, n // tm),
        in_specs=[
            tok(WIDTH_A), tok(WIDTH_B), tok(WIDTH_C),
            _resident((None, WIDTH_A + WIDTH_B + WIDTH_C, d), lambda bi, i: (layer, 0, 0)),
            tok(d), modspec(2), modspec(3), modspec(4),
            pl.BlockSpec((None, 1, d), lambda bi, i: (layer, 0, 0)),
            pl.BlockSpec((None, d, 128), lambda bi, i: (layer, 0, 0)),
        ],
        out_specs=[tok(d), tok(d), pl.BlockSpec((None, N_EXPERTS, tm), lambda bi, i: (bi, 0, i))],
        out_shape=[
            jax.ShapeDtypeStruct((b, n, d), F32),
            jax.ShapeDtypeStruct((b, n, d), BF16),
            jax.ShapeDtypeStruct((b, N_EXPERTS, n), F32),
        ],
        compiler_params=_params("parallel", "parallel"),
        name="output_side",
    )(o_a, o_b, o_c, wts["w_out"], x, mod, mod, mod, wts["norm2_g"], wts["w_router"])


def _prefix_exclusive(mask):
    e, n = mask.shape
    ones = jnp.where(mask, 1.0, 0.0)
    rr = lax.broadcasted_iota(jnp.int32, (128, 128), 0)
    cc = lax.broadcasted_iota(jnp.int32, (128, 128), 1)
    tri = jnp.where(rr <= cc, 1.0, 0.0).astype(BF16)
    carry = jnp.zeros((e, 1), F32)
    outs = []
    for c in range(n // 128):
        blk = ones[:, c * 128:(c + 1) * 128]
        inc = jnp.dot(blk.astype(BF16), tri, preferred_element_type=F32)
        outs.append(inc - blk + carry)
        carry = carry + inc[:, 127:128]
    return jnp.concatenate(outs, axis=1)


def _router_body(lg_ref, rank_o, rank_t_o, aff_t_o, cum_o, *, cap, chunk):
    bb, e, n = lg_ref.shape
    lg = lg_ref[...]
    ex = jnp.exp(lg - jnp.max(lg, axis=1, keepdims=True))
    aff = (ex / jnp.sum(ex, axis=1, keepdims=True)).reshape(bb * e, n)
    key = pltpu.bitcast(aff, jnp.int32)

    def step(i, t):
        cand = t | lax.shift_left(jnp.int32(1), 30 - i)
        cnt = jnp.sum(jnp.where(key >= cand, 1.0, 0.0), axis=1, keepdims=True)
        return jnp.where(cnt >= cap, cand, t)

    thr = lax.fori_loop(0, 31, step, jnp.zeros((bb * e, 1), jnp.int32))
    above = key > thr
    tied = key == thr
    need = cap - jnp.sum(jnp.where(above, 1.0, 0.0), axis=1, keepdims=True)
    chosen = above | (tied & (_prefix_exclusive(tied) < need))
    before = _prefix_exclusive(chosen)
    rank = jnp.where(chosen, before, -1.0)
    rank_o[...] = rank.astype(jnp.int32).reshape(bb, e, n)

    lane = lax.broadcasted_iota(jnp.int32, (bb * e, 128), 1)
    cum = jnp.full((bb * e, 128), float(cap), F32)
    for k in range(n // chunk):
        cum = jnp.where(lane == k, before[:, k * chunk:k * chunk + 1], cum)
    cum_o[...] = cum.astype(jnp.int32).reshape(bb, e, 128)

    fill = jnp.full((128 - e, n), -1.0, F32)
    for bi in range(bb):
        rank_p = jnp.concatenate([rank[bi * e:(bi + 1) * e], fill], axis=0)
        aff_p = jnp.concatenate([aff[bi * e:(bi + 1) * e], fill], axis=0)
        for c in range(n // 128):
            rank_t_o[bi, c * 128:(c + 1) * 128, :] = rank_p[:, c * 128:(c + 1) * 128].T
            aff_t_o[bi, c * 128:(c + 1) * 128, :] = aff_p[:, c * 128:(c + 1) * 128].T


def _route(logits_t):
    b, e, n = logits_t.shape
    cap = CAPACITY_FACTOR * n // e
    bb = max(1, min(b, ROUTE_TOKENS // n))
    return pl.pallas_call(
        functools.partial(_router_body, cap=cap, chunk=SCATTER_TOKENS),
        grid=(b // bb,),
        in_specs=[pl.BlockSpec((bb, e, n), lambda bi: (bi, 0, 0))],
        out_specs=[
            pl.BlockSpec((bb, e, n), lambda bi: (bi, 0, 0)),
            pl.BlockSpec((bb, n, 128), lambda bi: (bi, 0, 0)),
            pl.BlockSpec((bb, n, 128), lambda bi: (bi, 0, 0)),
            pl.BlockSpec((bb, e, 128), lambda bi: (bi, 0, 0)),
        ],
        out_shape=[
            jax.ShapeDtypeStruct((b, e, n), jnp.int32),
            jax.ShapeDtypeStruct((b, n, 128), F32),
            jax.ShapeDtypeStruct((b, n, 128), F32),
            jax.ShapeDtypeStruct((b, e, 128), jnp.int32),
        ],
        compiler_params=_params("parallel"),
        name="route",
    )(logits_t)


def _gather_body(cum_ref, h_ref, rank_ref, xe_o, *, chunk, rows):
    n, d = h_ref.shape
    experts, cap, _ = xe_o.shape
    nch = n // chunk
    per_chunk = chunk // SCATTER_TOKENS
    first = (pl.program_id(0) * pl.num_programs(1) + pl.program_id(1)) * experts
    for k in range(experts):
        rank = rank_ref[k]
        if nch == 1 and cap == rows:
            slot = lax.broadcasted_iota(jnp.int32, (rows, chunk), 0)
            onehot = jnp.where(slot == rank, 1.0, 0.0).astype(BF16)
            xe_o[k] = jnp.dot(onehot, h_ref[...], preferred_element_type=F32).astype(xe_o.dtype)
            continue
        base = (first + k) * (n // SCATTER_TOKENS + 1)
        xe_o[k] = jnp.zeros((cap, d), xe_o.dtype)
        for c in range(nch):
            lo = cum_ref[base + c * per_chunk]
            hi = cum_ref[base + (c + 1) * per_chunk]
            rank_c = rank[:, c * chunk:(c + 1) * chunk]
            for jb in range(cap // rows):

                @pl.when((lo < (jb + 1) * rows) & (hi > jb * rows))
                def _():
                    slot = lax.broadcasted_iota(jnp.int32, (rows, chunk), 0) + jb * rows
                    onehot = jnp.where(slot == rank_c, 1.0, 0.0).astype(BF16)
                    picked = jnp.dot(onehot, h_ref[c * chunk:(c + 1) * chunk, :], preferred_element_type=F32)
                    xe_o[k, jb * rows:(jb + 1) * rows, :] += picked.astype(xe_o.dtype)


def _gather(h2, rank, cum_flat):
    b, n, d = h2.shape
    e = rank.shape[1]
    cap = CAPACITY_FACTOR * n // e
    chunk = min(n, GATHER_CHUNK)
    per_step = e if n <= chunk else 1
    return pl.pallas_call(
        functools.partial(_gather_body, chunk=chunk, rows=min(cap, GATHER_ROWS)),
        grid_spec=pltpu.PrefetchScalarGridSpec(
            num_scalar_prefetch=1,
            grid=(b, e // per_step),
            in_specs=[
                pl.BlockSpec((None, n, d), lambda bi, ei, cum_ref: (bi, 0, 0)),
                pl.BlockSpec((None, per_step, 1, n), lambda bi, ei, cum_ref: (bi, ei, 0, 0)),
            ],
            out_specs=pl.BlockSpec((per_step, None, cap, d), lambda bi, ei, cum_ref: (ei, bi, 0, 0)),
        ),
        out_shape=jax.ShapeDtypeStruct((e, b, cap, d), BF16),
        compiler_params=_params("parallel", "parallel"),
        name="gather",
    )(cum_flat, h2, rank)


def _ffn_body(xe_ref, wg_ref, wu_ref, wd_ref, ye_o):
    bb, cap, d = xe_ref.shape
    xe = xe_ref[...].reshape(bb * cap, d)
    g = jnp.dot(xe, wg_ref[...], preferred_element_type=F32)
    u = jnp.dot(xe, wu_ref[...], preferred_element_type=F32)
    hid = (g / (1.0 + jnp.exp(-g)) * u).astype(BF16)
    ye = jnp.dot(hid, wd_ref[...], preferred_element_type=F32)
    ye_o[...] = ye.astype(ye_o.dtype).reshape(bb, cap, d)


def _expert_ffn(xe, w_gate, w_up, w_down):
    e, b, cap, d = xe.shape
    ff = w_gate.shape[-1]
    bb = max(1, min(b, 512 // cap))
    return pl.pallas_call(
        _ffn_body,
        grid=(e, b // bb),
        in_specs=[
            pl.BlockSpec((None, bb, cap, d), lambda ei, bi: (ei, bi, 0, 0)),
            pl.BlockSpec((None, d, ff), lambda ei, bi: (ei, 0, 0)),
            pl.BlockSpec((None, d, ff), lambda ei, bi: (ei, 0, 0)),
            pl.BlockSpec((None, ff, d), lambda ei, bi: (ei, 0, 0)),
        ],
        out_specs=pl.BlockSpec((None, bb, cap, d), lambda ei, bi: (ei, bi, 0, 0)),
        out_shape=jax.ShapeDtypeStruct((e, b, cap, d), BF16),
        compiler_params=_params("parallel", "arbitrary"),
        name="expert_ffn",
    )(xe, w_gate, w_up, w_down)


def _ffn_cast_body(xe_ref, wg_ref, wu_ref, wd_ref, ye_o, wg_o, wu_o, wd_o, acc):
    f = pl.program_id(1)
    bb, cap, d = xe_ref.shape
    wg = wg_ref[...].astype(BF16)
    wu = wu_ref[...].astype(BF16)
    wd = wd_ref[...].astype(BF16)
    wg_o[...] = wg
    wu_o[...] = wu
    wd_o[...] = wd
    xe = xe_ref[...].reshape(bb * cap, d)
    g = jnp.dot(xe, wg, preferred_element_type=F32)
    u = jnp.dot(xe, wu, preferred_element_type=F32)
    hid = (g / (1.0 + jnp.exp(-g)) * u).astype(BF16)
    part = jnp.dot(hid, wd, preferred_element_type=F32)

    @pl.when(f == 0)
    def _():
        acc[...] = part

    @pl.when(f > 0)
    def _():
        acc[...] += part

    @pl.when(f == pl.num_programs(1) - 1)
    def _():
        ye_o[...] = acc[...].astype(ye_o.dtype).reshape(bb, cap, d)


def _expert_ffn_cast(xe, layer, w_gate, w_up, w_down):
    e, b, cap, d = xe.shape
    ff = w_gate.shape[-1]
    ffs = FF_SLICE
    return pl.pallas_call(
        _ffn_cast_body,
        grid=(e, ff // ffs),
        in_specs=[
            pl.BlockSpec((None, b, cap, d), lambda ei, f: (ei, 0, 0, 0)),
            pl.BlockSpec((None, None, d, ffs), lambda ei, f: (layer, ei, 0, f)),
            pl.BlockSpec((None, None, d, ffs), lambda ei, f: (layer, ei, 0, f)),
            pl.BlockSpec((None, None, ffs, d), lambda ei, f: (layer, ei, f, 0)),
        ],
        out_specs=[
            pl.BlockSpec((None, b, cap, d), lambda ei, f: (ei, 0, 0, 0)),
            pl.BlockSpec((None, d, ffs), lambda ei, f: (ei, 0, f)),
            pl.BlockSpec((None, d, ffs), lambda ei, f: (ei, 0, f)),
            pl.BlockSpec((None, ffs, d), lambda ei, f: (ei, f, 0)),
        ],
        out_shape=[
            jax.ShapeDtypeStruct((e, b, cap, d), BF16),
            jax.ShapeDtypeStruct((e, d, ff), BF16),
            jax.ShapeDtypeStruct((e, d, ff), BF16),
            jax.ShapeDtypeStruct((e, ff, d), BF16),
        ],
        scratch_shapes=[pltpu.VMEM((b * cap, d), F32)],
        compiler_params=_params("parallel", "arbitrary"),
        name="expert_ffn_cast",
    )(xe, w_gate, w_up, w_down)


def _scatter_body(cum_ref, ye_ref, x_ref, rank_ref, aff_ref, gate_ref, o_ref):
    e, cap, dh = ye_ref.shape
    tn = x_ref.shape[0]
    sub_tiles = tn // SCATTER_TOKENS
    win = min(cap, 2 * SCATTER_TOKENS)
    per_expert = pl.num_programs(2) * sub_tiles + 1
    col = lax.broadcasted_iota(jnp.int32, (SCATTER_TOKENS, win), 1).astype(F32)
    for sub in range(sub_tiles):
        rows = slice(sub * SCATTER_TOKENS, (sub + 1) * SCATTER_TOKENS)
        acc = jnp.zeros((SCATTER_TOKENS, dh), F32)
        for ei in range(e):
            rank = rank_ref[rows, ei:ei + 1]
            if cap > win:
                first = cum_ref[(pl.program_id(0) * e + ei) * per_expert + pl.program_id(2) * sub_tiles + sub]
                start = jnp.clip((first // SCATTER_TOKENS) * SCATTER_TOKENS, 0, cap - win)
                start = pl.multiple_of(start, SCATTER_TOKENS)
                onehot = jnp.where(rank - start.astype(F32) == col, 1.0, 0.0).astype(BF16)
                contrib = jnp.dot(onehot, ye_ref[ei, pl.ds(start, win), :], preferred_element_type=F32)
            else:
                onehot = jnp.where(rank == col, 1.0, 0.0).astype(BF16)
                contrib = jnp.dot(onehot, ye_ref[ei], preferred_element_type=F32)
            acc = acc + contrib * aff_ref[rows, ei:ei + 1]
        o_ref[rows, :] = x_ref[rows, :] + gate_ref[...] * acc


def _scatter(ye, x1, rank_t, aff_t, mod, cum_flat):
    e, b, cap, d = ye.shape
    n = x1.shape[1]
    tn = min(n, 512)
    dh = d if e * cap * d * 2 <= SCATTER_YE_BYTES else d // 2
    mb = mod.shape[0]
    bsel = (lambda i: i) if mb > 1 else (lambda i: 0)
    return pl.pallas_call(
        _scatter_body,
        grid_spec=pltpu.PrefetchScalarGridSpec(
            num_scalar_prefetch=1,
            grid=(b, d // dh, n // tn),
            in_specs=[
                pl.BlockSpec((e, None, cap, dh), lambda bi, j, i, cum_ref: (0, bi, 0, j)),
                pl.BlockSpec((None, tn, dh), lambda bi, j, i, cum_ref: (bi, i, j)),
                pl.BlockSpec((None, tn, 128), lambda bi, j, i, cum_ref: (bi, i, 0)),
                pl.BlockSpec((None, tn, 128), lambda bi, j, i, cum_ref: (bi, i, 0)),
                pl.BlockSpec((None, None, 1, dh), lambda bi, j, i, cum_ref: (bsel(bi), 5, 0, j)),
            ],
            out_specs=pl.BlockSpec((None, tn, dh), lambda bi, j, i, cum_ref: (bi, i, j)),
        ),
        out_shape=jax.ShapeDtypeStruct(x1.shape, F32),
        compiler_params=_params("parallel", "parallel", "parallel"),
        name="scatter",
    )(cum_flat, ye, x1, rank_t, aff_t, mod)


def _final_norm_body(x_ref, g_ref, o_ref):
    o_ref[...] = _rms(x_ref[...], g_ref[...])


def _final_norm(x, g):
    b, n, d = x.shape
    tm = TOKEN_TILE
    return pl.pallas_call(
        _final_norm_body,
        grid=(b, n // tm),
        in_specs=[pl.BlockSpec((None, tm, d), lambda bi, i: (bi, i, 0)), pl.BlockSpec((1, d), lambda bi, i: (0, 0))],
        out_specs=pl.BlockSpec((None, tm, d), lambda bi, i: (bi, i, 0)),
        out_shape=jax.ShapeDtypeStruct(x.shape, F32),
        compiler_params=_params("parallel", "parallel"),
        name="final_norm",
    )(x, g.reshape(1, d))


def _rope_tables(n_tokens, rot_dim):
    t = jnp.arange(n_tokens, dtype=jnp.int32)
    row = (t // GRID_W).astype(F32)
    col = (t % GRID_W).astype(F32)
    axis_dim = rot_dim // 2
    freqs = ROPE_THETA ** (-jnp.arange(0, axis_dim, 2, dtype=F32) / axis_dim)
    ang = jnp.concatenate([row[:, None] * freqs[None, :], col[:, None] * freqs[None, :]], axis=-1)
    cos, sin = jnp.cos(ang), jnp.sin(ang)
    cos2 = jnp.repeat(cos, 2, axis=-1)
    sin2 = jnp.stack([-sin, sin], axis=-1).reshape(n_tokens, rot_dim)
    pad = 128 - rot_dim
    if pad:
        cos2 = jnp.concatenate([cos2, jnp.ones((n_tokens, pad), F32)], axis=-1)
        sin2 = jnp.concatenate([sin2, jnp.zeros((n_tokens, pad), F32)], axis=-1)
    return cos2, sin2


def _prepare_weights(norm1_g, norm2_g, w_in, qa_norm_g, ka_norm_g, q_norm_b, kv_norm_b, w_q_up, w_kv_up, w_out,
                     w_router):
    depth, d, _ = w_in.shape
    kr_lo = SEG_CKV[1]
    w_in16 = w_in.astype(BF16)
    w_in_p = jnp.concatenate(
        [w_in16[:, :, :kr_lo], w_in16[:, :, kr_lo + ROPE_DIM_B:], w_in16[:, :, kr_lo:kr_lo + ROPE_DIM_B],
         jnp.zeros((depth, d, 128 - ROPE_DIM_B), BF16)], axis=-1)
    wq = w_q_up.reshape(depth, Q_RANK_B, HEADS_B, NOPE_DIM_B + ROPE_DIM_B)
    wq = jnp.pad(wq, ((0, 0), (0, 0), (0, 0), (0, QB_PAD - NOPE_DIM_B - ROPE_DIM_B)))
    wkv = w_kv_up.reshape(depth, KV_RANK_B, HEADS_B, NOPE_DIM_B + V_DIM_B)
    return {
        "norm1_g": norm1_g.reshape(depth, 1, d),
        "norm2_g": norm2_g.reshape(depth, 1, d),
        "w_in": w_in_p,
        "qa_g": qa_norm_g.reshape(depth, 1, HEAD_DIM),
        "ka_g": ka_norm_g.reshape(depth, 1, HEAD_DIM),
        "qn_g": q_norm_b.reshape(depth, 1, Q_RANK_B),
        "kvn_g": kv_norm_b.reshape(depth, 1, KV_RANK_B),
        "w_q_up": wq.reshape(depth, Q_RANK_B, HEADS_B * QB_PAD).astype(BF16),
        "w_kv_k": wkv[..., :NOPE_DIM_B].reshape(depth, KV_RANK_B, HEADS_B * NOPE_DIM_B).astype(BF16),
        "w_kv_v": wkv[..., NOPE_DIM_B:].reshape(depth, KV_RANK_B, HEADS_B * V_DIM_B).astype(BF16),
        "w_out": w_out.astype(BF16),
        "w_router": jnp.pad(w_router, ((0, 0), (0, 0), (0, 128 - N_EXPERTS))).astype(BF16),
    }


def _moe_select(h2, logits_t):
    rank, rank_t, aff_t, cum = _route(logits_t)
    n = rank.shape[2]
    cum_flat = cum[:, :, :n // SCATTER_TOKENS + 1].reshape(-1)
    xe = _gather(h2, rank.reshape(rank.shape[0], rank.shape[1], 1, n), cum_flat)
    return xe, rank_t, aff_t, cum_flat


def kernel(x_prompt, x_sample, cache_a_k, cache_a_v, cache_b_ckv, cache_b_krope, cache_c_k, cache_c_v, c, c_ctx, w_ada, b_ada, norm1_g, norm2_g, w_in, qa_norm_g, ka_norm_g, q_norm_b, kv_norm_b, w_q_up, w_kv_up, na_bias, w_out, w_router, w_gate, w_up, w_down, final_norm_g):
    depth, d, _ = w_in.shape
    bp, seq, _ = x_prompt.shape
    bs, n_lat, _ = x_sample.shape
    past = cache_a_k.shape[2]
    rows = n_lat // GRID_W

    wts = _prepare_weights(norm1_g, norm2_g, w_in, qa_norm_g, ka_norm_g, q_norm_b, kv_norm_b, w_q_up, w_kv_up,
                           w_out, w_router)
    rope_tabs = _rope_tables(n_lat, HEAD_DIM) + _rope_tables(n_lat, ROPE_DIM_B)

    cond8 = jnp.concatenate([c_ctx[None], c, jnp.zeros((8 - 1 - bs, d), F32)], axis=0)
    mods = _modulation(cond8, w_ada, b_ada)
    mods = mods.reshape(depth, 8, 6, 1, d)

    ca_k = cache_a_k.reshape(bs, depth, past, KV_A).astype(BF16)
    ca_v = jnp.swapaxes(cache_a_v.reshape(bs, depth, past, KV_A), 2, 3).astype(BF16)
    cc_k = cache_c_k.reshape(bs, depth, past, WIDTH_C).astype(BF16)
    cc_v = cache_c_v.reshape(bs, depth, past, WIDTH_C).astype(BF16)
    cb_k, cb_v = _cache_mla_kv(cache_b_ckv, cache_b_krope, wts["w_kv_k"], wts["w_kv_v"])
    nat_bias = _nat_bias(na_bias, rows)

    xp, xs = x_prompt, x_sample
    states = ()
    for l in range(depth):
        mod_c = mods[l, 0:1]
        mod_l = mods[l, 1:1 + bs]

        qa, ka, va, qb, kb, vb, qc, kc, vc, *states = _input_side(xp, mod_c, l, wts, None, states)
        o_a, o_b, o_c = _ctx_attention(qa, ka, va, qb, kb, vb, qc, kc, vc)
        x1, h2, lg = _output_side(o_a, o_b, o_c, xp, mod_c, l, wts)
        xe, rank_t, aff_t, cum_flat = _moe_select(h2, lg)
        ye, wg16, wu16, wd16 = _expert_ffn_cast(xe, l, w_gate, w_up, w_down)
        xp = _scatter(ye, x1, rank_t, aff_t, mod_c, cum_flat)

        qa, ka, va, qb, kb, vb, qc, kc, vc = _input_side(xs, mod_l, l, wts, rope_tabs, None)
        o_a = _attention(qa, ka, va, ca_k, ca_v, l, HEADS_A, HEADS_A // KV_HEADS_A, HEAD_DIM, HEAD_DIM)
        o_b = _attention(qb, kb, vb, cb_k, cb_v, l, HEADS_B, 1, QB_PAD, V_DIM_B)
        o_c = _neighbourhood_attention(qc, kc, vc, cc_k, cc_v, nat_bias, l)
        x1, h2, lg = _output_side(o_a, o_b, o_c, xs, mod_l, l, wts)
        xe, rank_t, aff_t, cum_flat = _moe_select(h2, lg)
        ye = _expert_ffn(xe, wg16, wu16, wd16)
        xs = _scatter(ye, x1, rank_t, aff_t, mod_l, cum_flat)

    y_prompt = _final_norm(xp, final_norm_g)
    y_sample = _final_norm(xs, final_norm_g)
    st_ka, st_va, new_b_ckv, new_b_krope, st_kc, st_vc = states
    new_a_k = st_ka.reshape(bp, depth, seq, KV_HEADS_A, HEAD_DIM)
    new_a_v = st_va.reshape(bp, depth, seq, KV_HEADS_A, HEAD_DIM)
    new_c_k = st_kc.reshape(bp, depth, seq, HEADS_C, HEAD_DIM)
    new_c_v = st_vc.reshape(bp, depth, seq, HEADS_C, HEAD_DIM)
    return (y_prompt, y_sample, new_a_k, new_a_v, new_b_ckv, new_b_krope, new_c_k, new_c_v)
```

```python
import functools

import jax
import jax.numpy as jnp
from jax import lax
from jax.experimental import pallas as pl
from jax.experimental.pallas import tpu as pltpu

F32 = jnp.float32
BF16 = jnp.bfloat16

GRID_W = 64
HEAD_DIM = 128
HEADS_A = 6
KV_HEADS_A = 2
HEADS_B = 5
Q_RANK_B = 512
KV_RANK_B = 256
NOPE_DIM_B = 128
ROPE_DIM_B = 64
V_DIM_B = 128
HEADS_C = 5
WIN_ROWS_MAX = 8
WIN_COLS = 16
N_EXPERTS = 16
CAPACITY_FACTOR = 2
ROPE_THETA = 10000.0
EPS = 1e-6

QB_PAD = 256
WIDTH_A = HEADS_A * HEAD_DIM
WIDTH_B = HEADS_B * V_DIM_B
WIDTH_C = HEADS_C * HEAD_DIM
KV_A = KV_HEADS_A * HEAD_DIM

SEG_QA = (0, WIDTH_A)
SEG_KA = (SEG_QA[1], SEG_QA[1] + KV_A)
SEG_VA = (SEG_KA[1], SEG_KA[1] + KV_A)
SEG_CQ = (SEG_VA[1], SEG_VA[1] + Q_RANK_B)
SEG_CKV = (SEG_CQ[1], SEG_CQ[1] + KV_RANK_B)
SEG_QC = (SEG_CKV[1], SEG_CKV[1] + WIDTH_C)
SEG_KC = (SEG_QC[1], SEG_QC[1] + WIDTH_C)
SEG_VC = (SEG_KC[1], SEG_KC[1] + WIDTH_C)
SEG_KR = (SEG_VC[1], SEG_VC[1] + 128)
IN_COLS_P = SEG_KR[1]

STATE_WIDTHS = (KV_A, KV_A, KV_RANK_B, ROPE_DIM_B, WIDTH_C, WIDTH_C)

NAT_QROWS = 4
NAT_KROWS = 12
MASK_VALUE = -1e30
LOG2E = 1.4426950408889634

TOKEN_TILE = 256
PROJ_TILE = 512
ATTN_ROWS = 512
ATTN_GROUPS = 4
NAT_GROUPS = 16
GATHER_CHUNK = 512
GATHER_ROWS = 128
SCATTER_TOKENS = 128
ROUTE_TOKENS = 4096
SCATTER_YE_BYTES = 16 * 1024 * 1024
FF_SLICE = 512
VMEM_LIMIT = 56 * 1024 * 1024
NT_DIMS = (((1,), (1,)), ((), ()))


def _params(*sem):
    return pltpu.CompilerParams(dimension_semantics=sem, vmem_limit_bytes=VMEM_LIMIT)


def _resident(block_shape, index_map):
    return pl.BlockSpec(block_shape, index_map, pipeline_mode=pl.Buffered(1))


def _rms(x, g):
    ms = jnp.mean(x * x, axis=-1, keepdims=True)
    return x * lax.rsqrt(ms + EPS) * g


def _swap_pairs(x):
    lane = lax.broadcasted_iota(jnp.int32, x.shape, x.ndim - 1)
    nxt = pltpu.roll(x, x.shape[-1] - 1, x.ndim - 1)
    prv = pltpu.roll(x, 1, x.ndim - 1)
    return jnp.where((lane & 1) == 0, nxt, prv)


def _rope(x, cos, sin_signed):
    return x * cos + _swap_pairs(x) * sin_signed


def _mod_body(c_ref, w_ref, b_ref, o_ref):
    c = c_ref[...]
    s = (c / (1.0 + jnp.exp(-c))).astype(BF16)
    o_ref[...] = jnp.dot(s, w_ref[...].astype(BF16), preferred_element_type=F32) + b_ref[...]


def _modulation(cond8, w_ada, b_ada):
    depth, d, cols = w_ada.shape
    tn = 1024
    return pl.pallas_call(
        _mod_body,
        grid=(depth, cols // tn),
        in_specs=[
            pl.BlockSpec((8, d), lambda l, j: (0, 0)),
            pl.BlockSpec((None, d, tn), lambda l, j: (l, 0, j)),
            pl.BlockSpec((None, 1, tn), lambda l, j: (l, 0, j)),
        ],
        out_specs=pl.BlockSpec((None, 8, tn), lambda l, j: (l, 0, j)),
        out_shape=jax.ShapeDtypeStruct((depth, 8, cols), F32),
        compiler_params=_params("parallel", "parallel"),
        name="modulation",
    )(cond8, w_ada, b_ada.reshape(depth, 1, cols))


def _in_body(*refs, rope, state, carried):
    it = iter(refs)
    x_ref, sh_ref, sc_ref, g_ref, win_ref = (next(it) for _ in range(5))
    qag_ref, kag_ref, qng_ref, kvng_ref = (next(it) for _ in range(4))
    wq_ref, wkk_ref, wkv_ref = (next(it) for _ in range(3))
    if rope:
        ca_ref, sa_ref, cb_ref, sb_ref = (next(it) for _ in range(4))
    for _ in range(carried):
        next(it)
    qa_o, ka_o, va_o, qb_o, kb_o, vb_o, qc_o, kc_o, vc_o = (next(it) for _ in range(9))
    if state:
        st_ka, st_va, st_ckv, st_kr, st_kc, st_vc = (next(it) for _ in range(6))

    h = _rms(x_ref[...], g_ref[...]) * (1.0 + sc_ref[...]) + sh_ref[...]
    hb = h.astype(BF16)

    def seg(bounds):
        return jnp.dot(hb, win_ref[:, bounds[0]:bounds[1]], preferred_element_type=F32)

    def rope_a(y):
        return _rope(y, ca_ref[...], sa_ref[...]) if rope else y

    def rope_b(y):
        return _rope(y, cb_ref[...], sb_ref[...]) if rope else y

    qa = seg(SEG_QA)
    for hd in range(HEADS_A):
        lo = hd * HEAD_DIM
        y = rope_a(_rms(qa[:, lo:lo + HEAD_DIM], qag_ref[...]))
        qa_o[:, lo:lo + HEAD_DIM] = (y * (HEAD_DIM ** -0.5 * LOG2E)).astype(BF16)
    ka = seg(SEG_KA)
    for hd in range(KV_HEADS_A):
        lo = hd * HEAD_DIM
        y = rope_a(_rms(ka[:, lo:lo + HEAD_DIM], kag_ref[...]))
        ka_o[:, lo:lo + HEAD_DIM] = y.astype(BF16)
        if state:
            st_ka[:, lo:lo + HEAD_DIM] = y
    va = seg(SEG_VA)
    va_o[...] = (va if state else va.T).astype(BF16)

    cq = _rms(seg(SEG_CQ), qng_ref[...]).astype(BF16)
    qb = jnp.dot(cq, wq_ref[...], preferred_element_type=F32)
    qscale = (NOPE_DIM_B + ROPE_DIM_B) ** -0.5 * LOG2E
    for hd in range(HEADS_B):
        lo = hd * QB_PAD
        qb_o[:, lo:lo + 128] = (qb[:, lo:lo + 128] * qscale).astype(BF16)
        qb_o[:, lo + 128:lo + 256] = (rope_b(qb[:, lo + 128:lo + 256]) * qscale).astype(BF16)
    ckv = _rms(seg(SEG_CKV), kvng_ref[...])
    ckvb = ckv.astype(BF16)
    kr = rope_b(seg(SEG_KR))
    krb = kr.astype(BF16)
    kn = jnp.dot(ckvb, wkk_ref[...], preferred_element_type=F32)
    for hd in range(HEADS_B):
        kb_o[:, hd * QB_PAD:hd * QB_PAD + 128] = kn[:, hd * 128:(hd + 1) * 128].astype(BF16)
        kb_o[:, hd * QB_PAD + 128:(hd + 1) * QB_PAD] = krb
    vb = jnp.dot(ckvb, wkv_ref[...], preferred_element_type=F32)
    vb_o[...] = (vb if state else vb.T).astype(BF16)

    qc_o[...] = (seg(SEG_QC) * (HEAD_DIM ** -0.5 * LOG2E)).astype(BF16)
    kc = seg(SEG_KC)
    kc_o[...] = kc.astype(BF16)
    vc = seg(SEG_VC)
    vc_o[...] = vc.astype(BF16)

    if state:
        st_va[...] = va
        st_ckv[...] = ckv
        st_kr[...] = kr[:, :ROPE_DIM_B]
        st_kc[...] = kc
        st_vc[...] = vc


def _input_side(x, mod, layer, wts, rope_tabs, states):
    state = states is not None
    b, n, d = x.shape
    tm = min(n, PROJ_TILE)
    mb = mod.shape[0]
    bsel = (lambda i: i) if mb > 1 else (lambda i: 0)
    rope = rope_tabs is not None

    def modspec(k):
        return pl.BlockSpec((None, None, 1, d), lambda bi, i: (bsel(bi), k, 0, 0))

    def vec(w):
        return pl.BlockSpec((None, 1, w), lambda bi, i: (layer, 0, 0))

    in_specs = [
        pl.BlockSpec((None, tm, d), lambda bi, i: (bi, i, 0)),
        modspec(0), modspec(1), vec(d),
        _resident((None, d, IN_COLS_P), lambda bi, i: (layer, 0, 0)),
        vec(HEAD_DIM), vec(HEAD_DIM), vec(Q_RANK_B), vec(KV_RANK_B),
        _resident((None, Q_RANK_B, HEADS_B * QB_PAD), lambda bi, i: (layer, 0, 0)),
        _resident((None, KV_RANK_B, HEADS_B * 128), lambda bi, i: (layer, 0, 0)),
        _resident((None, KV_RANK_B, HEADS_B * 128), lambda bi, i: (layer, 0, 0)),
    ]
    args = [x, mod, mod, wts["norm1_g"], wts["w_in"], wts["qa_g"], wts["ka_g"], wts["qn_g"], wts["kvn_g"],
            wts["w_q_up"], wts["w_kv_k"], wts["w_kv_v"]]
    if rope:
        in_specs += [pl.BlockSpec((tm, 128), lambda bi, i: (i, 0))] * 4
        args += list(rope_tabs)

    widths = [WIDTH_A, KV_A, KV_A, HEADS_B * QB_PAD, HEADS_B * QB_PAD, WIDTH_B, WIDTH_C, WIDTH_C, WIDTH_C]
    out_shape = [jax.ShapeDtypeStruct((b, n, w), BF16) for w in widths]
    out_specs = [pl.BlockSpec((None, tm, w), lambda bi, i: (bi, i, 0)) for w in widths]
    if not state:
        for k in (2, 5):
            out_shape[k] = jax.ShapeDtypeStruct((b, widths[k], n), BF16)
            out_specs[k] = pl.BlockSpec((None, widths[k], tm), lambda bi, i: (bi, 0, i))
    aliases = {}
    if state:
        depth = wts["w_in"].shape[0]
        for w in STATE_WIDTHS:
            out_shape.append(jax.ShapeDtypeStruct((b, depth, n, w), F32))
            out_specs.append(pl.BlockSpec((None, None, tm, w), lambda bi, i: (bi, layer, i, 0)))
        for k, buf in enumerate(states):
            aliases[len(args)] = len(widths) + k
            in_specs.append(pl.BlockSpec(memory_space=pl.ANY))
            args.append(buf)

    return pl.pallas_call(
        functools.partial(_in_body, rope=rope, state=state, carried=len(aliases)),
        grid=(b, n // tm),
        in_specs=in_specs,
        out_specs=out_specs,
        out_shape=out_shape,
        input_output_aliases=aliases,
        compiler_params=_params("parallel", "parallel"),
        name="input_side",
    )(*args)


def _attn_body(q_ref, ks_ref, vs_ref, kc_ref, vc_ref, o_ref):
    rows = min(q_ref.shape[0], ATTN_ROWS)
    groups = q_ref.shape[0] // rows

    def scores(c):
        q = q_ref[c * rows:(c + 1) * rows, :]
        s_self = lax.dot_general(ks_ref[...], q, NT_DIMS, preferred_element_type=F32)
        s_ctx = lax.dot_general(kc_ref[...], q, NT_DIMS, preferred_element_type=F32)
        m = jnp.maximum(jnp.max(s_self, axis=0, keepdims=True), jnp.max(s_ctx, axis=0, keepdims=True))
        return s_self, s_ctx, m

    def finish(c, s_self, s_ctx, m):
        p_self = jnp.exp2(s_self - m)
        p_ctx = jnp.exp2(s_ctx - m)
        l = jnp.sum(p_self, axis=0, keepdims=True) + jnp.sum(p_ctx, axis=0, keepdims=True)
        acc = jnp.dot(vs_ref[...], p_self.astype(BF16), preferred_element_type=F32)
        acc = acc + jnp.dot(vc_ref[...], p_ctx.astype(BF16), preferred_element_type=F32)
        o_ref[c * rows:(c + 1) * rows, :] = (acc / l).T.astype(o_ref.dtype)

    pending = scores(0)
    for c in range(groups):
        following = scores(c + 1) if c + 1 < groups else None
        finish(c, *pending)
        pending = following


def _ctx_attn_body(qa_ref, ka_ref, va_ref, qb_ref, kb_ref, vb_ref, qc_ref, kc_ref, vc_ref, oa_ref, ob_ref, oc_ref):
    def head(q, k, v):
        s = lax.dot_general(q, k, NT_DIMS, preferred_element_type=F32)
        p = jnp.exp2(s - jnp.max(s, axis=-1, keepdims=True))
        acc = jnp.dot(p.astype(BF16), v, preferred_element_type=F32)
        return (acc / jnp.sum(p, axis=-1, keepdims=True)).astype(BF16)

    def lanes(ref, i, w):
        return ref[:, i * w:(i + 1) * w]

    for h in range(HEADS_A):
        g = h // (HEADS_A // KV_HEADS_A)
        oa_ref[:, h * HEAD_DIM:(h + 1) * HEAD_DIM] = head(
            lanes(qa_ref, h, HEAD_DIM), lanes(ka_ref, g, HEAD_DIM), lanes(va_ref, g, HEAD_DIM))
    for h in range(HEADS_B):
        ob_ref[:, h * V_DIM_B:(h + 1) * V_DIM_B] = head(
            lanes(qb_ref, h, QB_PAD), lanes(kb_ref, h, QB_PAD), lanes(vb_ref, h, V_DIM_B))
    for h in range(HEADS_C):
        oc_ref[:, h * HEAD_DIM:(h + 1) * HEAD_DIM] = head(
            lanes(qc_ref, h, HEAD_DIM), lanes(kc_ref, h, HEAD_DIM), lanes(vc_ref, h, HEAD_DIM))


def _ctx_attention(qa, ka, va, qb, kb, vb, qc, kc, vc):
    b, n, _ = qa.shape
    args = [qa, ka, va, qb, kb, vb, qc, kc, vc]
    widths = [WIDTH_A, WIDTH_B, WIDTH_C]
    return pl.pallas_call(
        _ctx_attn_body,
        grid=(b,),
        in_specs=[pl.BlockSpec((None, n, a.shape[2]), lambda bi: (bi, 0, 0)) for a in args],
        out_specs=[pl.BlockSpec((None, n, w), lambda bi: (bi, 0, 0)) for w in widths],
        out_shape=[jax.ShapeDtypeStruct((b, n, w), BF16) for w in widths],
        compiler_params=_params("parallel"),
        name="ctx_attention",
    )(*args)


def _attention(q, ks, vs_t, kc, vc_t, layer, heads, group, dq, dv):
    b, n, _ = q.shape
    ms = ks.shape[1]
    mc = kc.shape[2]
    tq = min(n, ATTN_GROUPS * ATTN_ROWS)
    return pl.pallas_call(
        _attn_body,
        grid=(b, heads, n // tq),
        in_specs=[
            pl.BlockSpec((None, tq, dq), lambda bi, h, i: (bi, i, h)),
            pl.BlockSpec((None, ms, dq), lambda bi, h, i: (bi, 0, h // group)),
            pl.BlockSpec((None, dv, ms), lambda bi, h, i: (bi, h // group, 0)),
            pl.BlockSpec((None, None, mc, dq), lambda bi, h, i: (bi, layer, 0, h // group)),
            pl.BlockSpec((None, None, dv, mc), lambda bi, h, i: (bi, layer, h // group, 0)),
        ],
        out_specs=pl.BlockSpec((None, tq, dv), lambda bi, h, i: (bi, i, h)),
        out_shape=jax.ShapeDtypeStruct((b, n, heads * dv), BF16),
        compiler_params=_params("parallel", "parallel", "parallel"),
        name="attention",
    )(q, ks, vs_t, kc, vc_t)


def _nat_body(q_ref, ks_ref, vs_ref, kc_ref, vc_ref, bias_ref, o_ref):
    tq = NAT_QROWS * GRID_W
    nk = NAT_KROWS * GRID_W
    groups = q_ref.shape[0] // tq
    key_rows = ks_ref.shape[0] // GRID_W
    last = ks_ref.shape[0] // tq - 1

    def scores(c):
        r = pl.program_id(2) * groups + c
        kr0 = jnp.clip(NAT_QROWS * r - WIN_ROWS_MAX // 2, 0, key_rows - NAT_KROWS)
        start = pl.multiple_of(kr0 * GRID_W, tq)
        kind = jnp.where(r == 0, 0, jnp.where(r == last, 2, 1))
        q = q_ref[c * tq:(c + 1) * tq, :]
        s_win = lax.dot_general(q, ks_ref[pl.ds(start, nk), :], NT_DIMS, preferred_element_type=F32)
        s_win = s_win + bias_ref[kind]
        s_ctx = lax.dot_general(q, kc_ref[...], NT_DIMS, preferred_element_type=F32)
        m = jnp.maximum(jnp.max(s_win, axis=-1, keepdims=True), jnp.max(s_ctx, axis=-1, keepdims=True))
        return s_win, s_ctx, m, start

    def finish(c, s_win, s_ctx, m, start):
        p_win = jnp.exp2(s_win - m)
        p_ctx = jnp.exp2(s_ctx - m)
        l = jnp.sum(p_win, axis=-1, keepdims=True) + jnp.sum(p_ctx, axis=-1, keepdims=True)
        acc = jnp.dot(p_win.astype(BF16), vs_ref[pl.ds(start, nk), :], preferred_element_type=F32)
        acc = acc + jnp.dot(p_ctx.astype(BF16), vc_ref[...], preferred_element_type=F32)
        o_ref[c * tq:(c + 1) * tq, :] = (acc / l).astype(o_ref.dtype)

    pending = scores(0)
    for c in range(groups):
        following = scores(c + 1) if c + 1 < groups else None
        finish(c, *pending)
        pending = following


N_DROW = 2 * WIN_ROWS_MAX - 1
N_DCOL = 2 * WIN_COLS - 1


def _nat_bias_body(b_ref, o_ref, *, rows):
    base = (pl.program_id(0) * HEADS_C + pl.program_id(1)) * (N_DROW * N_DCOL)
    c = lax.broadcasted_iota(jnp.int32, (GRID_W, GRID_W), 0)
    kc = lax.broadcasted_iota(jnp.int32, (GRID_W, GRID_W), 1)
    c0 = jnp.clip(c - WIN_COLS // 2, 0, GRID_W - WIN_COLS)
    in_window = (kc >= c0) & (kc < c0 + WIN_COLS)
    dc = kc - c + (WIN_COLS - 1)
    masked = jnp.full((GRID_W, GRID_W), MASK_VALUE, F32)
    tables = {}

    def table(dr):
        if dr not in tables:
            t = jnp.zeros((GRID_W, GRID_W), F32)
            for j in range(N_DCOL):
                t = jnp.where(dc == j, b_ref[base + dr * N_DCOL + j] * LOG2E, t)
            tables[dr] = jnp.where(in_window, t, MASK_VALUE)
        return tables[dr]

    kh = min(WIN_ROWS_MAX, rows)
    for ty, blk in enumerate((0, 1, rows // NAT_QROWS - 1)):
        r_first = NAT_QROWS * blk
        kr0 = min(max(r_first - WIN_ROWS_MAX // 2, 0), rows - NAT_KROWS)
        for a in range(NAT_QROWS):
            r = r_first + a
            r0 = min(max(r - kh // 2, 0), rows - kh)
            for i in range(NAT_KROWS):
                kr = kr0 + i
                blkval = table(kr - r + WIN_ROWS_MAX - 1) if r0 <= kr < r0 + kh else masked
                o_ref[ty, a * GRID_W:(a + 1) * GRID_W, i * GRID_W:(i + 1) * GRID_W] = blkval


def _nat_bias(na_bias, rows):
    depth = na_bias.shape[0]
    tq, nk = NAT_QROWS * GRID_W, NAT_KROWS * GRID_W
    return pl.pallas_call(
        functools.partial(_nat_bias_body, rows=rows),
        grid=(depth, HEADS_C),
        in_specs=[pl.BlockSpec(memory_space=pltpu.SMEM)],
        out_specs=pl.BlockSpec((None, None, 3, tq, nk), lambda l, h: (l, h, 0, 0, 0)),
        out_shape=jax.ShapeDtypeStruct((depth, HEADS_C, 3, tq, nk), F32),
        compiler_params=_params("parallel", "parallel"),
        name="nat_bias",
    )(na_bias.reshape(-1))


def _neighbourhood_attention(q, ks, vs, kc, vc, bias, layer):
    b, n, _ = q.shape
    nblk = n // (NAT_QROWS * GRID_W)
    groups = min(nblk, NAT_GROUPS)
    tq = groups * NAT_QROWS * GRID_W
    mc = kc.shape[2]
    d = HEAD_DIM

    return pl.pallas_call(
        _nat_body,
        grid=(b, HEADS_C, nblk // groups),
        in_specs=[
            pl.BlockSpec((None, tq, d), lambda bi, h, r: (bi, r, h)),
            pl.BlockSpec((None, n, d), lambda bi, h, r: (bi, 0, h)),
            pl.BlockSpec((None, n, d), lambda bi, h, r: (bi, 0, h)),
            pl.BlockSpec((None, None, mc, d), lambda bi, h, r: (bi, layer, 0, h)),
            pl.BlockSpec((None, None, mc, d), lambda bi, h, r: (bi, layer, 0, h)),
            pl.BlockSpec((None, None, 3, NAT_QROWS * GRID_W, NAT_KROWS * GRID_W),
                         lambda bi, h, r: (layer, h, 0, 0, 0)),
        ],
        out_specs=pl.BlockSpec((None, tq, d), lambda bi, h, r: (bi, r, h)),
        out_shape=jax.ShapeDtypeStruct((b, n, HEADS_C * d), BF16),
        compiler_params=_params("parallel", "parallel", "arbitrary"),
        name="neighbourhood_attention",
    )(q, ks, vs, kc, vc, bias)


def _cache_kv_body(ckv_ref, kr_ref, wkk_ref, wkv_ref, kb_o, vb_o):
    ckvb = ckv_ref[...].astype(BF16)
    krb = kr_ref[...].astype(BF16)
    kn = jnp.dot(ckvb, wkk_ref[...], preferred_element_type=F32)
    zeros = jnp.zeros((krb.shape[0], QB_PAD - 128 - ROPE_DIM_B), BF16)
    for hd in range(HEADS_B):
        lo = hd * QB_PAD
        kb_o[:, lo:lo + 128] = kn[:, hd * 128:(hd + 1) * 128].astype(BF16)
        kb_o[:, lo + 128:lo + 128 + ROPE_DIM_B] = krb
        kb_o[:, lo + 128 + ROPE_DIM_B:lo + QB_PAD] = zeros
    vb_o[...] = jnp.dot(ckvb, wkv_ref[...], preferred_element_type=F32).T.astype(BF16)


def _cache_mla_kv(cache_ckv, cache_krope, w_kv_k, w_kv_v):
    b, depth, m, _ = cache_ckv.shape
    return pl.pallas_call(
        _cache_kv_body,
        grid=(b, depth),
        in_specs=[
            pl.BlockSpec((None, None, m, KV_RANK_B), lambda bi, l: (bi, l, 0, 0)),
            pl.BlockSpec((None, None, m, ROPE_DIM_B), lambda bi, l: (bi, l, 0, 0)),
            pl.BlockSpec((None, KV_RANK_B, HEADS_B * 128), lambda bi, l: (l, 0, 0)),
            pl.BlockSpec((None, KV_RANK_B, HEADS_B * 128), lambda bi, l: (l, 0, 0)),
        ],
        out_specs=[
            pl.BlockSpec((None, None, m, HEADS_B * QB_PAD), lambda bi, l: (bi, l, 0, 0)),
            pl.BlockSpec((None, None, WIDTH_B, m), lambda bi, l: (bi, l, 0, 0)),
        ],
        out_shape=[
            jax.ShapeDtypeStruct((b, depth, m, HEADS_B * QB_PAD), BF16),
            jax.ShapeDtypeStruct((b, depth, WIDTH_B, m), BF16),
        ],
        compiler_params=_params("parallel", "parallel"),
        name="cache_mla_kv",
    )(cache_ckv, cache_krope, w_kv_k, w_kv_v)


def _out_body(oa_ref, ob_ref, oc_ref, wo_ref, x_ref, gate_ref, sh_ref, sc_ref, g_ref, wr_ref, x1_o, h2_o, lg_o):
    lo_b, lo_c = WIDTH_A, WIDTH_A + WIDTH_B
    o = jnp.dot(oa_ref[...], wo_ref[:lo_b, :], preferred_element_type=F32)
    o = o + jnp.dot(ob_ref[...], wo_ref[lo_b:lo_c, :], preferred_element_type=F32)
    o = o + jnp.dot(oc_ref[...], wo_ref[lo_c:, :], preferred_element_type=F32)
    x1 = x_ref[...] + gate_ref[...] * o
    x1_o[...] = x1
    h = _rms(x1, g_ref[...]) * (1.0 + sc_ref[...]) + sh_ref[...]
    hb = h.astype(BF16)
    h2_o[...] = hb
    lg = jnp.dot(hb, wr_ref[...], preferred_element_type=F32)
    for c in range(lg.shape[0] // 128):
        lg_o[:, c * 128:(c + 1) * 128] = lg[c * 128:(c + 1) * 128, :].T[:N_EXPERTS, :]


def _output_side(o_a, o_b, o_c, x, mod, layer, wts):
    b, n, d = x.shape
    tm = TOKEN_TILE
    mb = mod.shape[0]
    bsel = (lambda i: i) if mb > 1 else (lambda i: 0)

    def modspec(k):
        return pl.BlockSpec((None, None, 1, d), lambda bi, i: (bsel(bi), k, 0, 0))

    def tok(w):
        return pl.BlockSpec((None, tm, w), lambda bi, i: (bi, i, 0))

    return pl.pallas_call(
        _out_body,
        grid=(b, n // tm),
        in_specs=[
            tok(WIDTH_A), tok(WIDTH_B), tok(WIDTH_C),
            _resident((None, WIDTH_A + WIDTH_B + WIDTH_C, d), lambda bi, i: (layer, 0, 0)),
            tok(d), modspec(2), modspec(3), modspec(4),
            pl.BlockSpec((None, 1, d), lambda bi, i: (layer, 0, 0)),
            pl.BlockSpec((None, d, 128), lambda bi, i: (layer, 0, 0)),
        ],
        out_specs=[tok(d), tok(d), pl.BlockSpec((None, N_EXPERTS, tm), lambda bi, i: (bi, 0, i))],
        out_shape=[
            jax.ShapeDtypeStruct((b, n, d), F32),
            jax.ShapeDtypeStruct((b, n, d), BF16),
            jax.ShapeDtypeStruct((b, N_EXPERTS, n), F32),
        ],
        compiler_params=_params("parallel", "parallel"),
        name="output_side",
    )(o_a, o_b, o_c, wts["w_out"], x, mod, mod, mod, wts["norm2_g"], wts["w_router"])


def _prefix_exclusive(mask):
    e, n = mask.shape
    ones = jnp.where(mask, 1.0, 0.0)
    rr = lax.broadcasted_iota(jnp.int32, (128, 128), 0)
    cc = lax.broadcasted_iota(jnp.int32, (128, 128), 1)
    tri = jnp.where(rr <= cc, 1.0, 0.0).astype(BF16)
    carry = jnp.zeros((e, 1), F32)
    outs = []
    for c in range(n // 128):
        blk = ones[:, c * 128:(c + 1) * 128]
        inc = jnp.dot(blk.astype(BF16), tri, preferred_element_type=F32)
        outs.append(inc - blk + carry)
        carry = carry + inc[:, 127:128]
    return jnp.concatenate(outs, axis=1)


def _router_body(lg_ref, rank_o, rank_t_o, aff_t_o, cum_o, *, cap, chunk):
    bb, e, n = lg_ref.shape
    lg = lg_ref[...]
    ex = jnp.exp(lg - jnp.max(lg, axis=1, keepdims=True))
    aff = (ex / jnp.sum(ex, axis=1, keepdims=True)).reshape(bb * e, n)
    key = pltpu.bitcast(aff, jnp.int32)

    def step(i, t):
        cand = t | lax.shift_left(jnp.int32(1), 30 - i)
        cnt = jnp.sum(jnp.where(key >= cand, 1.0, 0.0), axis=1, keepdims=True)
        return jnp.where(cnt >= cap, cand, t)

    thr = lax.fori_loop(0, 31, step, jnp.zeros((bb * e, 1), jnp.int32))
    above = key > thr
    tied = key == thr
    need = cap - jnp.sum(jnp.where(above, 1.0, 0.0), axis=1, keepdims=True)
    chosen = above | (tied & (_prefix_exclusive(tied) < need))
    before = _prefix_exclusive(chosen)
    rank = jnp.where(chosen, before, -1.0)
    rank_o[...] = rank.astype(jnp.int32).reshape(bb, e, n)

    lane = lax.broadcasted_iota(jnp.int32, (bb * e, 128), 1)
    cum = jnp.full((bb * e, 128), float(cap), F32)
    for k in range(n // chunk):
        cum = jnp.where(lane == k, before[:, k * chunk:k * chunk + 1], cum)
    cum_o[...] = cum.astype(jnp.int32).reshape(bb, e, 128)

    fill = jnp.full((128 - e, n), -1.0, F32)
    for bi in range(bb):
        rank_p = jnp.concatenate([rank[bi * e:(bi + 1) * e], fill], axis=0)
        aff_p = jnp.concatenate([aff[bi * e:(bi + 1) * e], fill], axis=0)
        for c in range(n // 128):
            rank_t_o[bi, c * 128:(c + 1) * 128, :] = rank_p[:, c * 128:(c + 1) * 128].T
            aff_t_o[bi, c * 128:(c + 1) * 128, :] = aff_p[:, c * 128:(c + 1) * 128].T


def _route(logits_t):
    b, e, n = logits_t.shape
    cap = CAPACITY_FACTOR * n // e
    bb = max(1, min(b, ROUTE_TOKENS // n))
    return pl.pallas_call(
        functools.partial(_router_body, cap=cap, chunk=SCATTER_TOKENS),
        grid=(b // bb,),
        in_specs=[pl.BlockSpec((bb, e, n), lambda bi: (bi, 0, 0))],
        out_specs=[
            pl.BlockSpec((bb, e, n), lambda bi: (bi, 0, 0)),
            pl.BlockSpec((bb, n, 128), lambda bi: (bi, 0, 0)),
            pl.BlockSpec((bb, n, 128), lambda bi: (bi, 0, 0)),
            pl.BlockSpec((bb, e, 128), lambda bi: (bi, 0, 0)),
        ],
        out_shape=[
            jax.ShapeDtypeStruct((b, e, n), jnp.int32),
            jax.ShapeDtypeStruct((b, n, 128), F32),
            jax.ShapeDtypeStruct((b, n, 128), F32),
            jax.ShapeDtypeStruct((b, e, 128), jnp.int32),
        ],
        compiler_params=_params("parallel"),
        name="route",
    )(logits_t)


def _gather_body(cum_ref, h_ref, rank_ref, xe_o, *, chunk, rows):
    n, d = h_ref.shape
    experts, cap, _ = xe_o.shape
    nch = n // chunk
    per_chunk = chunk // SCATTER_TOKENS
    first = (pl.program_id(0) * pl.num_programs(1) + pl.program_id(1)) * experts
    for k in range(experts):
        rank = rank_ref[k]
        if nch == 1 and cap == rows:
            slot = lax.broadcasted_iota(jnp.int32, (rows, chunk), 0)
            onehot = jnp.where(slot == rank, 1.0, 0.0).astype(BF16)
            xe_o[k] = jnp.dot(onehot, h_ref[...], preferred_element_type=F32).astype(xe_o.dtype)
            continue
        base = (first + k) * (n // SCATTER_TOKENS + 1)
        xe_o[k] = jnp.zeros((cap, d), xe_o.dtype)
        for c in range(nch):
            lo = cum_ref[base + c * per_chunk]
            hi = cum_ref[base + (c + 1) * per_chunk]
            rank_c = rank[:, c * chunk:(c + 1) * chunk]
            for jb in range(cap // rows):

                @pl.when((lo < (jb + 1) * rows) & (hi > jb * rows))
                def _():
                    slot = lax.broadcasted_iota(jnp.int32, (rows, chunk), 0) + jb * rows
                    onehot = jnp.where(slot == rank_c, 1.0, 0.0).astype(BF16)
                    picked = jnp.dot(onehot, h_ref[c * chunk:(c + 1) * chunk, :], preferred_element_type=F32)
                    xe_o[k, jb * rows:(jb + 1) * rows, :] += picked.astype(xe_o.dtype)


def _gather(h2, rank, cum_flat):
    b, n, d = h2.shape
    e = rank.shape[1]
    cap = CAPACITY_FACTOR * n // e
    chunk = min(n, GATHER_CHUNK)
    per_step = e if n <= chunk else 1
    return pl.pallas_call(
        functools.partial(_gather_body, chunk=chunk, rows=min(cap, GATHER_ROWS)),
        grid_spec=pltpu.PrefetchScalarGridSpec(
            num_scalar_prefetch=1,
            grid=(b, e // per_step),
            in_specs=[
                pl.BlockSpec((None, n, d), lambda bi, ei, cum_ref: (bi, 0, 0)),
                pl.BlockSpec((None, per_step, 1, n), lambda bi, ei, cum_ref: (bi, ei, 0, 0)),
            ],
            out_specs=pl.BlockSpec((per_step, None, cap, d), lambda bi, ei, cum_ref: (ei, bi, 0, 0)),
        ),
        out_shape=jax.ShapeDtypeStruct((e, b, cap, d), BF16),
        compiler_params=_params("parallel", "parallel"),
        name="gather",
    )(cum_flat, h2, rank)


def _ffn_body(xe_ref, wg_ref, wu_ref, wd_ref, ye_o):
    bb, cap, d = xe_ref.shape
    xe = xe_ref[...].reshape(bb * cap, d)
    g = jnp.dot(xe, wg_ref[...], preferred_element_type=F32)
    u = jnp.dot(xe, wu_ref[...], preferred_element_type=F32)
    hid = (g / (1.0 + jnp.exp(-g)) * u).astype(BF16)
    ye = jnp.dot(hid, wd_ref[...], preferred_element_type=F32)
    ye_o[...] = ye.astype(ye_o.dtype).reshape(bb, cap, d)


def _expert_ffn(xe, w_gate, w_up, w_down):
    e, b, cap, d = xe.shape
    ff = w_gate.shape[-1]
    bb = max(1, min(b, 512 // cap))
    return pl.pallas_call(
        _ffn_body,
        grid=(e, b // bb),
        in_specs=[
            pl.BlockSpec((None, bb, cap, d), lambda ei, bi: (ei, bi, 0, 0)),
            pl.BlockSpec((None, d, ff), lambda ei, bi: (ei, 0, 0)),
            pl.BlockSpec((None, d, ff), lambda ei, bi: (ei, 0, 0)),
            pl.BlockSpec((None, ff, d), lambda ei, bi: (ei, 0, 0)),
        ],
        out_specs=pl.BlockSpec((None, bb, cap, d), lambda ei, bi: (ei, bi, 0, 0)),
        out_shape=jax.ShapeDtypeStruct((e, b, cap, d), BF16),
        compiler_params=_params("parallel", "arbitrary"),
        name="expert_ffn",
    )(xe, w_gate, w_up, w_down)


def _ffn_cast_body(xe_ref, wg_ref, wu_ref, wd_ref, ye_o, wg_o, wu_o, wd_o, acc):
    f = pl.program_id(1)
    bb, cap, d = xe_ref.shape
    wg = wg_ref[...].astype(BF16)
    wu = wu_ref[...].astype(BF16)
    wd = wd_ref[...].astype(BF16)
    wg_o[...] = wg
    wu_o[...] = wu
    wd_o[...] = wd
    xe = xe_ref[...].reshape(bb * cap, d)
    g = jnp.dot(xe, wg, preferred_element_type=F32)
    u = jnp.dot(xe, wu, preferred_element_type=F32)
    hid = (g / (1.0 + jnp.exp(-g)) * u).astype(BF16)
    part = jnp.dot(hid, wd, preferred_element_type=F32)

    @pl.when(f == 0)
    def _():
        acc[...] = part

    @pl.when(f > 0)
    def _():
        acc[...] += part

    @pl.when(f == pl.num_programs(1) - 1)
    def _():
        ye_o[...] = acc[...].astype(ye_o.dtype).reshape(bb, cap, d)


def _expert_ffn_cast(xe, layer, w_gate, w_up, w_down):
    e, b, cap, d = xe.shape
    ff = w_gate.shape[-1]
    ffs = FF_SLICE
    return pl.pallas_call(
        _ffn_cast_body,
        grid=(e, ff // ffs),
        in_specs=[
            pl.BlockSpec((None, b, cap, d), lambda ei, f: (ei, 0, 0, 0)),
            pl.BlockSpec((None, None, d, ffs), lambda ei, f: (layer, ei, 0, f)),
            pl.BlockSpec((None, None, d, ffs), lambda ei, f: (layer, ei, 0, f)),
            pl.BlockSpec((None, None, ffs, d), lambda ei, f: (layer, ei, f, 0)),
        ],
        out_specs=[
            pl.BlockSpec((None, b, cap, d), lambda ei, f: (ei, 0, 0, 0)),
            pl.BlockSpec((None, d, ffs), lambda ei, f: (ei, 0, f)),
            pl.BlockSpec((None, d, ffs), lambda ei, f: (ei, 0, f)),
            pl.BlockSpec((None, ffs, d), lambda ei, f: (ei, f, 0)),
        ],
        out_shape=[
            jax.ShapeDtypeStruct((e, b, cap, d), BF16),
            jax.ShapeDtypeStruct((e, d, ff), BF16),
            jax.ShapeDtypeStruct((e, d, ff), BF16),
            jax.ShapeDtypeStruct((e, ff, d), BF16),
        ],
        scratch_shapes=[pltpu.VMEM((b * cap, d), F32)],
        compiler_params=_params("parallel", "arbitrary"),
        name="expert_ffn_cast",
    )(xe, w_gate, w_up, w_down)


def _scatter_body(cum_ref, ye_ref, x_ref, rank_ref, aff_ref, gate_ref, o_ref):
    e, cap, dh = ye_ref.shape
    tn = x_ref.shape[0]
    sub_tiles = tn // SCATTER_TOKENS
    win = min(cap, 2 * SCATTER_TOKENS)
    per_expert = pl.num_programs(2) * sub_tiles + 1
    col = lax.broadcasted_iota(jnp.int32, (SCATTER_TOKENS, win), 1).astype(F32)
    for sub in range(sub_tiles):
        rows = slice(sub * SCATTER_TOKENS, (sub + 1) * SCATTER_TOKENS)
        acc = jnp.zeros((SCATTER_TOKENS, dh), F32)
        for ei in range(e):
            rank = rank_ref[rows, ei:ei + 1]
            if cap > win:
                first = cum_ref[(pl.program_id(0) * e + ei) * per_expert + pl.program_id(2) * sub_tiles + sub]
                start = jnp.clip((first // SCATTER_TOKENS) * SCATTER_TOKENS, 0, cap - win)
                start = pl.multiple_of(start, SCATTER_TOKENS)
                onehot = jnp.where(rank - start.astype(F32) == col, 1.0, 0.0).astype(BF16)
                contrib = jnp.dot(onehot, ye_ref[ei, pl.ds(start, win), :], preferred_element_type=F32)
            else:
                onehot = jnp.where(rank == col, 1.0, 0.0).astype(BF16)
                contrib = jnp.dot(onehot, ye_ref[ei], preferred_element_type=F32)
            acc = acc + contrib * aff_ref[rows, ei:ei + 1]
        o_ref[rows, :] = x_ref[rows, :] + gate_ref[...] * acc


def _scatter(ye, x1, rank_t, aff_t, mod, cum_flat):
    e, b, cap, d = ye.shape
    n = x1.shape[1]
    tn = min(n, 512)
    dh = d if e * cap * d * 2 <= SCATTER_YE_BYTES else d // 2
    mb = mod.shape[0]
    bsel = (lambda i: i) if mb > 1 else (lambda i: 0)
    return pl.pallas_call(
        _scatter_body,
        grid_spec=pltpu.PrefetchScalarGridSpec(
            num_scalar_prefetch=1,
            grid=(b, d // dh, n // tn),
            in_specs=[
                pl.BlockSpec((e, None, cap, dh), lambda bi, j, i, cum_ref: (0, bi, 0, j)),
                pl.BlockSpec((None, tn, dh), lambda bi, j, i, cum_ref: (bi, i, j)),
                pl.BlockSpec((None, tn, 128), lambda bi, j, i, cum_ref: (bi, i, 0)),
                pl.BlockSpec((None, tn, 128), lambda bi, j, i, cum_ref: (bi, i, 0)),
                pl.BlockSpec((None, None, 1, dh), lambda bi, j, i, cum_ref: (bsel(bi), 5, 0, j)),
            ],
            out_specs=pl.BlockSpec((None, tn, dh), lambda bi, j, i, cum_ref: (bi, i, j)),
        ),
        out_shape=jax.ShapeDtypeStruct(x1.shape, F32),
        compiler_params=_params("parallel", "parallel", "parallel"),
        name="scatter",
    )(cum_flat, ye, x1, rank_t, aff_t, mod)


def _final_norm_body(x_ref, g_ref, o_ref):
    o_ref[...] = _rms(x_ref[...], g_ref[...])


def _final_norm(x, g):
    b, n, d = x.shape
    tm = TOKEN_TILE
    return pl.pallas_call(
        _final_norm_body,
        grid=(b, n // tm),
        in_specs=[pl.BlockSpec((None, tm, d), lambda bi, i: (bi, i, 0)), pl.BlockSpec((1, d), lambda bi, i: (0, 0))],
        out_specs=pl.BlockSpec((None, tm, d), lambda bi, i: (bi, i, 0)),
        out_shape=jax.ShapeDtypeStruct(x.shape, F32),
        compiler_params=_params("parallel", "parallel"),
        name="final_norm",
    )(x, g.reshape(1, d))


def _rope_tables(n_tokens, rot_dim):
    t = jnp.arange(n_tokens, dtype=jnp.int32)
    row = (t // GRID_W).astype(F32)
    col = (t % GRID_W).astype(F32)
    axis_dim = rot_dim // 2
    freqs = ROPE_THETA ** (-jnp.arange(0, axis_dim, 2, dtype=F32) / axis_dim)
    ang = jnp.concatenate([row[:, None] * freqs[None, :], col[:, None] * freqs[None, :]], axis=-1)
    cos, sin = jnp.cos(ang), jnp.sin(ang)
    cos2 = jnp.repeat(cos, 2, axis=-1)
    sin2 = jnp.stack([-sin, sin], axis=-1).reshape(n_tokens, rot_dim)
    pad = 128 - rot_dim
    if pad:
        cos2 = jnp.concatenate([cos2, jnp.ones((n_tokens, pad), F32)], axis=-1)
        sin2 = jnp.concatenate([sin2, jnp.zeros((n_tokens, pad), F32)], axis=-1)
    return cos2, sin2


def _prepare_weights(norm1_g, norm2_g, w_in, qa_norm_g, ka_norm_g, q_norm_b, kv_norm_b, w_q_up, w_kv_up, w_out,
                     w_router):
    depth, d, _ = w_in.shape
    kr_lo = SEG_CKV[1]
    w_in16 = w_in.astype(BF16)
    w_in_p = jnp.concatenate(
        [w_in16[:, :, :kr_lo], w_in16[:, :, kr_lo + ROPE_DIM_B:], w_in16[:, :, kr_lo:kr_lo + ROPE_DIM_B],
         jnp.zeros((depth, d, 128 - ROPE_DIM_B), BF16)], axis=-1)
    wq = w_q_up.reshape(depth, Q_RANK_B, HEADS_B, NOPE_DIM_B + ROPE_DIM_B)
    wq = jnp.pad(wq, ((0, 0), (0, 0), (0, 0), (0, QB_PAD - NOPE_DIM_B - ROPE_DIM_B)))
    wkv = w_kv_up.reshape(depth, KV_RANK_B, HEADS_B, NOPE_DIM_B + V_DIM_B)
    return {
        "norm1_g": norm1_g.reshape(depth, 1, d),
        "norm2_g": norm2_g.reshape(depth, 1, d),
        "w_in": w_in_p,
        "qa_g": qa_norm_g.reshape(depth, 1, HEAD_DIM),
        "ka_g": ka_norm_g.reshape(depth, 1, HEAD_DIM),
        "qn_g": q_norm_b.reshape(depth, 1, Q_RANK_B),
        "kvn_g": kv_norm_b.reshape(depth, 1, KV_RANK_B),
        "w_q_up": wq.reshape(depth, Q_RANK_B, HEADS_B * QB_PAD).astype(BF16),
        "w_kv_k": wkv[..., :NOPE_DIM_B].reshape(depth, KV_RANK_B, HEADS_B * NOPE_DIM_B).astype(BF16),
        "w_kv_v": wkv[..., NOPE_DIM_B:].reshape(depth, KV_RANK_B, HEADS_B * V_DIM_B).astype(BF16),
        "w_out": w_out.astype(BF16),
        "w_router": jnp.pad(w_router, ((0, 0), (0, 0), (0, 128 - N_EXPERTS))).astype(BF16),
    }


def _moe_select(h2, logits_t):
    rank, rank_t, aff_t, cum = _route(logits_t)
    n = rank.shape[2]
    cum_flat = cum[:, :, :n // SCATTER_TOKENS + 1].reshape(-1)
    xe = _gather(h2, rank.reshape(rank.shape[0], rank.shape[1], 1, n), cum_flat)
    return xe, rank_t, aff_t, cum_flat


def kernel(x_prompt, x_sample, cache_a_k, cache_a_v, cache_b_ckv, cache_b_krope, cache_c_k, cache_c_v, c, c_ctx, w_ada, b_ada, norm1_g, norm2_g, w_in, qa_norm_g, ka_norm_g, q_norm_b, kv_norm_b, w_q_up, w_kv_up, na_bias, w_out, w_router, w_gate, w_up, w_down, final_norm_g):
    depth, d, _ = w_in.shape
    bp, seq, _ = x_prompt.shape
    bs, n_lat, _ = x_sample.shape
    past = cache_a_k.shape[2]
    rows = n_lat // GRID_W

    wts = _prepare_weights(norm1_g, norm2_g, w_in, qa_norm_g, ka_norm_g, q_norm_b, kv_norm_b, w_q_up, w_kv_up,
                           w_out, w_router)
    rope_tabs = _rope_tables(n_lat, HEAD_DIM) + _rope_tables(n_lat, ROPE_DIM_B)

    cond8 = jnp.concatenate([c_ctx[None], c, jnp.zeros((8 - 1 - bs, d), F32)], axis=0)
    mods = _modulation(cond8, w_ada, b_ada)
    mods = mods.reshape(depth, 8, 6, 1, d)

    ca_k = cache_a_k.reshape(bs, depth, past, KV_A).astype(BF16)
    ca_v = jnp.swapaxes(cache_a_v.reshape(bs, depth, past, KV_A), 2, 3).astype(BF16)
    cc_k = cache_c_k.reshape(bs, depth, past, WIDTH_C).astype(BF16)
    cc_v = cache_c_v.reshape(bs, depth, past, WIDTH_C).astype(BF16)
    cb_k, cb_v = _cache_mla_kv(cache_b_ckv, cache_b_krope, wts["w_kv_k"], wts["w_kv_v"])
    nat_bias = _nat_bias(na_bias, rows)

    xp, xs = x_prompt, x_sample
    states = tuple(jnp.zeros((bp, depth, seq, w), F32) for w in STATE_WIDTHS)
    for l in range(depth):
        mod_c = mods[l, 0:1]
        mod_l = mods[l, 1:1 + bs]

        qa, ka, va, qb, kb, vb, qc, kc, vc, *states = _input_side(xp, mod_c, l, wts, None, states)
        o_a, o_b, o_c = _ctx_attention(qa, ka, va, qb, kb, vb, qc, kc, vc)
        x1, h2, lg = _output_side(o_a, o_b, o_c, xp, mod_c, l, wts)
        xe, rank_t, aff_t, cum_flat = _moe_select(h2, lg)
        ye, wg16, wu16, wd16 = _expert_ffn_cast(xe, l, w_gate, w_up, w_down)
        xp = _scatter(ye, x1, rank_t, aff_t, mod_c, cum_flat)

        qa, ka, va, qb, kb, vb, qc, kc, vc = _input_side(xs, mod_l, l, wts, rope_tabs, None)
        o_a = _attention(qa, ka, va, ca_k, ca_v, l, HEADS_A, HEADS_A // KV_HEADS_A, HEAD_DIM, HEAD_DIM)
        o_b = _attention(qb, kb, vb, cb_k, cb_v, l, HEADS_B, 1, QB_PAD, V_DIM_B)
        o_c = _neighbourhood_attention(qc, kc, vc, cc_k, cc_v, nat_bias, l)
        x1, h2, lg = _output_side(o_a, o_b, o_c, xs, mod_l, l, wts)
        xe, rank_t, aff_t, cum_flat = _moe_select(h2, lg)
        ye = _expert_ffn(xe, wg16, wu16, wd16)
        xs = _scatter(ye, x1, rank_t, aff_t, mod_l, cum_flat)

    y_prompt = _final_norm(xp, final_norm_g)
    y_sample = _final_norm(xs, final_norm_g)
    st_ka, st_va, new_b_ckv, new_b_krope, st_kc, st_vc = states
    new_a_k = st_ka.reshape(bp, depth, seq, KV_HEADS_A, HEAD_DIM)
    new_a_v = st_va.reshape(bp, depth, seq, KV_HEADS_A, HEAD_DIM)
    new_c_k = st_kc.reshape(bp, depth, seq, HEADS_C, HEAD_DIM)
    new_c_v = st_vc.reshape(bp, depth, seq, HEADS_C, HEAD_DIM)
    return (y_prompt, y_sample, new_a_k, new_a_v, new_b_ckv, new_b_krope, new_c_k, new_c_v)
```

```python
import functools

import jax
import jax.numpy as jnp
from jax import lax
from jax.experimental import pallas as pl
from jax.experimental.pallas import tpu as pltpu

F32 = jnp.float32
BF16 = jnp.bfloat16

GRID_W = 64
HEAD_DIM = 128
HEADS_A = 6
KV_HEADS_A = 2
HEADS_B = 5
Q_RANK_B = 512
KV_RANK_B = 256
NOPE_DIM_B = 128
ROPE_DIM_B = 64
V_DIM_B = 128
HEADS_C = 5
WIN_ROWS_MAX = 8
WIN_COLS = 16
N_EXPERTS = 16
CAPACITY_FACTOR = 2
ROPE_THETA = 10000.0
EPS = 1e-6

QB_PAD = 256
WIDTH_A = HEADS_A * HEAD_DIM
WIDTH_B = HEADS_B * V_DIM_B
WIDTH_C = HEADS_C * HEAD_DIM
KV_A = KV_HEADS_A * HEAD_DIM

SEG_QA = (0, WIDTH_A)
SEG_KA = (SEG_QA[1], SEG_QA[1] + KV_A)
SEG_VA = (SEG_KA[1], SEG_KA[1] + KV_A)
SEG_CQ = (SEG_VA[1], SEG_VA[1] + Q_RANK_B)
SEG_CKV = (SEG_CQ[1], SEG_CQ[1] + KV_RANK_B)
SEG_QC = (SEG_CKV[1], SEG_CKV[1] + WIDTH_C)
SEG_KC = (SEG_QC[1], SEG_QC[1] + WIDTH_C)
SEG_VC = (SEG_KC[1], SEG_KC[1] + WIDTH_C)
SEG_KR = (SEG_VC[1], SEG_VC[1] + 128)
IN_COLS_P = SEG_KR[1]

STATE_WIDTHS = (KV_A, KV_A, KV_RANK_B, ROPE_DIM_B, WIDTH_C, WIDTH_C)

NAT_QROWS = 4
NAT_KROWS = 12
MASK_VALUE = -1e30
LOG2E = 1.4426950408889634

TOKEN_TILE = 256
PROJ_TILE = 512
ATTN_ROWS = 512
ATTN_GROUPS = 2
NAT_GROUPS = 16
GATHER_CHUNK = 512
GATHER_ROWS = 128
SCATTER_TOKENS = 128
ROUTE_TOKENS = 4096
SCATTER_YE_BYTES = 16 * 1024 * 1024
FF_SLICE = 512
VMEM_LIMIT = 56 * 1024 * 1024
NT_DIMS = (((1,), (1,)), ((), ()))


def _params(*sem):
    return pltpu.CompilerParams(dimension_semantics=sem, vmem_limit_bytes=VMEM_LIMIT)


def _resident(block_shape, index_map):
    return pl.BlockSpec(block_shape, index_map, pipeline_mode=pl.Buffered(1))


def _rms(x, g):
    ms = jnp.mean(x * x, axis=-1, keepdims=True)
    return x * lax.rsqrt(ms + EPS) * g


def _swap_pairs(x):
    lane = lax.broadcasted_iota(jnp.int32, x.shape, x.ndim - 1)
    nxt = pltpu.roll(x, x.shape[-1] - 1, x.ndim - 1)
    prv = pltpu.roll(x, 1, x.ndim - 1)
    return jnp.where((lane & 1) == 0, nxt, prv)


def _rope(x, cos, sin_signed):
    return x * cos + _swap_pairs(x) * sin_signed


def _mod_body(c_ref, w_ref, b_ref, o_ref):
    c = c_ref[...]
    s = (c / (1.0 + jnp.exp(-c))).astype(BF16)
    o_ref[...] = jnp.dot(s, w_ref[...].astype(BF16), preferred_element_type=F32) + b_ref[...]


def _modulation(cond8, w_ada, b_ada):
    depth, d, cols = w_ada.shape
    tn = 1024
    return pl.pallas_call(
        _mod_body,
        grid=(depth, cols // tn),
        in_specs=[
            pl.BlockSpec((8, d), lambda l, j: (0, 0)),
            pl.BlockSpec((None, d, tn), lambda l, j: (l, 0, j)),
            pl.BlockSpec((None, 1, tn), lambda l, j: (l, 0, j)),
        ],
        out_specs=pl.BlockSpec((None, 8, tn), lambda l, j: (l, 0, j)),
        out_shape=jax.ShapeDtypeStruct((depth, 8, cols), F32),
        compiler_params=_params("parallel", "parallel"),
        name="modulation",
    )(cond8, w_ada, b_ada.reshape(depth, 1, cols))


def _in_body(*refs, rope, state, carried, layer):
    it = iter(refs)
    x_ref, sh_ref, sc_ref, g_ref, win_ref = (next(it) for _ in range(5))
    qag_ref, kag_ref, qng_ref, kvng_ref = (next(it) for _ in range(4))
    wq_ref, wkk_ref, wkv_ref = (next(it) for _ in range(3))
    if rope:
        ca_ref, sa_ref, cb_ref, sb_ref = (next(it) for _ in range(4))
    for _ in range(carried):
        next(it)
    qa_o, ka_o, va_o, qb_o, kb_o, vb_o, qc_o, kc_o, vc_o = (next(it) for _ in range(9))
    if state:
        st_ka, st_va, st_ckv, st_kr, st_kc, st_vc = (next(it) for _ in range(6))

    def put_state(ref, value, lo=0):
        w = value.shape[1]
        if carried:
            ref[:, lo:lo + w] = value
        else:
            for l in range(ref.shape[0]):
                ref[l, :, lo:lo + w] = value if l == layer else jnp.zeros_like(value)

    h = _rms(x_ref[...], g_ref[...]) * (1.0 + sc_ref[...]) + sh_ref[...]
    hb = h.astype(BF16)

    def seg(bounds):
        return jnp.dot(hb, win_ref[:, bounds[0]:bounds[1]], preferred_element_type=F32)

    def rope_a(y):
        return _rope(y, ca_ref[...], sa_ref[...]) if rope else y

    def rope_b(y):
        return _rope(y, cb_ref[...], sb_ref[...]) if rope else y

    qa = seg(SEG_QA)
    for hd in range(HEADS_A):
        lo = hd * HEAD_DIM
        y = rope_a(_rms(qa[:, lo:lo + HEAD_DIM], qag_ref[...]))
        qa_o[:, lo:lo + HEAD_DIM] = (y * (HEAD_DIM ** -0.5 * LOG2E)).astype(BF16)
    ka = seg(SEG_KA)
    for hd in range(KV_HEADS_A):
        lo = hd * HEAD_DIM
        y = rope_a(_rms(ka[:, lo:lo + HEAD_DIM], kag_ref[...]))
        ka_o[:, lo:lo + HEAD_DIM] = y.astype(BF16)
        if state:
            put_state(st_ka, y, lo)
    va = seg(SEG_VA)
    va_o[...] = (va if state else va.T).astype(BF16)

    cq = _rms(seg(SEG_CQ), qng_ref[...]).astype(BF16)
    qb = jnp.dot(cq, wq_ref[...], preferred_element_type=F32)
    qscale = (NOPE_DIM_B + ROPE_DIM_B) ** -0.5 * LOG2E
    for hd in range(HEADS_B):
        lo = hd * QB_PAD
        qb_o[:, lo:lo + 128] = (qb[:, lo:lo + 128] * qscale).astype(BF16)
        qb_o[:, lo + 128:lo + 256] = (rope_b(qb[:, lo + 128:lo + 256]) * qscale).astype(BF16)
    ckv = _rms(seg(SEG_CKV), kvng_ref[...])
    ckvb = ckv.astype(BF16)
    kr = rope_b(seg(SEG_KR))
    krb = kr.astype(BF16)
    kn = jnp.dot(ckvb, wkk_ref[...], preferred_element_type=F32)
    for hd in range(HEADS_B):
        kb_o[:, hd * QB_PAD:hd * QB_PAD + 128] = kn[:, hd * 128:(hd + 1) * 128].astype(BF16)
        kb_o[:, hd * QB_PAD + 128:(hd + 1) * QB_PAD] = krb
    vb = jnp.dot(ckvb, wkv_ref[...], preferred_element_type=F32)
    vb_o[...] = (vb if state else vb.T).astype(BF16)

    qc_o[...] = (seg(SEG_QC) * (HEAD_DIM ** -0.5 * LOG2E)).astype(BF16)
    kc = seg(SEG_KC)
    kc_o[...] = kc.astype(BF16)
    vc = seg(SEG_VC)
    vc_o[...] = vc.astype(BF16)

    if state:
        put_state(st_va, va)
        put_state(st_ckv, ckv)
        put_state(st_kr, kr[:, :ROPE_DIM_B])
        put_state(st_kc, kc)
        put_state(st_vc, vc)


def _input_side(x, mod, layer, wts, rope_tabs, states):
    state = states is not None
    b, n, d = x.shape
    tm = min(n, PROJ_TILE)
    mb = mod.shape[0]
    bsel = (lambda i: i) if mb > 1 else (lambda i: 0)
    rope = rope_tabs is not None

    def modspec(k):
        return pl.BlockSpec((None, None, 1, d), lambda bi, i: (bsel(bi), k, 0, 0))

    def vec(w):
        return pl.BlockSpec((None, 1, w), lambda bi, i: (layer, 0, 0))

    in_specs = [
        pl.BlockSpec((None, tm, d), lambda bi, i: (bi, i, 0)),
        modspec(0), modspec(1), vec(d),
        _resident((None, d, IN_COLS_P), lambda bi, i: (layer, 0, 0)),
        vec(HEAD_DIM), vec(HEAD_DIM), vec(Q_RANK_B), vec(KV_RANK_B),
        _resident((None, Q_RANK_B, HEADS_B * QB_PAD), lambda bi, i: (layer, 0, 0)),
        _resident((None, KV_RANK_B, HEADS_B * 128), lambda bi, i: (layer, 0, 0)),
        _resident((None, KV_RANK_B, HEADS_B * 128), lambda bi, i: (layer, 0, 0)),
    ]
    args = [x, mod, mod, wts["norm1_g"], wts["w_in"], wts["qa_g"], wts["ka_g"], wts["qn_g"], wts["kvn_g"],
            wts["w_q_up"], wts["w_kv_k"], wts["w_kv_v"]]
    if rope:
        in_specs += [pl.BlockSpec((tm, 128), lambda bi, i: (i, 0))] * 4
        args += list(rope_tabs)

    widths = [WIDTH_A, KV_A, KV_A, HEADS_B * QB_PAD, HEADS_B * QB_PAD, WIDTH_B, WIDTH_C, WIDTH_C, WIDTH_C]
    out_shape = [jax.ShapeDtypeStruct((b, n, w), BF16) for w in widths]
    out_specs = [pl.BlockSpec((None, tm, w), lambda bi, i: (bi, i, 0)) for w in widths]
    if not state:
        for k in (2, 5):
            out_shape[k] = jax.ShapeDtypeStruct((b, widths[k], n), BF16)
            out_specs[k] = pl.BlockSpec((None, widths[k], tm), lambda bi, i: (bi, 0, i))
    aliases = {}
    if state:
        depth = wts["w_in"].shape[0]
        for w in STATE_WIDTHS:
            out_shape.append(jax.ShapeDtypeStruct((b, depth, n, w), F32))
            if states:
                out_specs.append(pl.BlockSpec((None, None, tm, w), lambda bi, i: (bi, layer, i, 0)))
            else:
                out_specs.append(pl.BlockSpec((None, depth, tm, w), lambda bi, i: (bi, 0, i, 0)))
        for k, buf in enumerate(states):
            aliases[len(args)] = len(widths) + k
            in_specs.append(pl.BlockSpec(memory_space=pl.ANY))
            args.append(buf)

    return pl.pallas_call(
        functools.partial(_in_body, rope=rope, state=state, carried=len(aliases), layer=layer),
        grid=(b, n // tm),
        in_specs=in_specs,
        out_specs=out_specs,
        out_shape=out_shape,
        input_output_aliases=aliases,
        compiler_params=_params("parallel", "parallel"),
        name="input_side",
    )(*args)


def _attn_body(q_ref, ks_ref, vs_ref, kc_ref, vc_ref, o_ref):
    rows = min(q_ref.shape[0], ATTN_ROWS)
    groups = q_ref.shape[0] // rows

    def scores(c):
        q = q_ref[c * rows:(c + 1) * rows, :]
        s_self = lax.dot_general(ks_ref[...], q, NT_DIMS, preferred_element_type=F32)
        s_ctx = lax.dot_general(kc_ref[...], q, NT_DIMS, preferred_element_type=F32)
        m = jnp.maximum(jnp.max(s_self, axis=0, keepdims=True), jnp.max(s_ctx, axis=0, keepdims=True))
        return s_self, s_ctx, m

    def finish(c, s_self, s_ctx, m):
        p_self = jnp.exp2(s_self - m)
        p_ctx = jnp.exp2(s_ctx - m)
        l = jnp.sum(p_self, axis=0, keepdims=True) + jnp.sum(p_ctx, axis=0, keepdims=True)
        acc = jnp.dot(vs_ref[...], p_self.astype(BF16), preferred_element_type=F32)
        acc = acc + jnp.dot(vc_ref[...], p_ctx.astype(BF16), preferred_element_type=F32)
        o_ref[c * rows:(c + 1) * rows, :] = (acc / l).T.astype(o_ref.dtype)

    pending = scores(0)
    for c in range(groups):
        following = scores(c + 1) if c + 1 < groups else None
        finish(c, *pending)
        pending = following


def _ctx_attn_body(qa_ref, ka_ref, va_ref, qb_ref, kb_ref, vb_ref, qc_ref, kc_ref, vc_ref, oa_ref, ob_ref, oc_ref):
    def head(q, k, v):
        s = lax.dot_general(q, k, NT_DIMS, preferred_element_type=F32)
        p = jnp.exp2(s - jnp.max(s, axis=-1, keepdims=True))
        acc = jnp.dot(p.astype(BF16), v, preferred_element_type=F32)
        return (acc / jnp.sum(p, axis=-1, keepdims=True)).astype(BF16)

    def lanes(ref, i, w):
        return ref[:, i * w:(i + 1) * w]

    for h in range(HEADS_A):
        g = h // (HEADS_A // KV_HEADS_A)
        oa_ref[:, h * HEAD_DIM:(h + 1) * HEAD_DIM] = head(
            lanes(qa_ref, h, HEAD_DIM), lanes(ka_ref, g, HEAD_DIM), lanes(va_ref, g, HEAD_DIM))
    for h in range(HEADS_B):
        ob_ref[:, h * V_DIM_B:(h + 1) * V_DIM_B] = head(
            lanes(qb_ref, h, QB_PAD), lanes(kb_ref, h, QB_PAD), lanes(vb_ref, h, V_DIM_B))
    for h in range(HEADS_C):
        oc_ref[:, h * HEAD_DIM:(h + 1) * HEAD_DIM] = head(
            lanes(qc_ref, h, HEAD_DIM), lanes(kc_ref, h, HEAD_DIM), lanes(vc_ref, h, HEAD_DIM))


def _ctx_attention(qa, ka, va, qb, kb, vb, qc, kc, vc):
    b, n, _ = qa.shape
    args = [qa, ka, va, qb, kb, vb, qc, kc, vc]
    widths = [WIDTH_A, WIDTH_B, WIDTH_C]
    return pl.pallas_call(
        _ctx_attn_body,
        grid=(b,),
        in_specs=[pl.BlockSpec((None, n, a.shape[2]), lambda bi: (bi, 0, 0)) for a in args],
        out_specs=[pl.BlockSpec((None, n, w), lambda bi: (bi, 0, 0)) for w in widths],
        out_shape=[jax.ShapeDtypeStruct((b, n, w), BF16) for w in widths],
        compiler_params=_params("parallel"),
        name="ctx_attention",
    )(*args)


def _attention(q, ks, vs_t, kc, vc_t, layer, heads, group, dq, dv):
    b, n, _ = q.shape
    ms = ks.shape[1]
    mc = kc.shape[2]
    tq = min(n, ATTN_GROUPS * ATTN_ROWS)
    return pl.pallas_call(
        _attn_body,
        grid=(b, heads, n // tq),
        in_specs=[
            pl.BlockSpec((None, tq, dq), lambda bi, h, i: (bi, i, h)),
            pl.BlockSpec((None, ms, dq), lambda bi, h, i: (bi, 0, h // group)),
            pl.BlockSpec((None, dv, ms), lambda bi, h, i: (bi, h // group, 0)),
            pl.BlockSpec((None, None, mc, dq), lambda bi, h, i: (bi, layer, 0, h // group)),
            pl.BlockSpec((None, None, dv, mc), lambda bi, h, i: (bi, layer, h // group, 0)),
        ],
        out_specs=pl.BlockSpec((None, tq, dv), lambda bi, h, i: (bi, i, h)),
        out_shape=jax.ShapeDtypeStruct((b, n, heads * dv), BF16),
        compiler_params=_params("parallel", "parallel", "parallel"),
        name="attention",
    )(q, ks, vs_t, kc, vc_t)


def _nat_body(q_ref, ks_ref, vs_ref, kc_ref, vc_ref, bias_ref, o_ref):
    tq = NAT_QROWS * GRID_W
    nk = NAT_KROWS * GRID_W
    groups = q_ref.shape[0] // tq
    key_rows = ks_ref.shape[0] // GRID_W
    last = ks_ref.shape[0] // tq - 1

    def scores(c):
        r = pl.program_id(2) * groups + c
        kr0 = jnp.clip(NAT_QROWS * r - WIN_ROWS_MAX // 2, 0, key_rows - NAT_KROWS)
        start = pl.multiple_of(kr0 * GRID_W, tq)
        kind = jnp.where(r == 0, 0, jnp.where(r == last, 2, 1))
        q = q_ref[c * tq:(c + 1) * tq, :]
        s_win = lax.dot_general(q, ks_ref[pl.ds(start, nk), :], NT_DIMS, preferred_element_type=F32)
        s_win = s_win + bias_ref[kind]
        s_ctx = lax.dot_general(q, kc_ref[...], NT_DIMS, preferred_element_type=F32)
        m = jnp.maximum(jnp.max(s_win, axis=-1, keepdims=True), jnp.max(s_ctx, axis=-1, keepdims=True))
        return s_win, s_ctx, m, start

    def finish(c, s_win, s_ctx, m, start):
        p_win = jnp.exp2(s_win - m)
        p_ctx = jnp.exp2(s_ctx - m)
        l = jnp.sum(p_win, axis=-1, keepdims=True) + jnp.sum(p_ctx, axis=-1, keepdims=True)
        acc = jnp.dot(p_win.astype(BF16), vs_ref[pl.ds(start, nk), :], preferred_element_type=F32)
        acc = acc + jnp.dot(p_ctx.astype(BF16), vc_ref[...], preferred_element_type=F32)
        o_ref[c * tq:(c + 1) * tq, :] = (acc / l).astype(o_ref.dtype)

    pending = scores(0)
    for c in range(groups):
        following = scores(c + 1) if c + 1 < groups else None
        finish(c, *pending)
        pending = following


N_DROW = 2 * WIN_ROWS_MAX - 1
N_DCOL = 2 * WIN_COLS - 1


def _nat_bias_body(b_ref, o_ref, *, rows):
    base = (pl.program_id(0) * HEADS_C + pl.program_id(1)) * (N_DROW * N_DCOL)
    c = lax.broadcasted_iota(jnp.int32, (GRID_W, GRID_W), 0)
    kc = lax.broadcasted_iota(jnp.int32, (GRID_W, GRID_W), 1)
    c0 = jnp.clip(c - WIN_COLS // 2, 0, GRID_W - WIN_COLS)
    in_window = (kc >= c0) & (kc < c0 + WIN_COLS)
    dc = kc - c + (WIN_COLS - 1)
    masked = jnp.full((GRID_W, GRID_W), MASK_VALUE, F32)
    tables = {}

    def table(dr):
        if dr not in tables:
            t = jnp.zeros((GRID_W, GRID_W), F32)
            for j in range(N_DCOL):
                t = jnp.where(dc == j, b_ref[base + dr * N_DCOL + j] * LOG2E, t)
            tables[dr] = jnp.where(in_window, t, MASK_VALUE)
        return tables[dr]

    kh = min(WIN_ROWS_MAX, rows)
    for ty, blk in enumerate((0, 1, rows // NAT_QROWS - 1)):
        r_first = NAT_QROWS * blk
        kr0 = min(max(r_first - WIN_ROWS_MAX // 2, 0), rows - NAT_KROWS)
        for a in range(NAT_QROWS):
            r = r_first + a
            r0 = min(max(r - kh // 2, 0), rows - kh)
            for i in range(NAT_KROWS):
                kr = kr0 + i
                blkval = table(kr - r + WIN_ROWS_MAX - 1) if r0 <= kr < r0 + kh else masked
                o_ref[ty, a * GRID_W:(a + 1) * GRID_W, i * GRID_W:(i + 1) * GRID_W] = blkval


def _nat_bias(na_bias, rows):
    depth = na_bias.shape[0]
    tq, nk = NAT_QROWS * GRID_W, NAT_KROWS * GRID_W
    return pl.pallas_call(
        functools.partial(_nat_bias_body, rows=rows),
        grid=(depth, HEADS_C),
        in_specs=[pl.BlockSpec(memory_space=pltpu.SMEM)],
        out_specs=pl.BlockSpec((None, None, 3, tq, nk), lambda l, h: (l, h, 0, 0, 0)),
        out_shape=jax.ShapeDtypeStruct((depth, HEADS_C, 3, tq, nk), F32),
        compiler_params=_params("parallel", "parallel"),
        name="nat_bias",
    )(na_bias.reshape(-1))


def _neighbourhood_attention(q, ks, vs, kc, vc, bias, layer):
    b, n, _ = q.shape
    nblk = n // (NAT_QROWS * GRID_W)
    groups = min(nblk, NAT_GROUPS)
    tq = groups * NAT_QROWS * GRID_W
    mc = kc.shape[2]
    d = HEAD_DIM

    return pl.pallas_call(
        _nat_body,
        grid=(b, HEADS_C, nblk // groups),
        in_specs=[
            pl.BlockSpec((None, tq, d), lambda bi, h, r: (bi, r, h)),
            pl.BlockSpec((None, n, d), lambda bi, h, r: (bi, 0, h)),
            pl.BlockSpec((None, n, d), lambda bi, h, r: (bi, 0, h)),
            pl.BlockSpec((None, None, mc, d), lambda bi, h, r: (bi, layer, 0, h)),
            pl.BlockSpec((None, None, mc, d), lambda bi, h, r: (bi, layer, 0, h)),
            pl.BlockSpec((None, None, 3, NAT_QROWS * GRID_W, NAT_KROWS * GRID_W),
                         lambda bi, h, r: (layer, h, 0, 0, 0)),
        ],
        out_specs=pl.BlockSpec((None, tq, d), lambda bi, h, r: (bi, r, h)),
        out_shape=jax.ShapeDtypeStruct((b, n, HEADS_C * d), BF16),
        compiler_params=_params("parallel", "parallel", "arbitrary"),
        name="neighbourhood_attention",
    )(q, ks, vs, kc, vc, bias)


def _cache_kv_body(ckv_ref, kr_ref, wkk_ref, wkv_ref, kb_o, vb_o):
    ckvb = ckv_ref[...].astype(BF16)
    krb = kr_ref[...].astype(BF16)
    kn = jnp.dot(ckvb, wkk_ref[...], preferred_element_type=F32)
    zeros = jnp.zeros((krb.shape[0], QB_PAD - 128 - ROPE_DIM_B), BF16)
    for hd in range(HEADS_B):
        lo = hd * QB_PAD
        kb_o[:, lo:lo + 128] = kn[:, hd * 128:(hd + 1) * 128].astype(BF16)
        kb_o[:, lo + 128:lo + 128 + ROPE_DIM_B] = krb
        kb_o[:, lo + 128 + ROPE_DIM_B:lo + QB_PAD] = zeros
    vb_o[...] = jnp.dot(ckvb, wkv_ref[...], preferred_element_type=F32).T.astype(BF16)


def _cache_mla_kv(cache_ckv, cache_krope, w_kv_k, w_kv_v):
    b, depth, m, _ = cache_ckv.shape
    return pl.pallas_call(
        _cache_kv_body,
        grid=(b, depth),
        in_specs=[
            pl.BlockSpec((None, None, m, KV_RANK_B), lambda bi, l: (bi, l, 0, 0)),
            pl.BlockSpec((None, None, m, ROPE_DIM_B), lambda bi, l: (bi, l, 0, 0)),
            pl.BlockSpec((None, KV_RANK_B, HEADS_B * 128), lambda bi, l: (l, 0, 0)),
            pl.BlockSpec((None, KV_RANK_B, HEADS_B * 128), lambda bi, l: (l, 0, 0)),
        ],
        out_specs=[
            pl.BlockSpec((None, None, m, HEADS_B * QB_PAD), lambda bi, l: (bi, l, 0, 0)),
            pl.BlockSpec((None, None, WIDTH_B, m), lambda bi, l: (bi, l, 0, 0)),
        ],
        out_shape=[
            jax.ShapeDtypeStruct((b, depth, m, HEADS_B * QB_PAD), BF16),
            jax.ShapeDtypeStruct((b, depth, WIDTH_B, m), BF16),
        ],
        compiler_params=_params("parallel", "parallel"),
        name="cache_mla_kv",
    )(cache_ckv, cache_krope, w_kv_k, w_kv_v)


def _out_body(oa_ref, ob_ref, oc_ref, wo_ref, x_ref, gate_ref, sh_ref, sc_ref, g_ref, wr_ref, x1_o, h2_o, lg_o):
    lo_b, lo_c = WIDTH_A, WIDTH_A + WIDTH_B
    o = jnp.dot(oa_ref[...], wo_ref[:lo_b, :], preferred_element_type=F32)
    o = o + jnp.dot(ob_ref[...], wo_ref[lo_b:lo_c, :], preferred_element_type=F32)
    o = o + jnp.dot(oc_ref[...], wo_ref[lo_c:, :], preferred_element_type=F32)
    x1 = x_ref[...] + gate_ref[...] * o
    x1_o[...] = x1
    h = _rms(x1, g_ref[...]) * (1.0 + sc_ref[...]) + sh_ref[...]
    hb = h.astype(BF16)
    h2_o[...] = hb
    lg = jnp.dot(hb, wr_ref[...], preferred_element_type=F32)
    for c in range(lg.shape[0] // 128):
        lg_o[:, c * 128:(c + 1) * 128] = lg[c * 128:(c + 1) * 128, :].T[:N_EXPERTS, :]


def _output_side(o_a, o_b, o_c, x, mod, layer, wts):
    b, n, d = x.shape
    tm = TOKEN_TILE
    mb = mod.shape[0]
    bsel = (lambda i: i) if mb > 1 else (lambda i: 0)

    def modspec(k):
        return pl.BlockSpec((None, None, 1, d), lambda bi, i: (bsel(bi), k, 0, 0))

    def tok(w):
        return pl.BlockSpec((None, tm, w), lambda bi, i: (bi, i, 0))

    return pl.pallas_call(
        _out_body,
        grid=(b, n // tm),
        in_specs=[
            tok(WIDTH_A), tok(WIDTH_B), tok(WIDTH_C),
            _resident((None, WIDTH_A + WIDTH_B + WIDTH_C, d), lambda bi, i: (layer, 0, 0)),
            tok(d), modspec(2), modspec(3), modspec(4),
            pl.BlockSpec((None, 1, d), lambda bi, i: (layer, 0, 0)),
            pl.BlockSpec((None, d, 128), lambda bi, i: (layer, 0, 0)),
        ],
        out_specs=[tok(d), tok(d), pl.BlockSpec((None, N_EXPERTS, tm), lambda bi, i: (bi, 0, i))],
        out_shape=[
            jax.ShapeDtypeStruct((b, n, d), F32),
            jax.ShapeDtypeStruct((b, n, d), BF16),
            jax.ShapeDtypeStruct((b, N_EXPERTS, n), F32),
        ],
        compiler_params=_params("parallel", "parallel"),
        name="output_side",
    )(o_a, o_b, o_c, wts["w_out"], x, mod, mod, mod, wts["norm2_g"], wts["w_router"])


def _prefix_exclusive(mask):
    e, n = mask.shape
    ones = jnp.where(mask, 1.0, 0.0)
    rr = lax.broadcasted_iota(jnp.int32, (128, 128), 0)
    cc = lax.broadcasted_iota(jnp.int32, (128, 128), 1)
    tri = jnp.where(rr <= cc, 1.0, 0.0).astype(BF16)
    carry = jnp.zeros((e, 1), F32)
    outs = []
    for c in range(n // 128):
        blk = ones[:, c * 128:(c + 1) * 128]
        inc = jnp.dot(blk.astype(BF16), tri, preferred_element_type=F32)
        outs.append(inc - blk + carry)
        carry = carry + inc[:, 127:128]
    return jnp.concatenate(outs, axis=1)


def _router_body(lg_ref, rank_o, rank_t_o, aff_t_o, cum_o, *, cap, chunk):
    bb, e, n = lg_ref.shape
    lg = lg_ref[...]
    ex = jnp.exp(lg - jnp.max(lg, axis=1, keepdims=True))
    aff = (ex / jnp.sum(ex, axis=1, keepdims=True)).reshape(bb * e, n)
    key = pltpu.bitcast(aff, jnp.int32)

    def step(i, t):
        cand = t | lax.shift_left(jnp.int32(1), 30 - i)
        cnt = jnp.sum(jnp.where(key >= cand, 1.0, 0.0), axis=1, keepdims=True)
        return jnp.where(cnt >= cap, cand, t)

    thr = lax.fori_loop(0, 31, step, jnp.zeros((bb * e, 1), jnp.int32))
    above = key > thr
    tied = key == thr
    need = cap - jnp.sum(jnp.where(above, 1.0, 0.0), axis=1, keepdims=True)
    chosen = above | (tied & (_prefix_exclusive(tied) < need))
    before = _prefix_exclusive(chosen)
    rank = jnp.where(chosen, before, -1.0)
    rank_o[...] = rank.astype(jnp.int32).reshape(bb, e, n)

    lane = lax.broadcasted_iota(jnp.int32, (bb * e, 128), 1)
    cum = jnp.full((bb * e, 128), float(cap), F32)
    for k in range(n // chunk):
        cum = jnp.where(lane == k, before[:, k * chunk:k * chunk + 1], cum)
    cum_o[...] = cum.astype(jnp.int32).reshape(bb, e, 128)

    fill = jnp.full((128 - e, n), -1.0, F32)
    for bi in range(bb):
        rank_p = jnp.concatenate([rank[bi * e:(bi + 1) * e], fill], axis=0)
        aff_p = jnp.concatenate([aff[bi * e:(bi + 1) * e], fill], axis=0)
        for c in range(n // 128):
            rank_t_o[bi, c * 128:(c + 1) * 128, :] = rank_p[:, c * 128:(c + 1) * 128].T
            aff_t_o[bi, c * 128:(c + 1) * 128, :] = aff_p[:, c * 128:(c + 1) * 128].T


def _route(logits_t):
    b, e, n = logits_t.shape
    cap = CAPACITY_FACTOR * n // e
    bb = max(1, min(b, ROUTE_TOKENS // n))
    return pl.pallas_call(
        functools.partial(_router_body, cap=cap, chunk=SCATTER_TOKENS),
        grid=(b // bb,),
        in_specs=[pl.BlockSpec((bb, e, n), lambda bi: (bi, 0, 0))],
        out_specs=[
            pl.BlockSpec((bb, e, n), lambda bi: (bi, 0, 0)),
            pl.BlockSpec((bb, n, 128), lambda bi: (bi, 0, 0)),
            pl.BlockSpec((bb, n, 128), lambda bi: (bi, 0, 0)),
            pl.BlockSpec((bb, e, 128), lambda bi: (bi, 0, 0)),
        ],
        out_shape=[
            jax.ShapeDtypeStruct((b, e, n), jnp.int32),
            jax.ShapeDtypeStruct((b, n, 128), F32),
            jax.ShapeDtypeStruct((b, n, 128), F32),
            jax.ShapeDtypeStruct((b, e, 128), jnp.int32),
        ],
        compiler_params=_params("parallel"),
        name="route",
    )(logits_t)


def _gather_body(cum_ref, h_ref, rank_ref, xe_o, *, chunk, rows):
    n, d = h_ref.shape
    experts, cap, _ = xe_o.shape
    nch = n // chunk
    per_chunk = chunk // SCATTER_TOKENS
    first = (pl.program_id(0) * pl.num_programs(1) + pl.program_id(1)) * experts
    for k in range(experts):
        rank = rank_ref[k]
        if nch == 1 and cap == rows:
            slot = lax.broadcasted_iota(jnp.int32, (rows, chunk), 0)
            onehot = jnp.where(slot == rank, 1.0, 0.0).astype(BF16)
            xe_o[k] = jnp.dot(onehot, h_ref[...], preferred_element_type=F32).astype(xe_o.dtype)
            continue
        base = (first + k) * (n // SCATTER_TOKENS + 1)
        xe_o[k] = jnp.zeros((cap, d), xe_o.dtype)
        for c in range(nch):
            lo = cum_ref[base + c * per_chunk]
            hi = cum_ref[base + (c + 1) * per_chunk]
            rank_c = rank[:, c * chunk:(c + 1) * chunk]
            for jb in range(cap // rows):

                @pl.when((lo < (jb + 1) * rows) & (hi > jb * rows))
                def _():
                    slot = lax.broadcasted_iota(jnp.int32, (rows, chunk), 0) + jb * rows
                    onehot = jnp.where(slot == rank_c, 1.0, 0.0).astype(BF16)
                    picked = jnp.dot(onehot, h_ref[c * chunk:(c + 1) * chunk, :], preferred_element_type=F32)
                    xe_o[k, jb * rows:(jb + 1) * rows, :] += picked.astype(xe_o.dtype)


def _gather(h2, rank, cum_flat):
    b, n, d = h2.shape
    e = rank.shape[1]
    cap = CAPACITY_FACTOR * n // e
    chunk = min(n, GATHER_CHUNK)
    per_step = e if n <= chunk else 1
    return pl.pallas_call(
        functools.partial(_gather_body, chunk=chunk, rows=min(cap, GATHER_ROWS)),
        grid_spec=pltpu.PrefetchScalarGridSpec(
            num_scalar_prefetch=1,
            grid=(b, e // per_step),
            in_specs=[
                pl.BlockSpec((None, n, d), lambda bi, ei, cum_ref: (bi, 0, 0)),
                pl.BlockSpec((None, per_step, 1, n), lambda bi, ei, cum_ref: (bi, ei, 0, 0)),
            ],
            out_specs=pl.BlockSpec((per_step, None, cap, d), lambda bi, ei, cum_ref: (ei, bi, 0, 0)),
        ),
        out_shape=jax.ShapeDtypeStruct((e, b, cap, d), BF16),
        compiler_params=_params("parallel", "parallel"),
        name="gather",
    )(cum_flat, h2, rank)


def _ffn_body(xe_ref, wg_ref, wu_ref, wd_ref, ye_o):
    bb, cap, d = xe_ref.shape
    xe = xe_ref[...].reshape(bb * cap, d)
    g = jnp.dot(xe, wg_ref[...], preferred_element_type=F32)
    u = jnp.dot(xe, wu_ref[...], preferred_element_type=F32)
    hid = (g / (1.0 + jnp.exp(-g)) * u).astype(BF16)
    ye = jnp.dot(hid, wd_ref[...], preferred_element_type=F32)
    ye_o[...] = ye.astype(ye_o.dtype).reshape(bb, cap, d)


def _expert_ffn(xe, w_gate, w_up, w_down):
    e, b, cap, d = xe.shape
    ff = w_gate.shape[-1]
    bb = max(1, min(b, 512 // cap))
    return pl.pallas_call(
        _ffn_body,
        grid=(e, b // bb),
        in_specs=[
            pl.BlockSpec((None, bb, cap, d), lambda ei, bi: (ei, bi, 0, 0)),
            pl.BlockSpec((None, d, ff), lambda ei, bi: (ei, 0, 0)),
            pl.BlockSpec((None, d, ff), lambda ei, bi: (ei, 0, 0)),
            pl.BlockSpec((None, ff, d), lambda ei, bi: (ei, 0, 0)),
        ],
        out_specs=pl.BlockSpec((None, bb, cap, d), lambda ei, bi: (ei, bi, 0, 0)),
        out_shape=jax.ShapeDtypeStruct((e, b, cap, d), BF16),
        compiler_params=_params("parallel", "arbitrary"),
        name="expert_ffn",
    )(xe, w_gate, w_up, w_down)


def _ffn_cast_body(xe_ref, wg_ref, wu_ref, wd_ref, ye_o, wg_o, wu_o, wd_o, acc):
    f = pl.program_id(1)
    bb, cap, d = xe_ref.shape
    wg = wg_ref[...].astype(BF16)
    wu = wu_ref[...].astype(BF16)
    wd = wd_ref[...].astype(BF16)
    wg_o[...] = wg
    wu_o[...] = wu
    wd_o[...] = wd
    xe = xe_ref[...].reshape(bb * cap, d)
    g = jnp.dot(xe, wg, preferred_element_type=F32)
    u = jnp.dot(xe, wu, preferred_element_type=F32)
    hid = (g / (1.0 + jnp.exp(-g)) * u).astype(BF16)
    part = jnp.dot(hid, wd, preferred_element_type=F32)

    @pl.when(f == 0)
    def _():
        acc[...] = part

    @pl.when(f > 0)
    def _():
        acc[...] += part

    @pl.when(f == pl.num_programs(1) - 1)
    def _():
        ye_o[...] = acc[...].astype(ye_o.dtype).reshape(bb, cap, d)


def _expert_ffn_cast(xe, layer, w_gate, w_up, w_down):
    e, b, cap, d = xe.shape
    ff = w_gate.shape[-1]
    ffs = FF_SLICE
    return pl.pallas_call(
        _ffn_cast_body,
        grid=(e, ff // ffs),
        in_specs=[
            pl.BlockSpec((None, b, cap, d), lambda ei, f: (ei, 0, 0, 0)),
            pl.BlockSpec((None, None, d, ffs), lambda ei, f: (layer, ei, 0, f)),
            pl.BlockSpec((None, None, d, ffs), lambda ei, f: (layer, ei, 0, f)),
            pl.BlockSpec((None, None, ffs, d), lambda ei, f: (layer, ei, f, 0)),
        ],
        out_specs=[
            pl.BlockSpec((None, b, cap, d), lambda ei, f: (ei, 0, 0, 0)),
            pl.BlockSpec((None, d, ffs), lambda ei, f: (ei, 0, f)),
            pl.BlockSpec((None, d, ffs), lambda ei, f: (ei, 0, f)),
            pl.BlockSpec((None, ffs, d), lambda ei, f: (ei, f, 0)),
        ],
        out_shape=[
            jax.ShapeDtypeStruct((e, b, cap, d), BF16),
            jax.ShapeDtypeStruct((e, d, ff), BF16),
            jax.ShapeDtypeStruct((e, d, ff), BF16),
            jax.ShapeDtypeStruct((e, ff, d), BF16),
        ],
        scratch_shapes=[pltpu.VMEM((b * cap, d), F32)],
        compiler_params=_params("parallel", "arbitrary"),
        name="expert_ffn_cast",
    )(xe, w_gate, w_up, w_down)


def _scatter_body(cum_ref, ye_ref, x_ref, rank_ref, aff_ref, gate_ref, o_ref):
    e, cap, dh = ye_ref.shape
    tn = x_ref.shape[0]
    sub_tiles = tn // SCATTER_TOKENS
    win = min(cap, 2 * SCATTER_TOKENS)
    per_expert = pl.num_programs(2) * sub_tiles + 1
    col = lax.broadcasted_iota(jnp.int32, (SCATTER_TOKENS, win), 1).astype(F32)
    for sub in range(sub_tiles):
        rows = slice(sub * SCATTER_TOKENS, (sub + 1) * SCATTER_TOKENS)
        acc = jnp.zeros((SCATTER_TOKENS, dh), F32)
        for ei in range(e):
            rank = rank_ref[rows, ei:ei + 1]
            if cap > win:
                first = cum_ref[(pl.program_id(0) * e + ei) * per_expert + pl.program_id(2) * sub_tiles + sub]
                start = jnp.clip((first // SCATTER_TOKENS) * SCATTER_TOKENS, 0, cap - win)
                start = pl.multiple_of(start, SCATTER_TOKENS)
                onehot = jnp.where(rank - start.astype(F32) == col, 1.0, 0.0).astype(BF16)
                contrib = jnp.dot(onehot, ye_ref[ei, pl.ds(start, win), :], preferred_element_type=F32)
            else:
                onehot = jnp.where(rank == col, 1.0, 0.0).astype(BF16)
                contrib = jnp.dot(onehot, ye_ref[ei], preferred_element_type=F32)
            acc = acc + contrib * aff_ref[rows, ei:ei + 1]
        o_ref[rows, :] = x_ref[rows, :] + gate_ref[...] * acc


def _scatter(ye, x1, rank_t, aff_t, mod, cum_flat):
    e, b, cap, d = ye.shape
    n = x1.shape[1]
    tn = min(n, 512)
    dh = d if e * cap * d * 2 <= SCATTER_YE_BYTES else d // 2
    mb = mod.shape[0]
    bsel = (lambda i: i) if mb > 1 else (lambda i: 0)
    return pl.pallas_call(
        _scatter_body,
        grid_spec=pltpu.PrefetchScalarGridSpec(
            num_scalar_prefetch=1,
            grid=(b, d // dh, n // tn),
            in_specs=[
                pl.BlockSpec((e, None, cap, dh), lambda bi, j, i, cum_ref: (0, bi, 0, j)),
                pl.BlockSpec((None, tn, dh), lambda bi, j, i, cum_ref: (bi, i, j)),
                pl.BlockSpec((None, tn, 128), lambda bi, j, i, cum_ref: (bi, i, 0)),
                pl.BlockSpec((None, tn, 128), lambda bi, j, i, cum_ref: (bi, i, 0)),
                pl.BlockSpec((None, None, 1, dh), lambda bi, j, i, cum_ref: (bsel(bi), 5, 0, j)),
            ],
            out_specs=pl.BlockSpec((None, tn, dh), lambda bi, j, i, cum_ref: (bi, i, j)),
        ),
        out_shape=jax.ShapeDtypeStruct(x1.shape, F32),
        compiler_params=_params("parallel", "parallel", "parallel"),
        name="scatter",
    )(cum_flat, ye, x1, rank_t, aff_t, mod)


def _final_norm_body(x_ref, g_ref, o_ref):
    o_ref[...] = _rms(x_ref[...], g_ref[...])


def _final_norm(x, g):
    b, n, d = x.shape
    tm = TOKEN_TILE
    return pl.pallas_call(
        _final_norm_body,
        grid=(b, n // tm),
        in_specs=[pl.BlockSpec((None, tm, d), lambda bi, i: (bi, i, 0)), pl.BlockSpec((1, d), lambda bi, i: (0, 0))],
        out_specs=pl.BlockSpec((None, tm, d), lambda bi, i: (bi, i, 0)),
        out_shape=jax.ShapeDtypeStruct(x.shape, F32),
        compiler_params=_params("parallel", "parallel"),
        name="final_norm",
    )(x, g.reshape(1, d))


def _rope_tables(n_tokens, rot_dim):
    t = jnp.arange(n_tokens, dtype=jnp.int32)
    row = (t // GRID_W).astype(F32)
    col = (t % GRID_W).astype(F32)
    axis_dim = rot_dim // 2
    freqs = ROPE_THETA ** (-jnp.arange(0, axis_dim, 2, dtype=F32) / axis_dim)
    ang = jnp.concatenate([row[:, None] * freqs[None, :], col[:, None] * freqs[None, :]], axis=-1)
    cos, sin = jnp.cos(ang), jnp.sin(ang)
    cos2 = jnp.repeat(cos, 2, axis=-1)
    sin2 = jnp.stack([-sin, sin], axis=-1).reshape(n_tokens, rot_dim)
    pad = 128 - rot_dim
    if pad:
        cos2 = jnp.concatenate([cos2, jnp.ones((n_tokens, pad), F32)], axis=-1)
        sin2 = jnp.concatenate([sin2, jnp.zeros((n_tokens, pad), F32)], axis=-1)
    return cos2, sin2


def _w_in_body(w_ref, o_ref):
    kr_lo = SEG_CKV[1]
    w = w_ref[...]
    rows = w.shape[0]
    o_ref[:, :kr_lo] = w[:, :kr_lo].astype(BF16)
    o_ref[:, kr_lo:SEG_KR[0]] = w[:, kr_lo + ROPE_DIM_B:].astype(BF16)
    o_ref[:, SEG_KR[0]:SEG_KR[0] + ROPE_DIM_B] = w[:, kr_lo:kr_lo + ROPE_DIM_B].astype(BF16)
    o_ref[:, SEG_KR[0] + ROPE_DIM_B:] = jnp.zeros((rows, 128 - ROPE_DIM_B), BF16)


def _reorder_w_in(w_in):
    depth, d, cols = w_in.shape
    rows = TOKEN_TILE
    return pl.pallas_call(
        _w_in_body,
        grid=(depth, d // rows),
        in_specs=[pl.BlockSpec((None, rows, cols), lambda l, i: (l, i, 0))],
        out_specs=pl.BlockSpec((None, rows, IN_COLS_P), lambda l, i: (l, i, 0)),
        out_shape=jax.ShapeDtypeStruct((depth, d, IN_COLS_P), BF16),
        compiler_params=_params("parallel", "parallel"),
        name="reorder_w_in",
    )(w_in)


def _prepare_weights(norm1_g, norm2_g, w_in, qa_norm_g, ka_norm_g, q_norm_b, kv_norm_b, w_q_up, w_kv_up, w_out,
                     w_router):
    depth, d, _ = w_in.shape
    w_in_p = _reorder_w_in(w_in)
    wq = w_q_up.reshape(depth, Q_RANK_B, HEADS_B, NOPE_DIM_B + ROPE_DIM_B)
    wq = jnp.pad(wq, ((0, 0), (0, 0), (0, 0), (0, QB_PAD - NOPE_DIM_B - ROPE_DIM_B)))
    wkv = w_kv_up.reshape(depth, KV_RANK_B, HEADS_B, NOPE_DIM_B + V_DIM_B)
    return {
        "norm1_g": norm1_g.reshape(depth, 1, d),
        "norm2_g": norm2_g.reshape(depth, 1, d),
        "w_in": w_in_p,
        "qa_g": qa_norm_g.reshape(depth, 1, HEAD_DIM),
        "ka_g": ka_norm_g.reshape(depth, 1, HEAD_DIM),
        "qn_g": q_norm_b.reshape(depth, 1, Q_RANK_B),
        "kvn_g": kv_norm_b.reshape(depth, 1, KV_RANK_B),
        "w_q_up": wq.reshape(depth, Q_RANK_B, HEADS_B * QB_PAD).astype(BF16),
        "w_kv_k": wkv[..., :NOPE_DIM_B].reshape(depth, KV_RANK_B, HEADS_B * NOPE_DIM_B).astype(BF16),
        "w_kv_v": wkv[..., NOPE_DIM_B:].reshape(depth, KV_RANK_B, HEADS_B * V_DIM_B).astype(BF16),
        "w_out": w_out.astype(BF16),
        "w_router": jnp.pad(w_router, ((0, 0), (0, 0), (0, 128 - N_EXPERTS))).astype(BF16),
    }


def _moe_select(h2, logits_t):
    rank, rank_t, aff_t, cum = _route(logits_t)
    n = rank.shape[2]
    cum_flat = cum[:, :, :n // SCATTER_TOKENS + 1].reshape(-1)
    xe = _gather(h2, rank.reshape(rank.shape[0], rank.shape[1], 1, n), cum_flat)
    return xe, rank_t, aff_t, cum_flat


def kernel(x_prompt, x_sample, cache_a_k, cache_a_v, cache_b_ckv, cache_b_krope, cache_c_k, cache_c_v, c, c_ctx, w_ada, b_ada, norm1_g, norm2_g, w_in, qa_norm_g, ka_norm_g, q_norm_b, kv_norm_b, w_q_up, w_kv_up, na_bias, w_out, w_router, w_gate, w_up, w_down, final_norm_g):
    depth, d, _ = w_in.shape
    bp, seq, _ = x_prompt.shape
    bs, n_lat, _ = x_sample.shape
    past = cache_a_k.shape[2]
    rows = n_lat // GRID_W

    wts = _prepare_weights(norm1_g, norm2_g, w_in, qa_norm_g, ka_norm_g, q_norm_b, kv_norm_b, w_q_up, w_kv_up,
                           w_out, w_router)
    rope_tabs = _rope_tables(n_lat, HEAD_DIM) + _rope_tables(n_lat, ROPE_DIM_B)

    cond8 = jnp.concatenate([c_ctx[None], c, jnp.zeros((8 - 1 - bs, d), F32)], axis=0)
    mods = _modulation(cond8, w_ada, b_ada)
    mods = mods.reshape(depth, 8, 6, 1, d)

    ca_k = cache_a_k.reshape(bs, depth, past, KV_A).astype(BF16)
    ca_v = jnp.swapaxes(cache_a_v.reshape(bs, depth, past, KV_A), 2, 3).astype(BF16)
    cc_k = cache_c_k.reshape(bs, depth, past, WIDTH_C).astype(BF16)
    cc_v = cache_c_v.reshape(bs, depth, past, WIDTH_C).astype(BF16)
    cb_k, cb_v = _cache_mla_kv(cache_b_ckv, cache_b_krope, wts["w_kv_k"], wts["w_kv_v"])
    nat_bias = _nat_bias(na_bias, rows)

    xp, xs = x_prompt, x_sample
    states = ()
    for l in range(depth):
        mod_c = mods[l, 0:1]
        mod_l = mods[l, 1:1 + bs]

        qa, ka, va, qb, kb, vb, qc, kc, vc, *states = _input_side(xp, mod_c, l, wts, None, states)
        o_a, o_b, o_c = _ctx_attention(qa, ka, va, qb, kb, vb, qc, kc, vc)
        x1, h2, lg = _output_side(o_a, o_b, o_c, xp, mod_c, l, wts)
        xe, rank_t, aff_t, cum_flat = _moe_select(h2, lg)
        ye, wg16, wu16, wd16 = _expert_ffn_cast(xe, l, w_gate, w_up, w_down)
        xp = _scatter(ye, x1, rank_t, aff_t, mod_c, cum_flat)

        qa, ka, va, qb, kb, vb, qc, kc, vc = _input_side(xs, mod_l, l, wts, rope_tabs, None)
        o_a = _attention(qa, ka, va, ca_k, ca_v, l, HEADS_A, HEADS_A // KV_HEADS_A, HEAD_DIM, HEAD_DIM)
        o_b = _attention(qb, kb, vb, cb_k, cb_v, l, HEADS_B, 1, QB_PAD, V_DIM_B)
        o_c = _neighbourhood_attention(qc, kc, vc, cc_k, cc_v, nat_bias, l)
        x1, h2, lg = _output_side(o_a, o_b, o_c, xs, mod_l, l, wts)
        xe, rank_t, aff_t, cum_flat = _moe_select(h2, lg)
        ye = _expert_ffn(xe, wg16, wu16, wd16)
        xs = _scatter(ye, x1, rank_t, aff_t, mod_l, cum_flat)

    y_prompt = _final_norm(xp, final_norm_g)
    y_sample = _final_norm(xs, final_norm_g)
    st_ka, st_va, new_b_ckv, new_b_krope, st_kc, st_vc = states
    new_a_k = st_ka.reshape(bp, depth, seq, KV_HEADS_A, HEAD_DIM)
    new_a_v = st_va.reshape(bp, depth, seq, KV_HEADS_A, HEAD_DIM)
    new_c_k = st_kc.reshape(bp, depth, seq, HEADS_C, HEAD_DIM)
    new_c_v = st_vc.reshape(bp, depth, seq, HEADS_C, HEAD_DIM)
    return (y_prompt, y_sample, new_a_k, new_a_v, new_b_ckv, new_b_krope, new_c_k, new_c_v)
```

```python
import functools

import jax
import jax.numpy as jnp
from jax import lax
from jax.experimental import pallas as pl
from jax.experimental.pallas import tpu as pltpu

F32 = jnp.float32
BF16 = jnp.bfloat16

GRID_W = 64
HEAD_DIM = 128
HEADS_A = 6
KV_HEADS_A = 2
HEADS_B = 5
Q_RANK_B = 512
KV_RANK_B = 256
NOPE_DIM_B = 128
ROPE_DIM_B = 64
V_DIM_B = 128
HEADS_C = 5
WIN_ROWS_MAX = 8
WIN_COLS = 16
N_EXPERTS = 16
CAPACITY_FACTOR = 2
ROPE_THETA = 10000.0
EPS = 1e-6

QB_PAD = 256
WIDTH_A = HEADS_A * HEAD_DIM
WIDTH_B = HEADS_B * V_DIM_B
WIDTH_C = HEADS_C * HEAD_DIM
KV_A = KV_HEADS_A * HEAD_DIM

SEG_QA = (0, WIDTH_A)
SEG_KA = (SEG_QA[1], SEG_QA[1] + KV_A)
SEG_VA = (SEG_KA[1], SEG_KA[1] + KV_A)
SEG_CQ = (SEG_VA[1], SEG_VA[1] + Q_RANK_B)
SEG_CKV = (SEG_CQ[1], SEG_CQ[1] + KV_RANK_B)
SEG_QC = (SEG_CKV[1], SEG_CKV[1] + WIDTH_C)
SEG_KC = (SEG_QC[1], SEG_QC[1] + WIDTH_C)
SEG_VC = (SEG_KC[1], SEG_KC[1] + WIDTH_C)
SEG_KR = (SEG_VC[1], SEG_VC[1] + 128)
IN_COLS_P = SEG_KR[1]

STATE_WIDTHS = (KV_A, KV_A, KV_RANK_B, ROPE_DIM_B, WIDTH_C, WIDTH_C)

NAT_QROWS = 4
NAT_KROWS = 12
MASK_VALUE = -1e30
LOG2E = 1.4426950408889634

TOKEN_TILE = 256
PROJ_TILE = 512
OUT_GROUPS = 2
ATTN_ROWS = 512
ATTN_GROUPS = 2
NAT_GROUPS = 16
GATHER_CHUNK = 512
GATHER_ROWS = 128
SCATTER_TOKENS = 128
ROUTE_TOKENS = 4096
SCATTER_YE_BYTES = 16 * 1024 * 1024
FF_SLICE = 512
VMEM_LIMIT = 56 * 1024 * 1024
NT_DIMS = (((1,), (1,)), ((), ()))


def _params(*sem):
    return pltpu.CompilerParams(dimension_semantics=sem, vmem_limit_bytes=VMEM_LIMIT)


def _resident(block_shape, index_map):
    return pl.BlockSpec(block_shape, index_map, pipeline_mode=pl.Buffered(1))


def _rms(x, g):
    ms = jnp.mean(x * x, axis=-1, keepdims=True)
    return x * lax.rsqrt(ms + EPS) * g


def _swap_pairs(x):
    lane = lax.broadcasted_iota(jnp.int32, x.shape, x.ndim - 1)
    nxt = pltpu.roll(x, x.shape[-1] - 1, x.ndim - 1)
    prv = pltpu.roll(x, 1, x.ndim - 1)
    return jnp.where((lane & 1) == 0, nxt, prv)


def _rope(x, cos, sin_signed):
    return x * cos + _swap_pairs(x) * sin_signed


def _mod_body(c_ref, w_ref, b_ref, o_ref):
    c = c_ref[...]
    s = (c / (1.0 + jnp.exp(-c))).astype(BF16)
    o_ref[...] = jnp.dot(s, w_ref[...].astype(BF16), preferred_element_type=F32) + b_ref[...]


def _modulation(cond8, w_ada, b_ada):
    depth, d, cols = w_ada.shape
    tn = 1024
    return pl.pallas_call(
        _mod_body,
        grid=(depth, cols // tn),
        in_specs=[
            pl.BlockSpec((8, d), lambda l, j: (0, 0)),
            pl.BlockSpec((None, d, tn), lambda l, j: (l, 0, j)),
            pl.BlockSpec((None, 1, tn), lambda l, j: (l, 0, j)),
        ],
        out_specs=pl.BlockSpec((None, 8, tn), lambda l, j: (l, 0, j)),
        out_shape=jax.ShapeDtypeStruct((depth, 8, cols), F32),
        compiler_params=_params("parallel", "parallel"),
        name="modulation",
    )(cond8, w_ada, b_ada.reshape(depth, 1, cols))


def _in_body(*refs, rope, state, carried, layer):
    it = iter(refs)
    x_ref, sh_ref, sc_ref, g_ref, win_ref = (next(it) for _ in range(5))
    qag_ref, kag_ref, qng_ref, kvng_ref = (next(it) for _ in range(4))
    wq_ref, wkk_ref, wkv_ref = (next(it) for _ in range(3))
    if rope:
        ca_ref, sa_ref, cb_ref, sb_ref = (next(it) for _ in range(4))
    for _ in range(carried):
        next(it)
    qa_o, ka_o, va_o, qb_o, kb_o, vb_o, qc_o, kc_o, vc_o = (next(it) for _ in range(9))
    if state:
        st_ka, st_va, st_ckv, st_kr, st_kc, st_vc = (next(it) for _ in range(6))

    def put_state(ref, value, lo=0):
        w = value.shape[1]
        if carried:
            ref[:, lo:lo + w] = value
        else:
            for l in range(ref.shape[0]):
                ref[l, :, lo:lo + w] = value if l == layer else jnp.zeros_like(value)

    h = _rms(x_ref[...], g_ref[...]) * (1.0 + sc_ref[...]) + sh_ref[...]
    hb = h.astype(BF16)

    def seg(bounds):
        return jnp.dot(hb, win_ref[:, bounds[0]:bounds[1]], preferred_element_type=F32)

    def rope_a(y):
        return _rope(y, ca_ref[...], sa_ref[...]) if rope else y

    def rope_b(y):
        return _rope(y, cb_ref[...], sb_ref[...]) if rope else y

    qa = seg(SEG_QA)
    for hd in range(HEADS_A):
        lo = hd * HEAD_DIM
        y = rope_a(_rms(qa[:, lo:lo + HEAD_DIM], qag_ref[...]))
        qa_o[:, lo:lo + HEAD_DIM] = (y * (HEAD_DIM ** -0.5 * LOG2E)).astype(BF16)
    ka = seg(SEG_KA)
    for hd in range(KV_HEADS_A):
        lo = hd * HEAD_DIM
        y = rope_a(_rms(ka[:, lo:lo + HEAD_DIM], kag_ref[...]))
        ka_o[:, lo:lo + HEAD_DIM] = y.astype(BF16)
        if state:
            put_state(st_ka, y, lo)
    va = seg(SEG_VA)
    va_o[...] = (va if state else va.T).astype(BF16)

    cq = _rms(seg(SEG_CQ), qng_ref[...]).astype(BF16)
    qb = jnp.dot(cq, wq_ref[...], preferred_element_type=F32)
    qscale = (NOPE_DIM_B + ROPE_DIM_B) ** -0.5 * LOG2E
    for hd in range(HEADS_B):
        lo = hd * QB_PAD
        qb_o[:, lo:lo + 128] = (qb[:, lo:lo + 128] * qscale).astype(BF16)
        qb_o[:, lo + 128:lo + 256] = (rope_b(qb[:, lo + 128:lo + 256]) * qscale).astype(BF16)
    ckv = _rms(seg(SEG_CKV), kvng_ref[...])
    ckvb = ckv.astype(BF16)
    kr = rope_b(seg(SEG_KR))
    krb = kr.astype(BF16)
    kn = jnp.dot(ckvb, wkk_ref[...], preferred_element_type=F32)
    for hd in range(HEADS_B):
        kb_o[:, hd * QB_PAD:hd * QB_PAD + 128] = kn[:, hd * 128:(hd + 1) * 128].astype(BF16)
        kb_o[:, hd * QB_PAD + 128:(hd + 1) * QB_PAD] = krb
    vb = jnp.dot(ckvb, wkv_ref[...], preferred_element_type=F32)
    vb_o[...] = (vb if state else vb.T).astype(BF16)

    qc_o[...] = (seg(SEG_QC) * (HEAD_DIM ** -0.5 * LOG2E)).astype(BF16)
    kc = seg(SEG_KC)
    kc_o[...] = kc.astype(BF16)
    vc = seg(SEG_VC)
    vc_o[...] = vc.astype(BF16)

    if state:
        put_state(st_va, va)
        put_state(st_ckv, ckv)
        put_state(st_kr, kr[:, :ROPE_DIM_B])
        put_state(st_kc, kc)
        put_state(st_vc, vc)


def _input_side(x, mod, layer, wts, rope_tabs, states):
    state = states is not None
    b, n, d = x.shape
    tm = min(n, PROJ_TILE)
    mb = mod.shape[0]
    bsel = (lambda i: i) if mb > 1 else (lambda i: 0)
    rope = rope_tabs is not None

    def modspec(k):
        return pl.BlockSpec((None, None, 1, d), lambda bi, i: (bsel(bi), k, 0, 0))

    def vec(w):
        return pl.BlockSpec((None, 1, w), lambda bi, i: (layer, 0, 0))

    in_specs = [
        pl.BlockSpec((None, tm, d), lambda bi, i: (bi, i, 0)),
        modspec(0), modspec(1), vec(d),
        _resident((None, d, IN_COLS_P), lambda bi, i: (layer, 0, 0)),
        vec(HEAD_DIM), vec(HEAD_DIM), vec(Q_RANK_B), vec(KV_RANK_B),
        _resident((None, Q_RANK_B, HEADS_B * QB_PAD), lambda bi, i: (layer, 0, 0)),
        _resident((None, KV_RANK_B, HEADS_B * 128), lambda bi, i: (layer, 0, 0)),
        _resident((None, KV_RANK_B, HEADS_B * 128), lambda bi, i: (layer, 0, 0)),
    ]
    args = [x, mod, mod, wts["norm1_g"], wts["w_in"], wts["qa_g"], wts["ka_g"], wts["qn_g"], wts["kvn_g"],
            wts["w_q_up"], wts["w_kv_k"], wts["w_kv_v"]]
    if rope:
        in_specs += [pl.BlockSpec((tm, 128), lambda bi, i: (i, 0))] * 4
        args += list(rope_tabs)

    widths = [WIDTH_A, KV_A, KV_A, HEADS_B * QB_PAD, HEADS_B * QB_PAD, WIDTH_B, WIDTH_C, WIDTH_C, WIDTH_C]
    out_shape = [jax.ShapeDtypeStruct((b, n, w), BF16) for w in widths]
    out_specs = [pl.BlockSpec((None, tm, w), lambda bi, i: (bi, i, 0)) for w in widths]
    if not state:
        for k in (2, 5):
            out_shape[k] = jax.ShapeDtypeStruct((b, widths[k], n), BF16)
            out_specs[k] = pl.BlockSpec((None, widths[k], tm), lambda bi, i: (bi, 0, i))
    aliases = {}
    if state:
        depth = wts["w_in"].shape[0]
        for w in STATE_WIDTHS:
            out_shape.append(jax.ShapeDtypeStruct((b, depth, n, w), F32))
            if states:
                out_specs.append(pl.BlockSpec((None, None, tm, w), lambda bi, i: (bi, layer, i, 0)))
            else:
                out_specs.append(pl.BlockSpec((None, depth, tm, w), lambda bi, i: (bi, 0, i, 0)))
        for k, buf in enumerate(states):
            aliases[len(args)] = len(widths) + k
            in_specs.append(pl.BlockSpec(memory_space=pl.ANY))
            args.append(buf)

    return pl.pallas_call(
        functools.partial(_in_body, rope=rope, state=state, carried=len(aliases), layer=layer),
        grid=(b, n // tm),
        in_specs=in_specs,
        out_specs=out_specs,
        out_shape=out_shape,
        input_output_aliases=aliases,
        compiler_params=_params("parallel", "parallel"),
        name="input_side",
    )(*args)


def _attn_body(q_ref, ks_ref, vs_ref, kc_ref, vc_ref, o_ref):
    rows = min(q_ref.shape[0], ATTN_ROWS)
    groups = q_ref.shape[0] // rows

    def scores(c):
        q = q_ref[c * rows:(c + 1) * rows, :]
        s_self = lax.dot_general(ks_ref[...], q, NT_DIMS, preferred_element_type=F32)
        s_ctx = lax.dot_general(kc_ref[...], q, NT_DIMS, preferred_element_type=F32)
        m = jnp.maximum(jnp.max(s_self, axis=0, keepdims=True), jnp.max(s_ctx, axis=0, keepdims=True))
        return s_self, s_ctx, m

    def finish(c, s_self, s_ctx, m):
        p_self = jnp.exp2(s_self - m)
        p_ctx = jnp.exp2(s_ctx - m)
        l = jnp.sum(p_self, axis=0, keepdims=True) + jnp.sum(p_ctx, axis=0, keepdims=True)
        acc = jnp.dot(vs_ref[...], p_self.astype(BF16), preferred_element_type=F32)
        acc = acc + jnp.dot(vc_ref[...], p_ctx.astype(BF16), preferred_element_type=F32)
        o_ref[c * rows:(c + 1) * rows, :] = (acc / l).T.astype(o_ref.dtype)

    pending = scores(0)
    for c in range(groups):
        following = scores(c + 1) if c + 1 < groups else None
        finish(c, *pending)
        pending = following


def _ctx_attn_body(qa_ref, ka_ref, va_ref, qb_ref, kb_ref, vb_ref, qc_ref, kc_ref, vc_ref, oa_ref, ob_ref, oc_ref):
    def head(q, k, v):
        s = lax.dot_general(q, k, NT_DIMS, preferred_element_type=F32)
        p = jnp.exp2(s - jnp.max(s, axis=-1, keepdims=True))
        acc = jnp.dot(p.astype(BF16), v, preferred_element_type=F32)
        return (acc / jnp.sum(p, axis=-1, keepdims=True)).astype(BF16)

    def lanes(ref, i, w):
        return ref[:, i * w:(i + 1) * w]

    for h in range(HEADS_A):
        g = h // (HEADS_A // KV_HEADS_A)
        oa_ref[:, h * HEAD_DIM:(h + 1) * HEAD_DIM] = head(
            lanes(qa_ref, h, HEAD_DIM), lanes(ka_ref, g, HEAD_DIM), lanes(va_ref, g, HEAD_DIM))
    for h in range(HEADS_B):
        ob_ref[:, h * V_DIM_B:(h + 1) * V_DIM_B] = head(
            lanes(qb_ref, h, QB_PAD), lanes(kb_ref, h, QB_PAD), lanes(vb_ref, h, V_DIM_B))
    for h in range(HEADS_C):
        oc_ref[:, h * HEAD_DIM:(h + 1) * HEAD_DIM] = head(
            lanes(qc_ref, h, HEAD_DIM), lanes(kc_ref, h, HEAD_DIM), lanes(vc_ref, h, HEAD_DIM))


def _ctx_attention(qa, ka, va, qb, kb, vb, qc, kc, vc):
    b, n, _ = qa.shape
    args = [qa, ka, va, qb, kb, vb, qc, kc, vc]
    widths = [WIDTH_A, WIDTH_B, WIDTH_C]
    return pl.pallas_call(
        _ctx_attn_body,
        grid=(b,),
        in_specs=[pl.BlockSpec((None, n, a.shape[2]), lambda bi: (bi, 0, 0)) for a in args],
        out_specs=[pl.BlockSpec((None, n, w), lambda bi: (bi, 0, 0)) for w in widths],
        out_shape=[jax.ShapeDtypeStruct((b, n, w), BF16) for w in widths],
        compiler_params=_params("parallel"),
        name="ctx_attention",
    )(*args)


def _attention(q, ks, vs_t, kc, vc_t, layer, heads, group, dq, dv):
    b, n, _ = q.shape
    ms = ks.shape[1]
    mc = kc.shape[2]
    tq = min(n, ATTN_GROUPS * ATTN_ROWS)
    return pl.pallas_call(
        _attn_body,
        grid=(b, heads, n // tq),
        in_specs=[
            pl.BlockSpec((None, tq, dq), lambda bi, h, i: (bi, i, h)),
            pl.BlockSpec((None, ms, dq), lambda bi, h, i: (bi, 0, h // group)),
            pl.BlockSpec((None, dv, ms), lambda bi, h, i: (bi, h // group, 0)),
            pl.BlockSpec((None, None, mc, dq), lambda bi, h, i: (bi, layer, 0, h // group)),
            pl.BlockSpec((None, None, dv, mc), lambda bi, h, i: (bi, layer, h // group, 0)),
        ],
        out_specs=pl.BlockSpec((None, tq, dv), lambda bi, h, i: (bi, i, h)),
        out_shape=jax.ShapeDtypeStruct((b, n, heads * dv), BF16),
        compiler_params=_params("parallel", "parallel", "parallel"),
        name="attention",
    )(q, ks, vs_t, kc, vc_t)


def _nat_body(q_ref, ks_ref, vs_ref, kc_ref, vc_ref, bias_ref, o_ref):
    tq = NAT_QROWS * GRID_W
    nk = NAT_KROWS * GRID_W
    groups = q_ref.shape[0] // tq
    key_rows = ks_ref.shape[0] // GRID_W
    last = ks_ref.shape[0] // tq - 1

    def scores(c):
        r = pl.program_id(2) * groups + c
        kr0 = jnp.clip(NAT_QROWS * r - WIN_ROWS_MAX // 2, 0, key_rows - NAT_KROWS)
        start = pl.multiple_of(kr0 * GRID_W, tq)
        kind = jnp.where(r == 0, 0, jnp.where(r == last, 2, 1))
        q = q_ref[c * tq:(c + 1) * tq, :]
        s_win = lax.dot_general(q, ks_ref[pl.ds(start, nk), :], NT_DIMS, preferred_element_type=F32)
        s_win = s_win + bias_ref[kind]
        s_ctx = lax.dot_general(q, kc_ref[...], NT_DIMS, preferred_element_type=F32)
        m = jnp.maximum(jnp.max(s_win, axis=-1, keepdims=True), jnp.max(s_ctx, axis=-1, keepdims=True))
        return s_win, s_ctx, m, start

    def finish(c, s_win, s_ctx, m, start):
        p_win = jnp.exp2(s_win - m)
        p_ctx = jnp.exp2(s_ctx - m)
        l = jnp.sum(p_win, axis=-1, keepdims=True) + jnp.sum(p_ctx, axis=-1, keepdims=True)
        acc = jnp.dot(p_win.astype(BF16), vs_ref[pl.ds(start, nk), :], preferred_element_type=F32)
        acc = acc + jnp.dot(p_ctx.astype(BF16), vc_ref[...], preferred_element_type=F32)
        o_ref[c * tq:(c + 1) * tq, :] = (acc / l).astype(o_ref.dtype)

    pending = scores(0)
    for c in range(groups):
        following = scores(c + 1) if c + 1 < groups else None
        finish(c, *pending)
        pending = following


N_DROW = 2 * WIN_ROWS_MAX - 1
N_DCOL = 2 * WIN_COLS - 1


def _nat_bias_body(b_ref, o_ref, *, rows):
    base = (pl.program_id(0) * HEADS_C + pl.program_id(1)) * (N_DROW * N_DCOL)
    c = lax.broadcasted_iota(jnp.int32, (GRID_W, GRID_W), 0)
    kc = lax.broadcasted_iota(jnp.int32, (GRID_W, GRID_W), 1)
    c0 = jnp.clip(c - WIN_COLS // 2, 0, GRID_W - WIN_COLS)
    in_window = (kc >= c0) & (kc < c0 + WIN_COLS)
    dc = kc - c + (WIN_COLS - 1)
    masked = jnp.full((GRID_W, GRID_W), MASK_VALUE, F32)
    tables = {}

    def table(dr):
        if dr not in tables:
            t = jnp.zeros((GRID_W, GRID_W), F32)
            for j in range(N_DCOL):
                t = jnp.where(dc == j, b_ref[base + dr * N_DCOL + j] * LOG2E, t)
            tables[dr] = jnp.where(in_window, t, MASK_VALUE)
        return tables[dr]

    kh = min(WIN_ROWS_MAX, rows)
    for ty, blk in enumerate((0, 1, rows // NAT_QROWS - 1)):
        r_first = NAT_QROWS * blk
        kr0 = min(max(r_first - WIN_ROWS_MAX // 2, 0), rows - NAT_KROWS)
        for a in range(NAT_QROWS):
            r = r_first + a
            r0 = min(max(r - kh // 2, 0), rows - kh)
            for i in range(NAT_KROWS):
                kr = kr0 + i
                blkval = table(kr - r + WIN_ROWS_MAX - 1) if r0 <= kr < r0 + kh else masked
                o_ref[ty, a * GRID_W:(a + 1) * GRID_W, i * GRID_W:(i + 1) * GRID_W] = blkval


def _nat_bias(na_bias, rows):
    depth = na_bias.shape[0]
    tq, nk = NAT_QROWS * GRID_W, NAT_KROWS * GRID_W
    return pl.pallas_call(
        functools.partial(_nat_bias_body, rows=rows),
        grid=(depth, HEADS_C),
        in_specs=[pl.BlockSpec(memory_space=pltpu.SMEM)],
        out_specs=pl.BlockSpec((None, None, 3, tq, nk), lambda l, h: (l, h, 0, 0, 0)),
        out_shape=jax.ShapeDtypeStruct((depth, HEADS_C, 3, tq, nk), F32),
        compiler_params=_params("parallel", "parallel"),
        name="nat_bias",
    )(na_bias.reshape(-1))


def _neighbourhood_attention(q, ks, vs, kc, vc, bias, layer):
    b, n, _ = q.shape
    nblk = n // (NAT_QROWS * GRID_W)
    groups = min(nblk, NAT_GROUPS)
    tq = groups * NAT_QROWS * GRID_W
    mc = kc.shape[2]
    d = HEAD_DIM

    return pl.pallas_call(
        _nat_body,
        grid=(b, HEADS_C, nblk // groups),
        in_specs=[
            pl.BlockSpec((None, tq, d), lambda bi, h, r: (bi, r, h)),
            pl.BlockSpec((None, n, d), lambda bi, h, r: (bi, 0, h)),
            pl.BlockSpec((None, n, d), lambda bi, h, r: (bi, 0, h)),
            pl.BlockSpec((None, None, mc, d), lambda bi, h, r: (bi, layer, 0, h)),
            pl.BlockSpec((None, None, mc, d), lambda bi, h, r: (bi, layer, 0, h)),
            pl.BlockSpec((None, None, 3, NAT_QROWS * GRID_W, NAT_KROWS * GRID_W),
                         lambda bi, h, r: (layer, h, 0, 0, 0)),
        ],
        out_specs=pl.BlockSpec((None, tq, d), lambda bi, h, r: (bi, r, h)),
        out_shape=jax.ShapeDtypeStruct((b, n, HEADS_C * d), BF16),
        compiler_params=_params("parallel", "parallel", "arbitrary"),
        name="neighbourhood_attention",
    )(q, ks, vs, kc, vc, bias)


def _cache_kv_body(ckv_ref, kr_ref, wkk_ref, wkv_ref, kb_o, vb_o):
    ckvb = ckv_ref[...].astype(BF16)
    krb = kr_ref[...].astype(BF16)
    kn = jnp.dot(ckvb, wkk_ref[...], preferred_element_type=F32)
    zeros = jnp.zeros((krb.shape[0], QB_PAD - 128 - ROPE_DIM_B), BF16)
    for hd in range(HEADS_B):
        lo = hd * QB_PAD
        kb_o[:, lo:lo + 128] = kn[:, hd * 128:(hd + 1) * 128].astype(BF16)
        kb_o[:, lo + 128:lo + 128 + ROPE_DIM_B] = krb
        kb_o[:, lo + 128 + ROPE_DIM_B:lo + QB_PAD] = zeros
    vb_o[...] = jnp.dot(ckvb, wkv_ref[...], preferred_element_type=F32).T.astype(BF16)


def _cache_mla_kv(cache_ckv, cache_krope, w_kv_k, w_kv_v):
    b, depth, m, _ = cache_ckv.shape
    return pl.pallas_call(
        _cache_kv_body,
        grid=(b, depth),
        in_specs=[
            pl.BlockSpec((None, None, m, KV_RANK_B), lambda bi, l: (bi, l, 0, 0)),
            pl.BlockSpec((None, None, m, ROPE_DIM_B), lambda bi, l: (bi, l, 0, 0)),
            pl.BlockSpec((None, KV_RANK_B, HEADS_B * 128), lambda bi, l: (l, 0, 0)),
            pl.BlockSpec((None, KV_RANK_B, HEADS_B * 128), lambda bi, l: (l, 0, 0)),
        ],
        out_specs=[
            pl.BlockSpec((None, None, m, HEADS_B * QB_PAD), lambda bi, l: (bi, l, 0, 0)),
            pl.BlockSpec((None, None, WIDTH_B, m), lambda bi, l: (bi, l, 0, 0)),
        ],
        out_shape=[
            jax.ShapeDtypeStruct((b, depth, m, HEADS_B * QB_PAD), BF16),
            jax.ShapeDtypeStruct((b, depth, WIDTH_B, m), BF16),
        ],
        compiler_params=_params("parallel", "parallel"),
        name="cache_mla_kv",
    )(cache_ckv, cache_krope, w_kv_k, w_kv_v)


def _out_body(oa_ref, ob_ref, oc_ref, wo_ref, x_ref, gate_ref, sh_ref, sc_ref, g_ref, wr_ref, x1_o, h2_o, lg_o):
    lo_b, lo_c = WIDTH_A, WIDTH_A + WIDTH_B
    rows = min(x_ref.shape[0], TOKEN_TILE)
    groups = x_ref.shape[0] // rows

    def project(c):
        r = slice(c * rows, (c + 1) * rows)
        o = jnp.dot(oa_ref[r, :], wo_ref[:lo_b, :], preferred_element_type=F32)
        o = o + jnp.dot(ob_ref[r, :], wo_ref[lo_b:lo_c, :], preferred_element_type=F32)
        return o + jnp.dot(oc_ref[r, :], wo_ref[lo_c:, :], preferred_element_type=F32)

    def finish(c, o):
        r = slice(c * rows, (c + 1) * rows)
        x1 = x_ref[r, :] + gate_ref[...] * o
        x1_o[r, :] = x1
        h = _rms(x1, g_ref[...]) * (1.0 + sc_ref[...]) + sh_ref[...]
        hb = h.astype(BF16)
        h2_o[r, :] = hb
        lg = jnp.dot(hb, wr_ref[...], preferred_element_type=F32)
        for k in range(rows // 128):
            col = c * rows + k * 128
            lg_o[:, col:col + 128] = lg[k * 128:(k + 1) * 128, :].T[:N_EXPERTS, :]

    pending = project(0)
    for c in range(groups):
        following = project(c + 1) if c + 1 < groups else None
        finish(c, pending)
        pending = following


def _output_side(o_a, o_b, o_c, x, mod, layer, wts):
    b, n, d = x.shape
    tm = min(n, OUT_GROUPS * TOKEN_TILE)
    mb = mod.shape[0]
    bsel = (lambda i: i) if mb > 1 else (lambda i: 0)

    def modspec(k):
        return pl.BlockSpec((None, None, 1, d), lambda bi, i: (bsel(bi), k, 0, 0))

    def tok(w):
        return pl.BlockSpec((None, tm, w), lambda bi, i: (bi, i, 0))

    return pl.pallas_call(
        _out_body,
        grid=(b, n // tm),
        in_specs=[
            tok(WIDTH_A), tok(WIDTH_B), tok(WIDTH_C),
            _resident((None, WIDTH_A + WIDTH_B + WIDTH_C, d), lambda bi, i: (layer, 0, 0)),
            tok(d), modspec(2), modspec(3), modspec(4),
            pl.BlockSpec((None, 1, d), lambda bi, i: (layer, 0, 0)),
            pl.BlockSpec((None, d, 128), lambda bi, i: (layer, 0, 0)),
        ],
        out_specs=[tok(d), tok(d), pl.BlockSpec((None, N_EXPERTS, tm), lambda bi, i: (bi, 0, i))],
        out_shape=[
            jax.ShapeDtypeStruct((b, n, d), F32),
            jax.ShapeDtypeStruct((b, n, d), BF16),
            jax.ShapeDtypeStruct((b, N_EXPERTS, n), F32),
        ],
        compiler_params=_params("parallel", "parallel"),
        name="output_side",
    )(o_a, o_b, o_c, wts["w_out"], x, mod, mod, mod, wts["norm2_g"], wts["w_router"])


def _prefix_exclusive(mask):
    e, n = mask.shape
    ones = jnp.where(mask, 1.0, 0.0)
    rr = lax.broadcasted_iota(jnp.int32, (128, 128), 0)
    cc = lax.broadcasted_iota(jnp.int32, (128, 128), 1)
    tri = jnp.where(rr <= cc, 1.0, 0.0).astype(BF16)
    carry = jnp.zeros((e, 1), F32)
    outs = []
    for c in range(n // 128):
        blk = ones[:, c * 128:(c + 1) * 128]
        inc = jnp.dot(blk.astype(BF16), tri, preferred_element_type=F32)
        outs.append(inc - blk + carry)
        carry = carry + inc[:, 127:128]
    return jnp.concatenate(outs, axis=1)


def _router_body(lg_ref, rank_o, rank_t_o, aff_t_o, cum_o, *, cap, chunk):
    bb, e, n = lg_ref.shape
    lg = lg_ref[...]
    ex = jnp.exp(lg - jnp.max(lg, axis=1, keepdims=True))
    aff = (ex / jnp.sum(ex, axis=1, keepdims=True)).reshape(bb * e, n)
    key = pltpu.bitcast(aff, jnp.int32)

    def step(i, t):
        cand = t | lax.shift_left(jnp.int32(1), 30 - i)
        cnt = jnp.sum(jnp.where(key >= cand, 1.0, 0.0), axis=1, keepdims=True)
        return jnp.where(cnt >= cap, cand, t)

    thr = lax.fori_loop(0, 31, step, jnp.zeros((bb * e, 1), jnp.int32))
    above = key > thr
    tied = key == thr
    need = cap - jnp.sum(jnp.where(above, 1.0, 0.0), axis=1, keepdims=True)
    chosen = above | (tied & (_prefix_exclusive(tied) < need))
    before = _prefix_exclusive(chosen)
    rank = jnp.where(chosen, before, -1.0)
    rank_o[...] = rank.astype(jnp.int32).reshape(bb, e, n)

    lane = lax.broadcasted_iota(jnp.int32, (bb * e, 128), 1)
    cum = jnp.full((bb * e, 128), float(cap), F32)
    for k in range(n // chunk):
        cum = jnp.where(lane == k, before[:, k * chunk:k * chunk + 1], cum)
    cum_o[...] = cum.astype(jnp.int32).reshape(bb, e, 128)

    fill = jnp.full((128 - e, n), -1.0, F32)
    for bi in range(bb):
        rank_p = jnp.concatenate([rank[bi * e:(bi + 1) * e], fill], axis=0)
        aff_p = jnp.concatenate([aff[bi * e:(bi + 1) * e], fill], axis=0)
        for c in range(n // 128):
            rank_t_o[bi, c * 128:(c + 1) * 128, :] = rank_p[:, c * 128:(c + 1) * 128].T
            aff_t_o[bi, c * 128:(c + 1) * 128, :] = aff_p[:, c * 128:(c + 1) * 128].T


def _route(logits_t):
    b, e, n = logits_t.shape
    cap = CAPACITY_FACTOR * n // e
    bb = max(1, min(b, ROUTE_TOKENS // n))
    return pl.pallas_call(
        functools.partial(_router_body, cap=cap, chunk=SCATTER_TOKENS),
        grid=(b // bb,),
        in_specs=[pl.BlockSpec((bb, e, n), lambda bi: (bi, 0, 0))],
        out_specs=[
            pl.BlockSpec((bb, e, n), lambda bi: (bi, 0, 0)),
            pl.BlockSpec((bb, n, 128), lambda bi: (bi, 0, 0)),
            pl.BlockSpec((bb, n, 128), lambda bi: (bi, 0, 0)),
            pl.BlockSpec((bb, e, 128), lambda bi: (bi, 0, 0)),
        ],
        out_shape=[
            jax.ShapeDtypeStruct((b, e, n), jnp.int32),
            jax.ShapeDtypeStruct((b, n, 128), F32),
            jax.ShapeDtypeStruct((b, n, 128), F32),
            jax.ShapeDtypeStruct((b, e, 128), jnp.int32),
        ],
        compiler_params=_params("parallel"),
        name="route",
    )(logits_t)


def _gather_body(cum_ref, h_ref, rank_ref, xe_o, *, chunk, rows):
    n, d = h_ref.shape
    experts, cap, _ = xe_o.shape
    nch = n // chunk
    per_chunk = chunk // SCATTER_TOKENS
    first = (pl.program_id(0) * pl.num_programs(1) + pl.program_id(1)) * experts
    for k in range(experts):
        rank = rank_ref[k]
        if nch == 1 and cap == rows:
            slot = lax.broadcasted_iota(jnp.int32, (rows, chunk), 0)
            onehot = jnp.where(slot == rank, 1.0, 0.0).astype(BF16)
            xe_o[k] = jnp.dot(onehot, h_ref[...], preferred_element_type=F32).astype(xe_o.dtype)
            continue
        base = (first + k) * (n // SCATTER_TOKENS + 1)
        xe_o[k] = jnp.zeros((cap, d), xe_o.dtype)
        for c in range(nch):
            lo = cum_ref[base + c * per_chunk]
            hi = cum_ref[base + (c + 1) * per_chunk]
            rank_c = rank[:, c * chunk:(c + 1) * chunk]
            for jb in range(cap // rows):

                @pl.when((lo < (jb + 1) * rows) & (hi > jb * rows))
                def _():
                    slot = lax.broadcasted_iota(jnp.int32, (rows, chunk), 0) + jb * rows
                    onehot = jnp.where(slot == rank_c, 1.0, 0.0).astype(BF16)
                    picked = jnp.dot(onehot, h_ref[c * chunk:(c + 1) * chunk, :], preferred_element_type=F32)
                    xe_o[k, jb * rows:(jb + 1) * rows, :] += picked.astype(xe_o.dtype)


def _gather(h2, rank, cum_flat):
    b, n, d = h2.shape
    e = rank.shape[1]
    cap = CAPACITY_FACTOR * n // e
    chunk = min(n, GATHER_CHUNK)
    per_step = e if n <= chunk else 1
    return pl.pallas_call(
        functools.partial(_gather_body, chunk=chunk, rows=min(cap, GATHER_ROWS)),
        grid_spec=pltpu.PrefetchScalarGridSpec(
            num_scalar_prefetch=1,
            grid=(b, e // per_step),
            in_specs=[
                pl.BlockSpec((None, n, d), lambda bi, ei, cum_ref: (bi, 0, 0)),
                pl.BlockSpec((None, per_step, 1, n), lambda bi, ei, cum_ref: (bi, ei, 0, 0)),
            ],
            out_specs=pl.BlockSpec((per_step, None, cap, d), lambda bi, ei, cum_ref: (ei, bi, 0, 0)),
        ),
        out_shape=jax.ShapeDtypeStruct((e, b, cap, d), BF16),
        compiler_params=_params("parallel", "parallel"),
        name="gather",
    )(cum_flat, h2, rank)


def _ffn_body(xe_ref, wg_ref, wu_ref, wd_ref, ye_o):
    bb, cap, d = xe_ref.shape
    xe = xe_ref[...].reshape(bb * cap, d)
    g = jnp.dot(xe, wg_ref[...], preferred_element_type=F32)
    u = jnp.dot(xe, wu_ref[...], preferred_element_type=F32)
    hid = (g / (1.0 + jnp.exp(-g)) * u).astype(BF16)
    ye = jnp.dot(hid, wd_ref[...], preferred_element_type=F32)
    ye_o[...] = ye.astype(ye_o.dtype).reshape(bb, cap, d)


def _expert_ffn(xe, w_gate, w_up, w_down):
    e, b, cap, d = xe.shape
    ff = w_gate.shape[-1]
    bb = max(1, min(b, 512 // cap))
    return pl.pallas_call(
        _ffn_body,
        grid=(e, b // bb),
        in_specs=[
            pl.BlockSpec((None, bb, cap, d), lambda ei, bi: (ei, bi, 0, 0)),
            pl.BlockSpec((None, d, ff), lambda ei, bi: (ei, 0, 0)),
            pl.BlockSpec((None, d, ff), lambda ei, bi: (ei, 0, 0)),
            pl.BlockSpec((None, ff, d), lambda ei, bi: (ei, 0, 0)),
        ],
        out_specs=pl.BlockSpec((None, bb, cap, d), lambda ei, bi: (ei, bi, 0, 0)),
        out_shape=jax.ShapeDtypeStruct((e, b, cap, d), BF16),
        compiler_params=_params("parallel", "arbitrary"),
        name="expert_ffn",
    )(xe, w_gate, w_up, w_down)


def _ffn_cast_body(xe_ref, wg_ref, wu_ref, wd_ref, ye_o, wg_o, wu_o, wd_o, acc):
    f = pl.program_id(1)
    bb, cap, d = xe_ref.shape
    wg = wg_ref[...].astype(BF16)
    wu = wu_ref[...].astype(BF16)
    wd = wd_ref[...].astype(BF16)
    wg_o[...] = wg
    wu_o[...] = wu
    wd_o[...] = wd
    xe = xe_ref[...].reshape(bb * cap, d)
    g = jnp.dot(xe, wg, preferred_element_type=F32)
    u = jnp.dot(xe, wu, preferred_element_type=F32)
    hid = (g / (1.0 + jnp.exp(-g)) * u).astype(BF16)
    part = jnp.dot(hid, wd, preferred_element_type=F32)

    @pl.when(f == 0)
    def _():
        acc[...] = part

    @pl.when(f > 0)
    def _():
        acc[...] += part

    @pl.when(f == pl.num_programs(1) - 1)
    def _():
        ye_o[...] = acc[...].astype(ye_o.dtype).reshape(bb, cap, d)


def _expert_ffn_cast(xe, layer, w_gate, w_up, w_down):
    e, b, cap, d = xe.shape
    ff = w_gate.shape[-1]
    ffs = FF_SLICE
    return pl.pallas_call(
        _ffn_cast_body,
        grid=(e, ff // ffs),
        in_specs=[
            pl.BlockSpec((None, b, cap, d), lambda ei, f: (ei, 0, 0, 0)),
            pl.BlockSpec((None, None, d, ffs), lambda ei, f: (layer, ei, 0, f)),
            pl.BlockSpec((None, None, d, ffs), lambda ei, f: (layer, ei, 0, f)),
            pl.BlockSpec((None, None, ffs, d), lambda ei, f: (layer, ei, f, 0)),
        ],
        out_specs=[
            pl.BlockSpec((None, b, cap, d), lambda ei, f: (ei, 0, 0, 0)),
            pl.BlockSpec((None, d, ffs), lambda ei, f: (ei, 0, f)),
            pl.BlockSpec((None, d, ffs), lambda ei, f: (ei, 0, f)),
            pl.BlockSpec((None, ffs, d), lambda ei, f: (ei, f, 0)),
        ],
        out_shape=[
            jax.ShapeDtypeStruct((e, b, cap, d), BF16),
            jax.ShapeDtypeStruct((e, d, ff), BF16),
            jax.ShapeDtypeStruct((e, d, ff), BF16),
            jax.ShapeDtypeStruct((e, ff, d), BF16),
        ],
        scratch_shapes=[pltpu.VMEM((b * cap, d), F32)],
        compiler_params=_params("parallel", "arbitrary"),
        name="expert_ffn_cast",
    )(xe, w_gate, w_up, w_down)


def _scatter_body(cum_ref, ye_ref, x_ref, rank_ref, aff_ref, gate_ref, o_ref):
    e, cap, dh = ye_ref.shape
    tn = x_ref.shape[0]
    sub_tiles = tn // SCATTER_TOKENS
    win = min(cap, 2 * SCATTER_TOKENS)
    per_expert = pl.num_programs(2) * sub_tiles + 1
    col = lax.broadcasted_iota(jnp.int32, (SCATTER_TOKENS, win), 1).astype(F32)
    for sub in range(sub_tiles):
        rows = slice(sub * SCATTER_TOKENS, (sub + 1) * SCATTER_TOKENS)
        acc = jnp.zeros((SCATTER_TOKENS, dh), F32)
        for ei in range(e):
            rank = rank_ref[rows, ei:ei + 1]
            if cap > win:
                first = cum_ref[(pl.program_id(0) * e + ei) * per_expert + pl.program_id(2) * sub_tiles + sub]
                start = jnp.clip((first // SCATTER_TOKENS) * SCATTER_TOKENS, 0, cap - win)
                start = pl.multiple_of(start, SCATTER_TOKENS)
                onehot = jnp.where(rank - start.astype(F32) == col, 1.0, 0.0).astype(BF16)
                contrib = jnp.dot(onehot, ye_ref[ei, pl.ds(start, win), :], preferred_element_type=F32)
            else:
                onehot = jnp.where(rank == col, 1.0, 0.0).astype(BF16)
                contrib = jnp.dot(onehot, ye_ref[ei], preferred_element_type=F32)
            acc = acc + contrib * aff_ref[rows, ei:ei + 1]
        o_ref[rows, :] = x_ref[rows, :] + gate_ref[...] * acc


def _scatter(ye, x1, rank_t, aff_t, mod, cum_flat):
    e, b, cap, d = ye.shape
    n = x1.shape[1]
    tn = min(n, 512)
    dh = d if e * cap * d * 2 <= SCATTER_YE_BYTES else d // 2
    mb = mod.shape[0]
    bsel = (lambda i: i) if mb > 1 else (lambda i: 0)
    return pl.pallas_call(
        _scatter_body,
        grid_spec=pltpu.PrefetchScalarGridSpec(
            num_scalar_prefetch=1,
            grid=(b, d // dh, n // tn),
            in_specs=[
                pl.BlockSpec((e, None, cap, dh), lambda bi, j, i, cum_ref: (0, bi, 0, j)),
                pl.BlockSpec((None, tn, dh), lambda bi, j, i, cum_ref: (bi, i, j)),
                pl.BlockSpec((None, tn, 128), lambda bi, j, i, cum_ref: (bi, i, 0)),
                pl.BlockSpec((None, tn, 128), lambda bi, j, i, cum_ref: (bi, i, 0)),
                pl.BlockSpec((None, None, 1, dh), lambda bi, j, i, cum_ref: (bsel(bi), 5, 0, j)),
            ],
            out_specs=pl.BlockSpec((None, tn, dh), lambda bi, j, i, cum_ref: (bi, i, j)),
        ),
        out_shape=jax.ShapeDtypeStruct(x1.shape, F32),
        compiler_params=_params("parallel", "parallel", "parallel"),
        name="scatter",
    )(cum_flat, ye, x1, rank_t, aff_t, mod)


def _final_norm_body(x_ref, g_ref, o_ref):
    o_ref[...] = _rms(x_ref[...], g_ref[...])


def _final_norm(x, g):
    b, n, d = x.shape
    tm = TOKEN_TILE
    return pl.pallas_call(
        _final_norm_body,
        grid=(b, n // tm),
        in_specs=[pl.BlockSpec((None, tm, d), lambda bi, i: (bi, i, 0)), pl.BlockSpec((1, d), lambda bi, i: (0, 0))],
        out_specs=pl.BlockSpec((None, tm, d), lambda bi, i: (bi, i, 0)),
        out_shape=jax.ShapeDtypeStruct(x.shape, F32),
        compiler_params=_params("parallel", "parallel"),
        name="final_norm",
    )(x, g.reshape(1, d))


def _rope_tables(n_tokens, rot_dim):
    t = jnp.arange(n_tokens, dtype=jnp.int32)
    row = (t // GRID_W).astype(F32)
    col = (t % GRID_W).astype(F32)
    axis_dim = rot_dim // 2
    freqs = ROPE_THETA ** (-jnp.arange(0, axis_dim, 2, dtype=F32) / axis_dim)
    ang = jnp.concatenate([row[:, None] * freqs[None, :], col[:, None] * freqs[None, :]], axis=-1)
    cos, sin = jnp.cos(ang), jnp.sin(ang)
    cos2 = jnp.repeat(cos, 2, axis=-1)
    sin2 = jnp.stack([-sin, sin], axis=-1).reshape(n_tokens, rot_dim)
    pad = 128 - rot_dim
    if pad:
        cos2 = jnp.concatenate([cos2, jnp.ones((n_tokens, pad), F32)], axis=-1)
        sin2 = jnp.concatenate([sin2, jnp.zeros((n_tokens, pad), F32)], axis=-1)
    return cos2, sin2


def _w_in_body(w_ref, o_ref):
    kr_lo = SEG_CKV[1]
    w = w_ref[...]
    rows = w.shape[0]
    o_ref[:, :kr_lo] = w[:, :kr_lo].astype(BF16)
    o_ref[:, kr_lo:SEG_KR[0]] = w[:, kr_lo + ROPE_DIM_B:].astype(BF16)
    o_ref[:, SEG_KR[0]:SEG_KR[0] + ROPE_DIM_B] = w[:, kr_lo:kr_lo + ROPE_DIM_B].astype(BF16)
    o_ref[:, SEG_KR[0] + ROPE_DIM_B:] = jnp.zeros((rows, 128 - ROPE_DIM_B), BF16)


def _reorder_w_in(w_in):
    depth, d, cols = w_in.shape
    rows = TOKEN_TILE
    return pl.pallas_call(
        _w_in_body,
        grid=(depth, d // rows),
        in_specs=[pl.BlockSpec((None, rows, cols), lambda l, i: (l, i, 0))],
        out_specs=pl.BlockSpec((None, rows, IN_COLS_P), lambda l, i: (l, i, 0)),
        out_shape=jax.ShapeDtypeStruct((depth, d, IN_COLS_P), BF16),
        compiler_params=_params("parallel", "parallel"),
        name="reorder_w_in",
    )(w_in)


def _prepare_weights(norm1_g, norm2_g, w_in, qa_norm_g, ka_norm_g, q_norm_b, kv_norm_b, w_q_up, w_kv_up, w_out,
                     w_router):
    depth, d, _ = w_in.shape
    w_in_p = _reorder_w_in(w_in)
    wq = w_q_up.reshape(depth, Q_RANK_B, HEADS_B, NOPE_DIM_B + ROPE_DIM_B)
    wq = jnp.pad(wq, ((0, 0), (0, 0), (0, 0), (0, QB_PAD - NOPE_DIM_B - ROPE_DIM_B)))
    wkv = w_kv_up.reshape(depth, KV_RANK_B, HEADS_B, NOPE_DIM_B + V_DIM_B)
    return {
        "norm1_g": norm1_g.reshape(depth, 1, d),
        "norm2_g": norm2_g.reshape(depth, 1, d),
        "w_in": w_in_p,
        "qa_g": qa_norm_g.reshape(depth, 1, HEAD_DIM),
        "ka_g": ka_norm_g.reshape(depth, 1, HEAD_DIM),
        "qn_g": q_norm_b.reshape(depth, 1, Q_RANK_B),
        "kvn_g": kv_norm_b.reshape(depth, 1, KV_RANK_B),
        "w_q_up": wq.reshape(depth, Q_RANK_B, HEADS_B * QB_PAD).astype(BF16),
        "w_kv_k": wkv[..., :NOPE_DIM_B].reshape(depth, KV_RANK_B, HEADS_B * NOPE_DIM_B).astype(BF16),
        "w_kv_v": wkv[..., NOPE_DIM_B:].reshape(depth, KV_RANK_B, HEADS_B * V_DIM_B).astype(BF16),
        "w_out": w_out.astype(BF16),
        "w_router": jnp.pad(w_router, ((0, 0), (0, 0), (0, 128 - N_EXPERTS))).astype(BF16),
    }


def _moe_select(h2, logits_t):
    rank, rank_t, aff_t, cum = _route(logits_t)
    n = rank.shape[2]
    cum_flat = cum[:, :, :n // SCATTER_TOKENS + 1].reshape(-1)
    xe = _gather(h2, rank.reshape(rank.shape[0], rank.shape[1], 1, n), cum_flat)
    return xe, rank_t, aff_t, cum_flat


def kernel(x_prompt, x_sample, cache_a_k, cache_a_v, cache_b_ckv, cache_b_krope, cache_c_k, cache_c_v, c, c_ctx, w_ada, b_ada, norm1_g, norm2_g, w_in, qa_norm_g, ka_norm_g, q_norm_b, kv_norm_b, w_q_up, w_kv_up, na_bias, w_out, w_router, w_gate, w_up, w_down, final_norm_g):
    depth, d, _ = w_in.shape
    bp, seq, _ = x_prompt.shape
    bs, n_lat, _ = x_sample.shape
    past = cache_a_k.shape[2]
    rows = n_lat // GRID_W

    wts = _prepare_weights(norm1_g, norm2_g, w_in, qa_norm_g, ka_norm_g, q_norm_b, kv_norm_b, w_q_up, w_kv_up,
                           w_out, w_router)
    rope_tabs = _rope_tables(n_lat, HEAD_DIM) + _rope_tables(n_lat, ROPE_DIM_B)

    cond8 = jnp.concatenate([c_ctx[None], c, jnp.zeros((8 - 1 - bs, d), F32)], axis=0)
    mods = _modulation(cond8, w_ada, b_ada)
    mods = mods.reshape(depth, 8, 6, 1, d)

    ca_k = cache_a_k.reshape(bs, depth, past, KV_A).astype(BF16)
    ca_v = jnp.swapaxes(cache_a_v.reshape(bs, depth, past, KV_A), 2, 3).astype(BF16)
    cc_k = cache_c_k.reshape(bs, depth, past, WIDTH_C).astype(BF16)
    cc_v = cache_c_v.reshape(bs, depth, past, WIDTH_C).astype(BF16)
    cb_k, cb_v = _cache_mla_kv(cache_b_ckv, cache_b_krope, wts["w_kv_k"], wts["w_kv_v"])
    nat_bias = _nat_bias(na_bias, rows)

    xp, xs = x_prompt, x_sample
    states = ()
    for l in range(depth):
        mod_c = mods[l, 0:1]
        mod_l = mods[l, 1:1 + bs]

        qa, ka, va, qb, kb, vb, qc, kc, vc, *states = _input_side(xp, mod_c, l, wts, None, states)
        o_a, o_b, o_c = _ctx_attention(qa, ka, va, qb, kb, vb, qc, kc, vc)
        x1, h2, lg = _output_side(o_a, o_b, o_c, xp, mod_c, l, wts)
        xe, rank_t, aff_t, cum_flat = _moe_select(h2, lg)
        ye, wg16, wu16, wd16 = _expert_ffn_cast(xe, l, w_gate, w_up, w_down)
        xp = _scatter(ye, x1, rank_t, aff_t, mod_c, cum_flat)

        qa, ka, va, qb, kb, vb, qc, kc, vc = _input_side(xs, mod_l, l, wts, rope_tabs, None)
        o_a = _attention(qa, ka, va, ca_k, ca_v, l, HEADS_A, HEADS_A // KV_HEADS_A, HEAD_DIM, HEAD_DIM)
        o_b = _attention(qb, kb, vb, cb_k, cb_v, l, HEADS_B, 1, QB_PAD, V_DIM_B)
        o_c = _neighbourhood_attention(qc, kc, vc, cc_k, cc_v, nat_bias, l)
        x1, h2, lg = _output_side(o_a, o_b, o_c, xs, mod_l, l, wts)
        xe, rank_t, aff_t, cum_flat = _moe_select(h2, lg)
        ye = _expert_ffn(xe, wg16, wu16, wd16)
        xs = _scatter(ye, x1, rank_t, aff_t, mod_l, cum_flat)

    y_prompt = _final_norm(xp, final_norm_g)
    y_sample = _final_norm(xs, final_norm_g)
    st_ka, st_va, new_b_ckv, new_b_krope, st_kc, st_vc = states
    new_a_k = st_ka.reshape(bp, depth, seq, KV_HEADS_A, HEAD_DIM)
    new_a_v = st_va.reshape(bp, depth, seq, KV_HEADS_A, HEAD_DIM)
    new_c_k = st_kc.reshape(bp, depth, seq, HEADS_C, HEAD_DIM)
    new_c_v = st_vc.reshape(bp, depth, seq, HEADS_C, HEAD_DIM)
    return (y_prompt, y_sample, new_a_k, new_a_v, new_b_ckv, new_b_krope, new_c_k, new_c_v)
```

```python
import functools

import jax
import jax.numpy as jnp
from jax import lax
from jax.experimental import pallas as pl
from jax.experimental.pallas import tpu as pltpu

F32 = jnp.float32
BF16 = jnp.bfloat16

GRID_W = 64
HEAD_DIM = 128
HEADS_A = 6
KV_HEADS_A = 2
HEADS_B = 5
Q_RANK_B = 512
KV_RANK_B = 256
NOPE_DIM_B = 128
ROPE_DIM_B = 64
V_DIM_B = 128
HEADS_C = 5
WIN_ROWS_MAX = 8
WIN_COLS = 16
N_EXPERTS = 16
CAPACITY_FACTOR = 2
ROPE_THETA = 10000.0
EPS = 1e-6

QB_PAD = 256
WIDTH_A = HEADS_A * HEAD_DIM
WIDTH_B = HEADS_B * V_DIM_B
WIDTH_C = HEADS_C * HEAD_DIM
KV_A = KV_HEADS_A * HEAD_DIM

SEG_QA = (0, WIDTH_A)
SEG_KA = (SEG_QA[1], SEG_QA[1] + KV_A)
SEG_VA = (SEG_KA[1], SEG_KA[1] + KV_A)
SEG_CQ = (SEG_VA[1], SEG_VA[1] + Q_RANK_B)
SEG_CKV = (SEG_CQ[1], SEG_CQ[1] + KV_RANK_B)
SEG_QC = (SEG_CKV[1], SEG_CKV[1] + WIDTH_C)
SEG_KC = (SEG_QC[1], SEG_QC[1] + WIDTH_C)
SEG_VC = (SEG_KC[1], SEG_KC[1] + WIDTH_C)
SEG_KR = (SEG_VC[1], SEG_VC[1] + 128)
IN_COLS_P = SEG_KR[1]

STATE_WIDTHS = (KV_A, KV_A, KV_RANK_B, ROPE_DIM_B, WIDTH_C, WIDTH_C)

NAT_QROWS = 4
NAT_KROWS = 12
MASK_VALUE = -1e30
LOG2E = 1.4426950408889634

TOKEN_TILE = 256
PROJ_TILE = 512
OUT_GROUPS = 2
ATTN_ROWS = 512
ATTN_GROUPS = 2
NAT_GROUPS = 16
GATHER_CHUNK = 512
GATHER_ROWS = 128
GATHER_WINDOW = 1536
SCATTER_TOKENS = 128
ROUTE_TOKENS = 4096
SCATTER_YE_BYTES = 16 * 1024 * 1024
FF_SLICE = 512
VMEM_LIMIT = 56 * 1024 * 1024
NT_DIMS = (((1,), (1,)), ((), ()))


def _params(*sem):
    return pltpu.CompilerParams(dimension_semantics=sem, vmem_limit_bytes=VMEM_LIMIT)


def _resident(block_shape, index_map):
    return pl.BlockSpec(block_shape, index_map, pipeline_mode=pl.Buffered(1))


def _rms(x, g):
    ms = jnp.mean(x * x, axis=-1, keepdims=True)
    return x * lax.rsqrt(ms + EPS) * g


def _swap_pairs(x):
    lane = lax.broadcasted_iota(jnp.int32, x.shape, x.ndim - 1)
    nxt = pltpu.roll(x, x.shape[-1] - 1, x.ndim - 1)
    prv = pltpu.roll(x, 1, x.ndim - 1)
    return jnp.where((lane & 1) == 0, nxt, prv)


def _rope(x, cos, sin_signed):
    return x * cos + _swap_pairs(x) * sin_signed


def _mod_body(c_ref, w_ref, b_ref, o_ref):
    c = c_ref[...]
    s = (c / (1.0 + jnp.exp(-c))).astype(BF16)
    o_ref[...] = jnp.dot(s, w_ref[...].astype(BF16), preferred_element_type=F32) + b_ref[...]


def _modulation(cond8, w_ada, b_ada):
    depth, d, cols = w_ada.shape
    tn = 1024
    return pl.pallas_call(
        _mod_body,
        grid=(depth, cols // tn),
        in_specs=[
            pl.BlockSpec((8, d), lambda l, j: (0, 0)),
            pl.BlockSpec((None, d, tn), lambda l, j: (l, 0, j)),
            pl.BlockSpec((None, 1, tn), lambda l, j: (l, 0, j)),
        ],
        out_specs=pl.BlockSpec((None, 8, tn), lambda l, j: (l, 0, j)),
        out_shape=jax.ShapeDtypeStruct((depth, 8, cols), F32),
        compiler_params=_params("parallel", "parallel"),
        name="modulation",
    )(cond8, w_ada, b_ada.reshape(depth, 1, cols))


def _in_body(*refs, rope, state, carried, layer):
    it = iter(refs)
    x_ref, sh_ref, sc_ref, g_ref, win_ref = (next(it) for _ in range(5))
    qag_ref, kag_ref, qng_ref, kvng_ref = (next(it) for _ in range(4))
    wq_ref, wkk_ref, wkv_ref = (next(it) for _ in range(3))
    if rope:
        ca_ref, sa_ref, cb_ref, sb_ref = (next(it) for _ in range(4))
    for _ in range(carried):
        next(it)
    qa_o, ka_o, va_o, qb_o, kb_o, vb_o, qc_o, kc_o, vc_o = (next(it) for _ in range(9))
    if state:
        st_ka, st_va, st_ckv, st_kr, st_kc, st_vc = (next(it) for _ in range(6))

    def put_state(ref, value, lo=0):
        w = value.shape[1]
        if carried:
            ref[:, lo:lo + w] = value
        else:
            for l in range(ref.shape[0]):
                ref[l, :, lo:lo + w] = value if l == layer else jnp.zeros_like(value)

    h = _rms(x_ref[...], g_ref[...]) * (1.0 + sc_ref[...]) + sh_ref[...]
    hb = h.astype(BF16)

    def seg(bounds):
        return jnp.dot(hb, win_ref[:, bounds[0]:bounds[1]], preferred_element_type=F32)

    def rope_a(y):
        return _rope(y, ca_ref[...], sa_ref[...]) if rope else y

    def rope_b(y):
        return _rope(y, cb_ref[...], sb_ref[...]) if rope else y

    qa = seg(SEG_QA)
    for hd in range(HEADS_A):
        lo = hd * HEAD_DIM
        y = rope_a(_rms(qa[:, lo:lo + HEAD_DIM], qag_ref[...]))
        qa_o[:, lo:lo + HEAD_DIM] = (y * (HEAD_DIM ** -0.5 * LOG2E)).astype(BF16)
    ka = seg(SEG_KA)
    for hd in range(KV_HEADS_A):
        lo = hd * HEAD_DIM
        y = rope_a(_rms(ka[:, lo:lo + HEAD_DIM], kag_ref[...]))
        ka_o[:, lo:lo + HEAD_DIM] = y.astype(BF16)
        if state:
            put_state(st_ka, y, lo)
    va = seg(SEG_VA)
    va_o[...] = (va if state else va.T).astype(BF16)

    cq = _rms(seg(SEG_CQ), qng_ref[...]).astype(BF16)
    qb = jnp.dot(cq, wq_ref[...], preferred_element_type=F32)
    qscale = (NOPE_DIM_B + ROPE_DIM_B) ** -0.5 * LOG2E
    for hd in range(HEADS_B):
        lo = hd * QB_PAD
        qb_o[:, lo:lo + 128] = (qb[:, lo:lo + 128] * qscale).astype(BF16)
        qb_o[:, lo + 128:lo + 256] = (rope_b(qb[:, lo + 128:lo + 256]) * qscale).astype(BF16)
    ckv = _rms(seg(SEG_CKV), kvng_ref[...])
    ckvb = ckv.astype(BF16)
    kr = rope_b(seg(SEG_KR))
    krb = kr.astype(BF16)
    kn = jnp.dot(ckvb, wkk_ref[...], preferred_element_type=F32)
    for hd in range(HEADS_B):
        kb_o[:, hd * QB_PAD:hd * QB_PAD + 128] = kn[:, hd * 128:(hd + 1) * 128].astype(BF16)
        kb_o[:, hd * QB_PAD + 128:(hd + 1) * QB_PAD] = krb
    vb = jnp.dot(ckvb, wkv_ref[...], preferred_element_type=F32)
    vb_o[...] = (vb if state else vb.T).astype(BF16)

    qc_o[...] = (seg(SEG_QC) * (HEAD_DIM ** -0.5 * LOG2E)).astype(BF16)
    kc = seg(SEG_KC)
    kc_o[...] = kc.astype(BF16)
    vc = seg(SEG_VC)
    vc_o[...] = vc.astype(BF16)

    if state:
        put_state(st_va, va)
        put_state(st_ckv, ckv)
        put_state(st_kr, kr[:, :ROPE_DIM_B])
        put_state(st_kc, kc)
        put_state(st_vc, vc)


def _input_side(x, mod, layer, wts, rope_tabs, states):
    state = states is not None
    b, n, d = x.shape
    tm = min(n, PROJ_TILE)
    mb = mod.shape[0]
    bsel = (lambda i: i) if mb > 1 else (lambda i: 0)
    rope = rope_tabs is not None

    def modspec(k):
        return pl.BlockSpec((None, None, 1, d), lambda bi, i: (bsel(bi), k, 0, 0))

    def vec(w):
        return pl.BlockSpec((None, 1, w), lambda bi, i: (layer, 0, 0))

    in_specs = [
        pl.BlockSpec((None, tm, d), lambda bi, i: (bi, i, 0)),
        modspec(0), modspec(1), vec(d),
        _resident((None, d, IN_COLS_P), lambda bi, i: (layer, 0, 0)),
        vec(HEAD_DIM), vec(HEAD_DIM), vec(Q_RANK_B), vec(KV_RANK_B),
        _resident((None, Q_RANK_B, HEADS_B * QB_PAD), lambda bi, i: (layer, 0, 0)),
        _resident((None, KV_RANK_B, HEADS_B * 128), lambda bi, i: (layer, 0, 0)),
        _resident((None, KV_RANK_B, HEADS_B * 128), lambda bi, i: (layer, 0, 0)),
    ]
    args = [x, mod, mod, wts["norm1_g"], wts["w_in"], wts["qa_g"], wts["ka_g"], wts["qn_g"], wts["kvn_g"],
            wts["w_q_up"], wts["w_kv_k"], wts["w_kv_v"]]
    if rope:
        in_specs += [pl.BlockSpec((tm, 128), lambda bi, i: (i, 0))] * 4
        args += list(rope_tabs)

    widths = [WIDTH_A, KV_A, KV_A, HEADS_B * QB_PAD, HEADS_B * QB_PAD, WIDTH_B, WIDTH_C, WIDTH_C, WIDTH_C]
    out_shape = [jax.ShapeDtypeStruct((b, n, w), BF16) for w in widths]
    out_specs = [pl.BlockSpec((None, tm, w), lambda bi, i: (bi, i, 0)) for w in widths]
    if not state:
        for k in (2, 5):
            out_shape[k] = jax.ShapeDtypeStruct((b, widths[k], n), BF16)
            out_specs[k] = pl.BlockSpec((None, widths[k], tm), lambda bi, i: (bi, 0, i))
    aliases = {}
    if state:
        depth = wts["w_in"].shape[0]
        for w in STATE_WIDTHS:
            out_shape.append(jax.ShapeDtypeStruct((b, depth, n, w), F32))
            if states:
                out_specs.append(pl.BlockSpec((None, None, tm, w), lambda bi, i: (bi, layer, i, 0)))
            else:
                out_specs.append(pl.BlockSpec((None, depth, tm, w), lambda bi, i: (bi, 0, i, 0)))
        for k, buf in enumerate(states):
            aliases[len(args)] = len(widths) + k
            in_specs.append(pl.BlockSpec(memory_space=pl.ANY))
            args.append(buf)

    return pl.pallas_call(
        functools.partial(_in_body, rope=rope, state=state, carried=len(aliases), layer=layer),
        grid=(b, n // tm),
        in_specs=in_specs,
        out_specs=out_specs,
        out_shape=out_shape,
        input_output_aliases=aliases,
        compiler_params=_params("parallel", "parallel"),
        name="input_side",
    )(*args)


def _attn_body(q_ref, ks_ref, vs_ref, kc_ref, vc_ref, o_ref):
    rows = min(q_ref.shape[0], ATTN_ROWS)
    groups = q_ref.shape[0] // rows

    def scores(c):
        q = q_ref[c * rows:(c + 1) * rows, :]
        s_self = lax.dot_general(ks_ref[...], q, NT_DIMS, preferred_element_type=F32)
        s_ctx = lax.dot_general(kc_ref[...], q, NT_DIMS, preferred_element_type=F32)
        m = jnp.maximum(jnp.max(s_self, axis=0, keepdims=True), jnp.max(s_ctx, axis=0, keepdims=True))
        return s_self, s_ctx, m

    def finish(c, s_self, s_ctx, m):
        p_self = jnp.exp2(s_self - m)
        p_ctx = jnp.exp2(s_ctx - m)
        l = jnp.sum(p_self, axis=0, keepdims=True) + jnp.sum(p_ctx, axis=0, keepdims=True)
        acc = jnp.dot(vs_ref[...], p_self.astype(BF16), preferred_element_type=F32)
        acc = acc + jnp.dot(vc_ref[...], p_ctx.astype(BF16), preferred_element_type=F32)
        o_ref[c * rows:(c + 1) * rows, :] = (acc / l).T.astype(o_ref.dtype)

    pending = scores(0)
    for c in range(groups):
        following = scores(c + 1) if c + 1 < groups else None
        finish(c, *pending)
        pending = following


def _ctx_attn_body(qa_ref, ka_ref, va_ref, qb_ref, kb_ref, vb_ref, qc_ref, kc_ref, vc_ref, oa_ref, ob_ref, oc_ref):
    def head(q, k, v):
        s = lax.dot_general(q, k, NT_DIMS, preferred_element_type=F32)
        p = jnp.exp2(s - jnp.max(s, axis=-1, keepdims=True))
        acc = jnp.dot(p.astype(BF16), v, preferred_element_type=F32)
        return (acc / jnp.sum(p, axis=-1, keepdims=True)).astype(BF16)

    def lanes(ref, i, w):
        return ref[:, i * w:(i + 1) * w]

    for h in range(HEADS_A):
        g = h // (HEADS_A // KV_HEADS_A)
        oa_ref[:, h * HEAD_DIM:(h + 1) * HEAD_DIM] = head(
            lanes(qa_ref, h, HEAD_DIM), lanes(ka_ref, g, HEAD_DIM), lanes(va_ref, g, HEAD_DIM))
    for h in range(HEADS_B):
        ob_ref[:, h * V_DIM_B:(h + 1) * V_DIM_B] = head(
            lanes(qb_ref, h, QB_PAD), lanes(kb_ref, h, QB_PAD), lanes(vb_ref, h, V_DIM_B))
    for h in range(HEADS_C):
        oc_ref[:, h * HEAD_DIM:(h + 1) * HEAD_DIM] = head(
            lanes(qc_ref, h, HEAD_DIM), lanes(kc_ref, h, HEAD_DIM), lanes(vc_ref, h, HEAD_DIM))


def _ctx_attention(qa, ka, va, qb, kb, vb, qc, kc, vc):
    b, n, _ = qa.shape
    args = [qa, ka, va, qb, kb, vb, qc, kc, vc]
    widths = [WIDTH_A, WIDTH_B, WIDTH_C]
    return pl.pallas_call(
        _ctx_attn_body,
        grid=(b,),
        in_specs=[pl.BlockSpec((None, n, a.shape[2]), lambda bi: (bi, 0, 0)) for a in args],
        out_specs=[pl.BlockSpec((None, n, w), lambda bi: (bi, 0, 0)) for w in widths],
        out_shape=[jax.ShapeDtypeStruct((b, n, w), BF16) for w in widths],
        compiler_params=_params("parallel"),
        name="ctx_attention",
    )(*args)


def _attention(q, ks, vs_t, kc, vc_t, layer, heads, group, dq, dv):
    b, n, _ = q.shape
    ms = ks.shape[1]
    mc = kc.shape[2]
    tq = min(n, ATTN_GROUPS * ATTN_ROWS)
    return pl.pallas_call(
        _attn_body,
        grid=(b, heads, n // tq),
        in_specs=[
            pl.BlockSpec((None, tq, dq), lambda bi, h, i: (bi, i, h)),
            pl.BlockSpec((None, ms, dq), lambda bi, h, i: (bi, 0, h // group)),
            pl.BlockSpec((None, dv, ms), lambda bi, h, i: (bi, h // group, 0)),
            pl.BlockSpec((None, None, mc, dq), lambda bi, h, i: (bi, layer, 0, h // group)),
            pl.BlockSpec((None, None, dv, mc), lambda bi, h, i: (bi, layer, h // group, 0)),
        ],
        out_specs=pl.BlockSpec((None, tq, dv), lambda bi, h, i: (bi, i, h)),
        out_shape=jax.ShapeDtypeStruct((b, n, heads * dv), BF16),
        compiler_params=_params("parallel", "parallel", "parallel"),
        name="attention",
    )(q, ks, vs_t, kc, vc_t)


def _nat_body(q_ref, ks_ref, vs_ref, kc_ref, vc_ref, bias_ref, o_ref):
    tq = NAT_QROWS * GRID_W
    nk = NAT_KROWS * GRID_W
    groups = q_ref.shape[0] // tq
    key_rows = ks_ref.shape[0] // GRID_W
    last = ks_ref.shape[0] // tq - 1

    def scores(c):
        r = pl.program_id(2) * groups + c
        kr0 = jnp.clip(NAT_QROWS * r - WIN_ROWS_MAX // 2, 0, key_rows - NAT_KROWS)
        start = pl.multiple_of(kr0 * GRID_W, tq)
        kind = jnp.where(r == 0, 0, jnp.where(r == last, 2, 1))
        q = q_ref[c * tq:(c + 1) * tq, :]
        s_win = lax.dot_general(q, ks_ref[pl.ds(start, nk), :], NT_DIMS, preferred_element_type=F32)
        s_win = s_win + bias_ref[kind]
        s_ctx = lax.dot_general(q, kc_ref[...], NT_DIMS, preferred_element_type=F32)
        m = jnp.maximum(jnp.max(s_win, axis=-1, keepdims=True), jnp.max(s_ctx, axis=-1, keepdims=True))
        return s_win, s_ctx, m, start

    def finish(c, s_win, s_ctx, m, start):
        p_win = jnp.exp2(s_win - m)
        p_ctx = jnp.exp2(s_ctx - m)
        l = jnp.sum(p_win, axis=-1, keepdims=True) + jnp.sum(p_ctx, axis=-1, keepdims=True)
        acc = jnp.dot(p_win.astype(BF16), vs_ref[pl.ds(start, nk), :], preferred_element_type=F32)
        acc = acc + jnp.dot(p_ctx.astype(BF16), vc_ref[...], preferred_element_type=F32)
        o_ref[c * tq:(c + 1) * tq, :] = (acc / l).astype(o_ref.dtype)

    pending = scores(0)
    for c in range(groups):
        following = scores(c + 1) if c + 1 < groups else None
        finish(c, *pending)
        pending = following


N_DROW = 2 * WIN_ROWS_MAX - 1
N_DCOL = 2 * WIN_COLS - 1


def _nat_bias_body(b_ref, o_ref, *, rows):
    base = (pl.program_id(0) * HEADS_C + pl.program_id(1)) * (N_DROW * N_DCOL)
    c = lax.broadcasted_iota(jnp.int32, (GRID_W, GRID_W), 0)
    kc = lax.broadcasted_iota(jnp.int32, (GRID_W, GRID_W), 1)
    c0 = jnp.clip(c - WIN_COLS // 2, 0, GRID_W - WIN_COLS)
    in_window = (kc >= c0) & (kc < c0 + WIN_COLS)
    dc = kc - c + (WIN_COLS - 1)
    masked = jnp.full((GRID_W, GRID_W), MASK_VALUE, F32)
    tables = {}

    def table(dr):
        if dr not in tables:
            t = jnp.zeros((GRID_W, GRID_W), F32)
            for j in range(N_DCOL):
                t = jnp.where(dc == j, b_ref[base + dr * N_DCOL + j] * LOG2E, t)
            tables[dr] = jnp.where(in_window, t, MASK_VALUE)
        return tables[dr]

    kh = min(WIN_ROWS_MAX, rows)
    for ty, blk in enumerate((0, 1, rows // NAT_QROWS - 1)):
        r_first = NAT_QROWS * blk
        kr0 = min(max(r_first - WIN_ROWS_MAX // 2, 0), rows - NAT_KROWS)
        for a in range(NAT_QROWS):
            r = r_first + a
            r0 = min(max(r - kh // 2, 0), rows - kh)
            for i in range(NAT_KROWS):
                kr = kr0 + i
                blkval = table(kr - r + WIN_ROWS_MAX - 1) if r0 <= kr < r0 + kh else masked
                o_ref[ty, a * GRID_W:(a + 1) * GRID_W, i * GRID_W:(i + 1) * GRID_W] = blkval


def _nat_bias(na_bias, rows):
    depth = na_bias.shape[0]
    tq, nk = NAT_QROWS * GRID_W, NAT_KROWS * GRID_W
    return pl.pallas_call(
        functools.partial(_nat_bias_body, rows=rows),
        grid=(depth, HEADS_C),
        in_specs=[pl.BlockSpec(memory_space=pltpu.SMEM)],
        out_specs=pl.BlockSpec((None, None, 3, tq, nk), lambda l, h: (l, h, 0, 0, 0)),
        out_shape=jax.ShapeDtypeStruct((depth, HEADS_C, 3, tq, nk), F32),
        compiler_params=_params("parallel", "parallel"),
        name="nat_bias",
    )(na_bias.reshape(-1))


def _neighbourhood_attention(q, ks, vs, kc, vc, bias, layer):
    b, n, _ = q.shape
    nblk = n // (NAT_QROWS * GRID_W)
    groups = min(nblk, NAT_GROUPS)
    tq = groups * NAT_QROWS * GRID_W
    mc = kc.shape[2]
    d = HEAD_DIM

    return pl.pallas_call(
        _nat_body,
        grid=(b, HEADS_C, nblk // groups),
        in_specs=[
            pl.BlockSpec((None, tq, d), lambda bi, h, r: (bi, r, h)),
            pl.BlockSpec((None, n, d), lambda bi, h, r: (bi, 0, h)),
            pl.BlockSpec((None, n, d), lambda bi, h, r: (bi, 0, h)),
            pl.BlockSpec((None, None, mc, d), lambda bi, h, r: (bi, layer, 0, h)),
            pl.BlockSpec((None, None, mc, d), lambda bi, h, r: (bi, layer, 0, h)),
            pl.BlockSpec((None, None, 3, NAT_QROWS * GRID_W, NAT_KROWS * GRID_W),
                         lambda bi, h, r: (layer, h, 0, 0, 0)),
        ],
        out_specs=pl.BlockSpec((None, tq, d), lambda bi, h, r: (bi, r, h)),
        out_shape=jax.ShapeDtypeStruct((b, n, HEADS_C * d), BF16),
        compiler_params=_params("parallel", "parallel", "arbitrary"),
        name="neighbourhood_attention",
    )(q, ks, vs, kc, vc, bias)


def _cache_kv_body(ckv_ref, kr_ref, wkk_ref, wkv_ref, kb_o, vb_o):
    ckvb = ckv_ref[...].astype(BF16)
    krb = kr_ref[...].astype(BF16)
    kn = jnp.dot(ckvb, wkk_ref[...], preferred_element_type=F32)
    zeros = jnp.zeros((krb.shape[0], QB_PAD - 128 - ROPE_DIM_B), BF16)
    for hd in range(HEADS_B):
        lo = hd * QB_PAD
        kb_o[:, lo:lo + 128] = kn[:, hd * 128:(hd + 1) * 128].astype(BF16)
        kb_o[:, lo + 128:lo + 128 + ROPE_DIM_B] = krb
        kb_o[:, lo + 128 + ROPE_DIM_B:lo + QB_PAD] = zeros
    vb_o[...] = jnp.dot(ckvb, wkv_ref[...], preferred_element_type=F32).T.astype(BF16)


def _cache_mla_kv(cache_ckv, cache_krope, w_kv_k, w_kv_v):
    b, depth, m, _ = cache_ckv.shape
    return pl.pallas_call(
        _cache_kv_body,
        grid=(b, depth),
        in_specs=[
            pl.BlockSpec((None, None, m, KV_RANK_B), lambda bi, l: (bi, l, 0, 0)),
            pl.BlockSpec((None, None, m, ROPE_DIM_B), lambda bi, l: (bi, l, 0, 0)),
            pl.BlockSpec((None, KV_RANK_B, HEADS_B * 128), lambda bi, l: (l, 0, 0)),
            pl.BlockSpec((None, KV_RANK_B, HEADS_B * 128), lambda bi, l: (l, 0, 0)),
        ],
        out_specs=[
            pl.BlockSpec((None, None, m, HEADS_B * QB_PAD), lambda bi, l: (bi, l, 0, 0)),
            pl.BlockSpec((None, None, WIDTH_B, m), lambda bi, l: (bi, l, 0, 0)),
        ],
        out_shape=[
            jax.ShapeDtypeStruct((b, depth, m, HEADS_B * QB_PAD), BF16),
            jax.ShapeDtypeStruct((b, depth, WIDTH_B, m), BF16),
        ],
        compiler_params=_params("parallel", "parallel"),
        name="cache_mla_kv",
    )(cache_ckv, cache_krope, w_kv_k, w_kv_v)


def _out_body(oa_ref, ob_ref, oc_ref, wo_ref, x_ref, gate_ref, sh_ref, sc_ref, g_ref, wr_ref, x1_o, h2_o, lg_o):
    lo_b, lo_c = WIDTH_A, WIDTH_A + WIDTH_B
    rows = min(x_ref.shape[0], TOKEN_TILE)
    groups = x_ref.shape[0] // rows

    def project(c):
        r = slice(c * rows, (c + 1) * rows)
        o = jnp.dot(oa_ref[r, :], wo_ref[:lo_b, :], preferred_element_type=F32)
        o = o + jnp.dot(ob_ref[r, :], wo_ref[lo_b:lo_c, :], preferred_element_type=F32)
        return o + jnp.dot(oc_ref[r, :], wo_ref[lo_c:, :], preferred_element_type=F32)

    def finish(c, o):
        r = slice(c * rows, (c + 1) * rows)
        x1 = x_ref[r, :] + gate_ref[...] * o
        x1_o[r, :] = x1
        h = _rms(x1, g_ref[...]) * (1.0 + sc_ref[...]) + sh_ref[...]
        hb = h.astype(BF16)
        h2_o[r, :] = hb
        lg = jnp.dot(hb, wr_ref[...], preferred_element_type=F32)
        for k in range(rows // 128):
            col = c * rows + k * 128
            lg_o[:, col:col + 128] = lg[k * 128:(k + 1) * 128, :].T[:N_EXPERTS, :]

    pending = project(0)
    for c in range(groups):
        following = project(c + 1) if c + 1 < groups else None
        finish(c, pending)
        pending = following


def _output_side(o_a, o_b, o_c, x, mod, layer, wts):
    b, n, d = x.shape
    tm = min(n, OUT_GROUPS * TOKEN_TILE)
    mb = mod.shape[0]
    bsel = (lambda i: i) if mb > 1 else (lambda i: 0)

    def modspec(k):
        return pl.BlockSpec((None, None, 1, d), lambda bi, i: (bsel(bi), k, 0, 0))

    def tok(w):
        return pl.BlockSpec((None, tm, w), lambda bi, i: (bi, i, 0))

    return pl.pallas_call(
        _out_body,
        grid=(b, n // tm),
        in_specs=[
            tok(WIDTH_A), tok(WIDTH_B), tok(WIDTH_C),
            _resident((None, WIDTH_A + WIDTH_B + WIDTH_C, d), lambda bi, i: (layer, 0, 0)),
            tok(d), modspec(2), modspec(3), modspec(4),
            pl.BlockSpec((None, 1, d), lambda bi, i: (layer, 0, 0)),
            pl.BlockSpec((None, d, 128), lambda bi, i: (layer, 0, 0)),
        ],
        out_specs=[tok(d), tok(d), pl.BlockSpec((None, N_EXPERTS, tm), lambda bi, i: (bi, 0, i))],
        out_shape=[
            jax.ShapeDtypeStruct((b, n, d), F32),
            jax.ShapeDtypeStruct((b, n, d), BF16),
            jax.ShapeDtypeStruct((b, N_EXPERTS, n), F32),
        ],
        compiler_params=_params("parallel", "parallel"),
        name="output_side",
    )(o_a, o_b, o_c, wts["w_out"], x, mod, mod, mod, wts["norm2_g"], wts["w_router"])


def _prefix_exclusive(mask):
    e, n = mask.shape
    ones = jnp.where(mask, 1.0, 0.0)
    rr = lax.broadcasted_iota(jnp.int32, (128, 128), 0)
    cc = lax.broadcasted_iota(jnp.int32, (128, 128), 1)
    tri = jnp.where(rr <= cc, 1.0, 0.0).astype(BF16)
    carry = jnp.zeros((e, 1), F32)
    outs = []
    for c in range(n // 128):
        blk = ones[:, c * 128:(c + 1) * 128]
        inc = jnp.dot(blk.astype(BF16), tri, preferred_element_type=F32)
        outs.append(inc - blk + carry)
        carry = carry + inc[:, 127:128]
    return jnp.concatenate(outs, axis=1)


def _router_body(lg_ref, rank_o, rank_t_o, aff_t_o, cum_o, *, cap, chunk):
    bb, e, n = lg_ref.shape
    lg = lg_ref[...]
    ex = jnp.exp(lg - jnp.max(lg, axis=1, keepdims=True))
    aff = (ex / jnp.sum(ex, axis=1, keepdims=True)).reshape(bb * e, n)
    key = pltpu.bitcast(aff, jnp.int32)

    def step(i, t):
        cand = t | lax.shift_left(jnp.int32(1), 30 - i)
        cnt = jnp.sum(jnp.where(key >= cand, 1.0, 0.0), axis=1, keepdims=True)
        return jnp.where(cnt >= cap, cand, t)

    thr = lax.fori_loop(0, 31, step, jnp.zeros((bb * e, 1), jnp.int32))
    above = key > thr
    tied = key == thr
    need = cap - jnp.sum(jnp.where(above, 1.0, 0.0), axis=1, keepdims=True)
    chosen = above | (tied & (_prefix_exclusive(tied) < need))
    before = _prefix_exclusive(chosen)
    rank = jnp.where(chosen, before, -1.0)
    rank_o[...] = rank.astype(jnp.int32).reshape(bb, e, n)

    lane = lax.broadcasted_iota(jnp.int32, (bb * e, 128), 1)
    cum = jnp.full((bb * e, 128), float(cap), F32)
    for k in range(n // chunk):
        cum = jnp.where(lane == k, before[:, k * chunk:k * chunk + 1], cum)
    cum_o[...] = cum.astype(jnp.int32).reshape(bb, e, 128)

    fill = jnp.full((128 - e, n), -1.0, F32)
    for bi in range(bb):
        rank_p = jnp.concatenate([rank[bi * e:(bi + 1) * e], fill], axis=0)
        aff_p = jnp.concatenate([aff[bi * e:(bi + 1) * e], fill], axis=0)
        for c in range(n // 128):
            rank_t_o[bi, c * 128:(c + 1) * 128, :] = rank_p[:, c * 128:(c + 1) * 128].T
            aff_t_o[bi, c * 128:(c + 1) * 128, :] = aff_p[:, c * 128:(c + 1) * 128].T


def _route(logits_t):
    b, e, n = logits_t.shape
    cap = CAPACITY_FACTOR * n // e
    bb = max(1, min(b, ROUTE_TOKENS // n))
    return pl.pallas_call(
        functools.partial(_router_body, cap=cap, chunk=SCATTER_TOKENS),
        grid=(b // bb,),
        in_specs=[pl.BlockSpec((bb, e, n), lambda bi: (bi, 0, 0))],
        out_specs=[
            pl.BlockSpec((bb, e, n), lambda bi: (bi, 0, 0)),
            pl.BlockSpec((bb, n, 128), lambda bi: (bi, 0, 0)),
            pl.BlockSpec((bb, n, 128), lambda bi: (bi, 0, 0)),
            pl.BlockSpec((bb, e, 128), lambda bi: (bi, 0, 0)),
        ],
        out_shape=[
            jax.ShapeDtypeStruct((b, e, n), jnp.int32),
            jax.ShapeDtypeStruct((b, n, 128), F32),
            jax.ShapeDtypeStruct((b, n, 128), F32),
            jax.ShapeDtypeStruct((b, e, 128), jnp.int32),
        ],
        compiler_params=_params("parallel"),
        name="route",
    )(logits_t)


def _gather_body(cum_ref, h_ref, rank_ref, xe_o, *, chunk, rows):
    n, d = h_ref.shape
    experts, cap, _ = xe_o.shape
    nch = n // chunk
    per_chunk = chunk // SCATTER_TOKENS
    first = (pl.program_id(0) * pl.num_programs(1) + pl.program_id(1)) * experts
    for k in range(experts):
        rank = rank_ref[k]
        if nch == 1 and cap == rows:
            slot = lax.broadcasted_iota(jnp.int32, (rows, chunk), 0)
            onehot = jnp.where(slot == rank, 1.0, 0.0).astype(BF16)
            xe_o[k] = jnp.dot(onehot, h_ref[...], preferred_element_type=F32).astype(xe_o.dtype)
            continue
        tiles = n // SCATTER_TOKENS
        base = (first + k) * (tiles + 1)
        window = min(n, GATHER_WINDOW)
        for jb in range(cap // rows):
            lo_slot, hi_slot = jb * rows, (jb + 1) * rows
            before = jnp.int32(0)
            upto = jnp.int32(0)
            for t in range(tiles):
                before += (cum_ref[base + t + 1] <= lo_slot).astype(jnp.int32)
                upto += (cum_ref[base + t] < hi_slot).astype(jnp.int32)
            start = jnp.minimum(before * SCATTER_TOKENS, n - window)
            start = pl.multiple_of(start, SCATTER_TOKENS)
            fits = upto * SCATTER_TOKENS <= start + window

            @pl.when(fits)
            def _():
                slot = lax.broadcasted_iota(jnp.int32, (rows, window), 0) + lo_slot
                onehot = jnp.where(slot == rank_ref[k, :, pl.ds(start, window)], 1.0, 0.0).astype(BF16)
                picked = jnp.dot(onehot, h_ref[pl.ds(start, window), :], preferred_element_type=F32)
                xe_o[k, lo_slot:hi_slot, :] = picked.astype(xe_o.dtype)

            @pl.when(jnp.logical_not(fits))
            def _():
                xe_o[k, lo_slot:hi_slot, :] = jnp.zeros((rows, d), xe_o.dtype)
                for c in range(nch):
                    lo = cum_ref[base + c * per_chunk]
                    hi = cum_ref[base + (c + 1) * per_chunk]

                    @pl.when((lo < hi_slot) & (hi > lo_slot))
                    def _():
                        slot = lax.broadcasted_iota(jnp.int32, (rows, chunk), 0) + lo_slot
                        onehot = jnp.where(slot == rank[:, c * chunk:(c + 1) * chunk], 1.0, 0.0).astype(BF16)
                        picked = jnp.dot(onehot, h_ref[c * chunk:(c + 1) * chunk, :], preferred_element_type=F32)
                        xe_o[k, lo_slot:hi_slot, :] += picked.astype(xe_o.dtype)


def _gather(h2, rank, cum_flat):
    b, n, d = h2.shape
    e = rank.shape[1]
    cap = CAPACITY_FACTOR * n // e
    chunk = min(n, GATHER_CHUNK)
    per_step = e if n <= chunk else 1
    return pl.pallas_call(
        functools.partial(_gather_body, chunk=chunk, rows=min(cap, GATHER_ROWS)),
        grid_spec=pltpu.PrefetchScalarGridSpec(
            num_scalar_prefetch=1,
            grid=(b, e // per_step),
            in_specs=[
                pl.BlockSpec((None, n, d), lambda bi, ei, cum_ref: (bi, 0, 0)),
                pl.BlockSpec((None, per_step, 1, n), lambda bi, ei, cum_ref: (bi, ei, 0, 0)),
            ],
            out_specs=pl.BlockSpec((per_step, None, cap, d), lambda bi, ei, cum_ref: (ei, bi, 0, 0)),
        ),
        out_shape=jax.ShapeDtypeStruct((e, b, cap, d), BF16),
        compiler_params=_params("parallel", "parallel"),
        name="gather",
    )(cum_flat, h2, rank)


def _ffn_body(xe_ref, wg_ref, wu_ref, wd_ref, ye_o):
    bb, cap, d = xe_ref.shape
    xe = xe_ref[...].reshape(bb * cap, d)
    g = jnp.dot(xe, wg_ref[...], preferred_element_type=F32)
    u = jnp.dot(xe, wu_ref[...], preferred_element_type=F32)
    hid = (g / (1.0 + jnp.exp(-g)) * u).astype(BF16)
    ye = jnp.dot(hid, wd_ref[...], preferred_element_type=F32)
    ye_o[...] = ye.astype(ye_o.dtype).reshape(bb, cap, d)


def _expert_ffn(xe, w_gate, w_up, w_down):
    e, b, cap, d = xe.shape
    ff = w_gate.shape[-1]
    bb = max(1, min(b, 512 // cap))
    return pl.pallas_call(
        _ffn_body,
        grid=(e, b // bb),
        in_specs=[
            pl.BlockSpec((None, bb, cap, d), lambda ei, bi: (ei, bi, 0, 0)),
            pl.BlockSpec((None, d, ff), lambda ei, bi: (ei, 0, 0)),
            pl.BlockSpec((None, d, ff), lambda ei, bi: (ei, 0, 0)),
            pl.BlockSpec((None, ff, d), lambda ei, bi: (ei, 0, 0)),
        ],
        out_specs=pl.BlockSpec((None, bb, cap, d), lambda ei, bi: (ei, bi, 0, 0)),
        out_shape=jax.ShapeDtypeStruct((e, b, cap, d), BF16),
        compiler_params=_params("parallel", "arbitrary"),
        name="expert_ffn",
    )(xe, w_gate, w_up, w_down)


def _ffn_cast_body(xe_ref, wg_ref, wu_ref, wd_ref, ye_o, wg_o, wu_o, wd_o, acc):
    f = pl.program_id(1)
    bb, cap, d = xe_ref.shape
    wg = wg_ref[...].astype(BF16)
    wu = wu_ref[...].astype(BF16)
    wd = wd_ref[...].astype(BF16)
    wg_o[...] = wg
    wu_o[...] = wu
    wd_o[...] = wd
    xe = xe_ref[...].reshape(bb * cap, d)
    g = jnp.dot(xe, wg, preferred_element_type=F32)
    u = jnp.dot(xe, wu, preferred_element_type=F32)
    hid = (g / (1.0 + jnp.exp(-g)) * u).astype(BF16)
    part = jnp.dot(hid, wd, preferred_element_type=F32)

    @pl.when(f == 0)
    def _():
        acc[...] = part

    @pl.when(f > 0)
    def _():
        acc[...] += part

    @pl.when(f == pl.num_programs(1) - 1)
    def _():
        ye_o[...] = acc[...].astype(ye_o.dtype).reshape(bb, cap, d)


def _expert_ffn_cast(xe, layer, w_gate, w_up, w_down):
    e, b, cap, d = xe.shape
    ff = w_gate.shape[-1]
    ffs = FF_SLICE
    return pl.pallas_call(
        _ffn_cast_body,
        grid=(e, ff // ffs),
        in_specs=[
            pl.BlockSpec((None, b, cap, d), lambda ei, f: (ei, 0, 0, 0)),
            pl.BlockSpec((None, None, d, ffs), lambda ei, f: (layer, ei, 0, f)),
            pl.BlockSpec((None, None, d, ffs), lambda ei, f: (layer, ei, 0, f)),
            pl.BlockSpec((None, None, ffs, d), lambda ei, f: (layer, ei, f, 0)),
        ],
        out_specs=[
            pl.BlockSpec((None, b, cap, d), lambda ei, f: (ei, 0, 0, 0)),
            pl.BlockSpec((None, d, ffs), lambda ei, f: (ei, 0, f)),
            pl.BlockSpec((None, d, ffs), lambda ei, f: (ei, 0, f)),
            pl.BlockSpec((None, ffs, d), lambda ei, f: (ei, f, 0)),
        ],
        out_shape=[
            jax.ShapeDtypeStruct((e, b, cap, d), BF16),
            jax.ShapeDtypeStruct((e, d, ff), BF16),
            jax.ShapeDtypeStruct((e, d, ff), BF16),
            jax.ShapeDtypeStruct((e, ff, d), BF16),
        ],
        scratch_shapes=[pltpu.VMEM((b * cap, d), F32)],
        compiler_params=_params("parallel", "arbitrary"),
        name="expert_ffn_cast",
    )(xe, w_gate, w_up, w_down)


def _scatter_body(cum_ref, ye_ref, x_ref, rank_ref, aff_ref, gate_ref, o_ref):
    e, cap, dh = ye_ref.shape
    tn = x_ref.shape[0]
    sub_tiles = tn // SCATTER_TOKENS
    win = min(cap, 2 * SCATTER_TOKENS)
    per_expert = pl.num_programs(2) * sub_tiles + 1
    col = lax.broadcasted_iota(jnp.int32, (SCATTER_TOKENS, win), 1).astype(F32)
    for sub in range(sub_tiles):
        rows = slice(sub * SCATTER_TOKENS, (sub + 1) * SCATTER_TOKENS)
        acc = jnp.zeros((SCATTER_TOKENS, dh), F32)
        for ei in range(e):
            rank = rank_ref[rows, ei:ei + 1]
            if cap > win:
                first = cum_ref[(pl.program_id(0) * e + ei) * per_expert + pl.program_id(2) * sub_tiles + sub]
                start = jnp.clip((first // SCATTER_TOKENS) * SCATTER_TOKENS, 0, cap - win)
                start = pl.multiple_of(start, SCATTER_TOKENS)
                onehot = jnp.where(rank - start.astype(F32) == col, 1.0, 0.0).astype(BF16)
                contrib = jnp.dot(onehot, ye_ref[ei, pl.ds(start, win), :], preferred_element_type=F32)
            else:
                onehot = jnp.where(rank == col, 1.0, 0.0).astype(BF16)
                contrib = jnp.dot(onehot, ye_ref[ei], preferred_element_type=F32)
            acc = acc + contrib * aff_ref[rows, ei:ei + 1]
        o_ref[rows, :] = x_ref[rows, :] + gate_ref[...] * acc


def _scatter(ye, x1, rank_t, aff_t, mod, cum_flat):
    e, b, cap, d = ye.shape
    n = x1.shape[1]
    tn = min(n, 512)
    dh = d if e * cap * d * 2 <= SCATTER_YE_BYTES else d // 2
    mb = mod.shape[0]
    bsel = (lambda i: i) if mb > 1 else (lambda i: 0)
    return pl.pallas_call(
        _scatter_body,
        grid_spec=pltpu.PrefetchScalarGridSpec(
            num_scalar_prefetch=1,
            grid=(b, d // dh, n // tn),
            in_specs=[
                pl.BlockSpec((e, None, cap, dh), lambda bi, j, i, cum_ref: (0, bi, 0, j)),
                pl.BlockSpec((None, tn, dh), lambda bi, j, i, cum_ref: (bi, i, j)),
                pl.BlockSpec((None, tn, 128), lambda bi, j, i, cum_ref: (bi, i, 0)),
                pl.BlockSpec((None, tn, 128), lambda bi, j, i, cum_ref: (bi, i, 0)),
                pl.BlockSpec((None, None, 1, dh), lambda bi, j, i, cum_ref: (bsel(bi), 5, 0, j)),
            ],
            out_specs=pl.BlockSpec((None, tn, dh), lambda bi, j, i, cum_ref: (bi, i, j)),
        ),
        out_shape=jax.ShapeDtypeStruct(x1.shape, F32),
        compiler_params=_params("parallel", "parallel", "parallel"),
        name="scatter",
    )(cum_flat, ye, x1, rank_t, aff_t, mod)


def _final_norm_body(x_ref, g_ref, o_ref):
    o_ref[...] = _rms(x_ref[...], g_ref[...])


def _final_norm(x, g):
    b, n, d = x.shape
    tm = TOKEN_TILE
    return pl.pallas_call(
        _final_norm_body,
        grid=(b, n // tm),
        in_specs=[pl.BlockSpec((None, tm, d), lambda bi, i: (bi, i, 0)), pl.BlockSpec((1, d), lambda bi, i: (0, 0))],
        out_specs=pl.BlockSpec((None, tm, d), lambda bi, i: (bi, i, 0)),
        out_shape=jax.ShapeDtypeStruct(x.shape, F32),
        compiler_params=_params("parallel", "parallel"),
        name="final_norm",
    )(x, g.reshape(1, d))


def _rope_tables(n_tokens, rot_dim):
    t = jnp.arange(n_tokens, dtype=jnp.int32)
    row = (t // GRID_W).astype(F32)
    col = (t % GRID_W).astype(F32)
    axis_dim = rot_dim // 2
    freqs = ROPE_THETA ** (-jnp.arange(0, axis_dim, 2, dtype=F32) / axis_dim)
    ang = jnp.concatenate([row[:, None] * freqs[None, :], col[:, None] * freqs[None, :]], axis=-1)
    cos, sin = jnp.cos(ang), jnp.sin(ang)
    cos2 = jnp.repeat(cos, 2, axis=-1)
    sin2 = jnp.stack([-sin, sin], axis=-1).reshape(n_tokens, rot_dim)
    pad = 128 - rot_dim
    if pad:
        cos2 = jnp.concatenate([cos2, jnp.ones((n_tokens, pad), F32)], axis=-1)
        sin2 = jnp.concatenate([sin2, jnp.zeros((n_tokens, pad), F32)], axis=-1)
    return cos2, sin2


def _w_in_body(w_ref, o_ref):
    kr_lo = SEG_CKV[1]
    w = w_ref[...]
    rows = w.shape[0]
    o_ref[:, :kr_lo] = w[:, :kr_lo].astype(BF16)
    o_ref[:, kr_lo:SEG_KR[0]] = w[:, kr_lo + ROPE_DIM_B:].astype(BF16)
    o_ref[:, SEG_KR[0]:SEG_KR[0] + ROPE_DIM_B] = w[:, kr_lo:kr_lo + ROPE_DIM_B].astype(BF16)
    o_ref[:, SEG_KR[0] + ROPE_DIM_B:] = jnp.zeros((rows, 128 - ROPE_DIM_B), BF16)


def _reorder_w_in(w_in):
    depth, d, cols = w_in.shape
    rows = TOKEN_TILE
    return pl.pallas_call(
        _w_in_body,
        grid=(depth, d // rows),
        in_specs=[pl.BlockSpec((None, rows, cols), lambda l, i: (l, i, 0))],
        out_specs=pl.BlockSpec((None, rows, IN_COLS_P), lambda l, i: (l, i, 0)),
        out_shape=jax.ShapeDtypeStruct((depth, d, IN_COLS_P), BF16),
        compiler_params=_params("parallel", "parallel"),
        name="reorder_w_in",
    )(w_in)


def _prepare_weights(norm1_g, norm2_g, w_in, qa_norm_g, ka_norm_g, q_norm_b, kv_norm_b, w_q_up, w_kv_up, w_out,
                     w_router):
    depth, d, _ = w_in.shape
    w_in_p = _reorder_w_in(w_in)
    wq = w_q_up.reshape(depth, Q_RANK_B, HEADS_B, NOPE_DIM_B + ROPE_DIM_B)
    wq = jnp.pad(wq, ((0, 0), (0, 0), (0, 0), (0, QB_PAD - NOPE_DIM_B - ROPE_DIM_B)))
    wkv = w_kv_up.reshape(depth, KV_RANK_B, HEADS_B, NOPE_DIM_B + V_DIM_B)
    return {
        "norm1_g": norm1_g.reshape(depth, 1, d),
        "norm2_g": norm2_g.reshape(depth, 1, d),
        "w_in": w_in_p,
        "qa_g": qa_norm_g.reshape(depth, 1, HEAD_DIM),
        "ka_g": ka_norm_g.reshape(depth, 1, HEAD_DIM),
        "qn_g": q_norm_b.reshape(depth, 1, Q_RANK_B),
        "kvn_g": kv_norm_b.reshape(depth, 1, KV_RANK_B),
        "w_q_up": wq.reshape(depth, Q_RANK_B, HEADS_B * QB_PAD).astype(BF16),
        "w_kv_k": wkv[..., :NOPE_DIM_B].reshape(depth, KV_RANK_B, HEADS_B * NOPE_DIM_B).astype(BF16),
        "w_kv_v": wkv[..., NOPE_DIM_B:].reshape(depth, KV_RANK_B, HEADS_B * V_DIM_B).astype(BF16),
        "w_out": w_out.astype(BF16),
        "w_router": jnp.pad(w_router, ((0, 0), (0, 0), (0, 128 - N_EXPERTS))).astype(BF16),
    }


def _moe_select(h2, logits_t):
    rank, rank_t, aff_t, cum = _route(logits_t)
    n = rank.shape[2]
    cum_flat = cum[:, :, :n // SCATTER_TOKENS + 1].reshape(-1)
    xe = _gather(h2, rank.reshape(rank.shape[0], rank.shape[1], 1, n), cum_flat)
    return xe, rank_t, aff_t, cum_flat


def kernel(x_prompt, x_sample, cache_a_k, cache_a_v, cache_b_ckv, cache_b_krope, cache_c_k, cache_c_v, c, c_ctx, w_ada, b_ada, norm1_g, norm2_g, w_in, qa_norm_g, ka_norm_g, q_norm_b, kv_norm_b, w_q_up, w_kv_up, na_bias, w_out, w_router, w_gate, w_up, w_down, final_norm_g):
    depth, d, _ = w_in.shape
    bp, seq, _ = x_prompt.shape
    bs, n_lat, _ = x_sample.shape
    past = cache_a_k.shape[2]
    rows = n_lat // GRID_W

    wts = _prepare_weights(norm1_g, norm2_g, w_in, qa_norm_g, ka_norm_g, q_norm_b, kv_norm_b, w_q_up, w_kv_up,
                           w_out, w_router)
    rope_tabs = _rope_tables(n_lat, HEAD_DIM) + _rope_tables(n_lat, ROPE_DIM_B)

    cond8 = jnp.concatenate([c_ctx[None], c, jnp.zeros((8 - 1 - bs, d), F32)], axis=0)
    mods = _modulation(cond8, w_ada, b_ada)
    mods = mods.reshape(depth, 8, 6, 1, d)

    ca_k = cache_a_k.reshape(bs, depth, past, KV_A).astype(BF16)
    ca_v = jnp.swapaxes(cache_a_v.reshape(bs, depth, past, KV_A), 2, 3).astype(BF16)
    cc_k = cache_c_k.reshape(bs, depth, past, WIDTH_C).astype(BF16)
    cc_v = cache_c_v.reshape(bs, depth, past, WIDTH_C).astype(BF16)
    cb_k, cb_v = _cache_mla_kv(cache_b_ckv, cache_b_krope, wts["w_kv_k"], wts["w_kv_v"])
    nat_bias = _nat_bias(na_bias, rows)

    xp, xs = x_prompt, x_sample
    states = ()
    for l in range(depth):
        mod_c = mods[l, 0:1]
        mod_l = mods[l, 1:1 + bs]

        qa, ka, va, qb, kb, vb, qc, kc, vc, *states = _input_side(xp, mod_c, l, wts, None, states)
        o_a, o_b, o_c = _ctx_attention(qa, ka, va, qb, kb, vb, qc, kc, vc)
        x1, h2, lg = _output_side(o_a, o_b, o_c, xp, mod_c, l, wts)
        xe, rank_t, aff_t, cum_flat = _moe_select(h2, lg)
        ye, wg16, wu16, wd16 = _expert_ffn_cast(xe, l, w_gate, w_up, w_down)
        xp = _scatter(ye, x1, rank_t, aff_t, mod_c, cum_flat)

        qa, ka, va, qb, kb, vb, qc, kc, vc = _input_side(xs, mod_l, l, wts, rope_tabs, None)
        o_a = _attention(qa, ka, va, ca_k, ca_v, l, HEADS_A, HEADS_A // KV_HEADS_A, HEAD_DIM, HEAD_DIM)
        o_b = _attention(qb, kb, vb, cb_k, cb_v, l, HEADS_B, 1, QB_PAD, V_DIM_B)
        o_c = _neighbourhood_attention(qc, kc, vc, cc_k, cc_v, nat_bias, l)
        x1, h2, lg = _output_side(o_a, o_b, o_c, xs, mod_l, l, wts)
        xe, rank_t, aff_t, cum_flat = _moe_select(h2, lg)
        ye = _expert_ffn(xe, wg16, wu16, wd16)
        xs = _scatter(ye, x1, rank_t, aff_t, mod_l, cum_flat)

    y_prompt = _final_norm(xp, final_norm_g)
    y_sample = _final_norm(xs, final_norm_g)
    st_ka, st_va, new_b_ckv, new_b_krope, st_kc, st_vc = states
    new_a_k = st_ka.reshape(bp, depth, seq, KV_HEADS_A, HEAD_DIM)
    new_a_v = st_va.reshape(bp, depth, seq, KV_HEADS_A, HEAD_DIM)
    new_c_k = st_kc.reshape(bp, depth, seq, HEADS_C, HEAD_DIM)
    new_c_v = st_vc.reshape(bp, depth, seq, HEADS_C, HEAD_DIM)
    return (y_prompt, y_sample, new_a_k, new_a_v, new_b_ckv, new_b_krope, new_c_k, new_c_v)
```
